```python
import math
import jax, jax.numpy as jnp
from jax import lax
import numpy as np

D_MODEL = 2048
BATCH = 4
SEQ = 2048
DEPTH = 1

N_META = 16
MLA_HEADS = 8
QK_NOPE_DIM = 128
QK_ROPE_DIM = 64
QK_HEAD_DIM = QK_NOPE_DIM + QK_ROPE_DIM
V_HEAD_DIM = 128
Q_LORA_RANK = 768
KV_LORA_RANK = 512
ROPE_THETA = 10000.0
Q_BLOCK = 128
MLA_WIDTH = MLA_HEADS * V_HEAD_DIM
CONV_CHANNELS = D_MODEL // 2
CONV_WIDTH = 31
MIX_WIDTH = MLA_WIDTH + CONV_CHANNELS
IN_PROJ_WIDTH = Q_LORA_RANK + KV_LORA_RANK + QK_ROPE_DIM + 2 * CONV_CHANNELS
N_EXPERTS = 32
TOP_K = 4
D_FF = D_MODEL
SWIGLU_LIMIT = 7.0
SWIGLU_ALPHA = 1.702
EXPERT_BLOCK = 128
DEEPNORM_ALPHA = (2.0 * DEPTH) ** 0.25
DEEPNORM_BETA = (8.0 * DEPTH) ** -0.25
LN_EPS = 1e-5
RMS_EPS = 1e-6

kernel_name = "hymba_mla_conformer_moe_deepnorm"


def layer_norm(x, g, b):
    xf = x.astype(jnp.float32)
    mu = jnp.mean(xf, axis=-1, keepdims=True)
    var = jnp.mean(jnp.square(xf - mu), axis=-1, keepdims=True)
    return ((xf - mu) * lax.rsqrt(var + LN_EPS)).astype(x.dtype) * g + b


def rms_norm(x, g):
    xf = x.astype(jnp.float32)
    ms = jnp.mean(jnp.square(xf), axis=-1, keepdims=True)
    return (xf * lax.rsqrt(ms + RMS_EPS)).astype(x.dtype) * g


def rope_tables(length, dtype):
    inv_freq = 1.0 / (ROPE_THETA ** (jnp.arange(0, QK_ROPE_DIM, 2, dtype=jnp.float32) / QK_ROPE_DIM))
    pos = jnp.arange(length, dtype=jnp.float32)
    freqs = pos[:, None] * inv_freq[None, :]
    emb = jnp.concatenate([freqs, freqs], axis=-1)
    return jnp.cos(emb).astype(dtype), jnp.sin(emb).astype(dtype)


def apply_rope(x, cos, sin):
    half = QK_ROPE_DIM // 2
    x1, x2 = x[..., :half], x[..., half:]
    rot = jnp.concatenate([-x2, x1], axis=-1)
    return x * cos + rot * sin


def mla_group(c_q, c_kv, k_pe, q_norm_g, w_uq, kv_norm_g, w_uk, w_uv, cos, sin):
    B, L, _ = c_q.shape
    q = (rms_norm(c_q, q_norm_g) @ w_uq).reshape(B, L, MLA_HEADS, QK_HEAD_DIM)
    q_nope, q_pe = q[..., :QK_NOPE_DIM], q[..., QK_NOPE_DIM:]
    q_pe = apply_rope(q_pe, cos[:, None, :], sin[:, None, :])
    ckv = rms_norm(c_kv, kv_norm_g)
    k_nope = (ckv @ w_uk).reshape(B, L, MLA_HEADS, QK_NOPE_DIM)
    v = (ckv @ w_uv).reshape(B, L, MLA_HEADS, V_HEAD_DIM)
    k_pe = apply_rope(k_pe, cos, sin)
    q = jnp.concatenate([q_nope, q_pe], axis=-1)
    k = jnp.concatenate([k_nope, jnp.broadcast_to(k_pe[:, :, None, :], (B, L, MLA_HEADS, QK_ROPE_DIM))], axis=-1)

    n_blocks = -(-L // Q_BLOCK)
    L_pad = n_blocks * Q_BLOCK
    pad = [(0, 0), (0, L_pad - L), (0, 0), (0, 0)]
    q, k, v = jnp.pad(q, pad), jnp.pad(k, pad), jnp.pad(v, pad)
    q_blocks = q.reshape(B, n_blocks, Q_BLOCK, MLA_HEADS, QK_HEAD_DIM).transpose(1, 0, 2, 3, 4)
    k_pos = jnp.arange(L_pad)
    scale = 1.0 / math.sqrt(QK_HEAD_DIM)

    def attend(args):
        qb, start = args
        s = jnp.einsum('bqhd,bkhd->bhqk', qb, k).astype(jnp.float32) * scale
        q_pos = start + jnp.arange(Q_BLOCK)
        mask = k_pos[None, :] <= q_pos[:, None]
        s = jnp.where(mask[None, None], s, -1e30)
        p = jax.nn.softmax(s, axis=-1).astype(v.dtype)
        return jnp.einsum('bhqk,bkhd->bqhd', p, v)

    out = lax.map(attend, (q_blocks, jnp.arange(n_blocks) * Q_BLOCK))
    out = out.transpose(1, 0, 2, 3, 4).reshape(B, L_pad, MLA_WIDTH)
    return out[:, :L]


def conv_group(u, dw_w, dw_b, ln_g, ln_b):
    a, gate = u[..., :CONV_CHANNELS], u[..., CONV_CHANNELS:]
    h = a * jax.nn.sigmoid(gate)
    h = lax.conv_general_dilated(
        h, dw_w[:, None, :], window_strides=(1,), padding=[(CONV_WIDTH - 1, 0)],
        dimension_numbers=('NWC', 'WIO', 'NWC'), feature_group_count=CONV_CHANNELS) + dw_b
    h = layer_norm(h, ln_g, ln_b)
    return jax.nn.silu(h)


def moe(h, w_router, b_router, w1, b1, w2, b2):
    B, L, D = h.shape
    xt = h.reshape(-1, D)
    T = xt.shape[0]
    logits = (xt @ w_router + b_router).astype(jnp.float32)
    top_vals, top_idx = lax.top_k(logits, TOP_K)
    gates = jax.nn.softmax(top_vals, axis=-1)

    TK = T * TOP_K
    expert_flat = top_idx.reshape(-1).astype(jnp.int32)
    token_flat = (jnp.arange(TK, dtype=jnp.int32) // TOP_K)
    gate_flat = gates.reshape(-1)
    order = jnp.argsort(expert_flat)
    e_sorted = expert_flat[order]
    counts = jnp.bincount(expert_flat, length=N_EXPERTS)
    starts = jnp.cumsum(counts) - counts
    padded = (counts + EXPERT_BLOCK - 1) // EXPERT_BLOCK * EXPERT_BLOCK
    pad_ends = jnp.cumsum(padded)
    pad_starts = pad_ends - padded
    dest = pad_starts[e_sorted] + (jnp.arange(TK) - starts[e_sorted])

    n_blocks = -(-TK // EXPERT_BLOCK) + N_EXPERTS
    n_rows = n_blocks * EXPERT_BLOCK
    row_token = jnp.zeros((n_rows,), jnp.int32).at[dest].set(token_flat[order])
    row_gate = jnp.zeros((n_rows,), jnp.float32).at[dest].set(gate_flat[order])
    block_expert = jnp.minimum(
        jnp.searchsorted(pad_ends, jnp.arange(n_blocks) * EXPERT_BLOCK, side='right'), N_EXPERTS - 1)

    def expert_block(args):
        tok, e = args
        xb = xt[tok]
        z = xb @ w1[e] + b1[e]
        g, u = z[:, :D_FF], z[:, D_FF:]
        g = jnp.minimum(g, SWIGLU_LIMIT)
        u = jnp.clip(u, -SWIGLU_LIMIT, SWIGLU_LIMIT)
        act = g * jax.nn.sigmoid(SWIGLU_ALPHA * g) * (u + 1.0)
        return act @ w2[e] + b2[e]

    out = lax.map(expert_block, (row_token.reshape(n_blocks, EXPERT_BLOCK), block_expert))
    out = out.reshape(n_rows, D) * row_gate[:, None].astype(out.dtype)
    y = jnp.zeros_like(xt).at[row_token].add(out)
    return y.reshape(B, L, D)


def setup_inputs(seed: int = 0) -> dict:
    key = jax.random.key(seed)
    ks = jax.random.split(key, 32)
    f32 = jnp.float32

    def nrm(k, shape, fan_in, scale=1.0):
        return jax.random.normal(k, shape, f32) * (scale * fan_in ** -0.5)

    def gain(k, shape):
        return 1.0 + 0.02 * jax.random.normal(k, shape, f32)

    def bias(k, shape, s=0.02):
        return s * jax.random.normal(k, shape, f32)

    Dp = DEPTH
    return {
        "x": jax.random.normal(ks[0], (BATCH, SEQ, D_MODEL), f32),
        "meta_tokens": jax.random.normal(ks[1], (N_META, D_MODEL), f32),
        "ln_in_g": gain(ks[2], (D_MODEL,)),
        "ln_in_b": bias(ks[3], (D_MODEL,)),
        "w_in": nrm(ks[4], (Dp, D_MODEL, IN_PROJ_WIDTH), D_MODEL),
        "q_norm_g": gain(ks[5], (Dp, Q_LORA_RANK)),
        "w_uq": nrm(ks[6], (Dp, Q_LORA_RANK, MLA_HEADS * QK_HEAD_DIM), Q_LORA_RANK),
        "kv_norm_g": gain(ks[7], (Dp, KV_LORA_RANK)),
        "w_uk": nrm(ks[8], (Dp, KV_LORA_RANK, MLA_HEADS * QK_NOPE_DIM), KV_LORA_RANK),
        "w_uv": nrm(ks[9], (Dp, KV_LORA_RANK, MLA_HEADS * V_HEAD_DIM), KV_LORA_RANK, DEEPNORM_BETA),
        "conv_dw_w": nrm(ks[10], (Dp, CONV_WIDTH, CONV_CHANNELS), CONV_WIDTH),
        "conv_dw_b": bias(ks[11], (Dp, CONV_CHANNELS)),
        "conv_ln_g": gain(ks[12], (Dp, CONV_CHANNELS)),
        "conv_ln_b": bias(ks[13], (Dp, CONV_CHANNELS)),
        "w_out": nrm(ks[14], (Dp, MIX_WIDTH, D_MODEL), MIX_WIDTH, DEEPNORM_BETA),
        "ln1_g": gain(ks[15], (Dp, D_MODEL)),
        "ln1_b": bias(ks[16], (Dp, D_MODEL)),
        "w_router": nrm(ks[17], (Dp, D_MODEL, N_EXPERTS), D_MODEL),
        "b_router": bias(ks[18], (Dp, N_EXPERTS), 0.01),
        "w_mlp1": nrm(ks[19], (Dp, N_EXPERTS, D_MODEL, 2 * D_FF), D_MODEL),
        "b_mlp1": bias(ks[20], (Dp, N_EXPERTS, 2 * D_FF)),
        "w_mlp2": nrm(ks[21], (Dp, N_EXPERTS, D_FF, D_MODEL), D_FF, DEEPNORM_BETA),
        "b_mlp2": bias(ks[22], (Dp, N_EXPERTS, D_MODEL)),
        "ln2_g": gain(ks[23], (Dp, D_MODEL)),
        "ln2_b": bias(ks[24], (Dp, D_MODEL)),
    }


def reference(x, meta_tokens, ln_in_g, ln_in_b, w_in, q_norm_g, w_uq, kv_norm_g, w_uk, w_uv,
              conv_dw_w, conv_dw_b, conv_ln_g, conv_ln_b, w_out, ln1_g, ln1_b,
              w_router, b_router, w_mlp1, b_mlp1, w_mlp2, b_mlp2, ln2_g, ln2_b):
    B = x.shape[0]
    meta = jnp.broadcast_to(meta_tokens[None].astype(x.dtype), (B, N_META, D_MODEL))
    h = jnp.concatenate([meta, x], axis=1)
    h = layer_norm(h, ln_in_g, ln_in_b)
    L = h.shape[1]
    cos, sin = rope_tables(L, h.dtype)
    s1 = Q_LORA_RANK
    s2 = s1 + KV_LORA_RANK
    s3 = s2 + QK_ROPE_DIM
    for l in range(DEPTH):
        proj = h @ w_in[l]
        c_q, c_kv, k_pe, u_conv = proj[..., :s1], proj[..., s1:s2], proj[..., s2:s3], proj[..., s3:]
        attn = mla_group(c_q, c_kv, k_pe, q_norm_g[l], w_uq[l], kv_norm_g[l], w_uk[l], w_uv[l], cos, sin)
        conv = conv_group(u_conv, conv_dw_w[l], conv_dw_b[l], conv_ln_g[l], conv_ln_b[l])
        mix = jnp.concatenate([attn, conv], axis=-1) @ w_out[l]
        h = layer_norm(DEEPNORM_ALPHA * h + mix, ln1_g[l], ln1_b[l])
        ffn = moe(h, w_router[l], b_router[l], w_mlp1[l], b_mlp1[l], w_mlp2[l], b_mlp2[l])
        h = layer_norm(DEEPNORM_ALPHA * h + ffn, ln2_g[l], ln2_b[l])
    return h[:, N_META:]
```

```python
import functools
import math

import jax
import jax.numpy as jnp
from jax import lax
from jax.experimental import pallas as pl
from jax.experimental.pallas import tpu as pltpu

D_MODEL = 2048
N_META = 16
N_HEADS = 8
QK_NOPE = 128
QK_ROPE = 64
QK_DIM = QK_NOPE + QK_ROPE
V_DIM = 128
Q_LORA = 768
KV_LORA = 512
ROPE_THETA = 10000.0
MLA_WIDTH = N_HEADS * V_DIM
CONV_CH = 1024
CONV_W = 31
N_EXPERTS = 32
TOP_K = 4
D_FF = 2048
SWIGLU_LIMIT = 7.0
SWIGLU_ALPHA = 1.702
DEEPNORM_ALPHA = 2.0 ** 0.25
LN_EPS = 1e-5
RMS_EPS = 1e-6

V7X_LANES = 128
V7X_VMEM_LIMIT = 56 * 1024 * 1024

ROW_TILE = 256
ATT_TILE = 256
CONV_TILE = 256
CONV_HALO = 32
CONV_ROWS = 32
CONV_LANES = 256
MOVE_TILE = 128
SEG_ALIGN = 128
SUPER_ROWS = 1536
FF_TILE = 256

F32 = jnp.float32
BF16 = jnp.bfloat16


def _dot(a, b):
    return jnp.dot(a, b, preferred_element_type=F32)


def _dot_nt(a, b):
    return lax.dot_general(a, b, (((1,), (1,)), ((), ())), preferred_element_type=F32)


def _layer_norm(x, g, b):
    mu = jnp.mean(x, axis=-1, keepdims=True)
    xc = x - mu
    var = jnp.mean(xc * xc, axis=-1, keepdims=True)
    return xc * lax.rsqrt(var + LN_EPS) * g + b


def _rms_norm(x, g):
    ms = jnp.mean(x * x, axis=-1, keepdims=True)
    return x * lax.rsqrt(ms + RMS_EPS) * g


def _const_spec(shape):
    zeros = (0,) * len(shape)
    return pl.BlockSpec(shape, lambda *_: zeros)


_C_Q = (0, Q_LORA)
_C_KV = (Q_LORA, Q_LORA + KV_LORA)
_C_KPE = (_C_KV[1], _C_KV[1] + 2 * QK_ROPE)
_C_A = (_C_KPE[1], _C_KPE[1] + CONV_CH)
_C_G = (_C_A[1], _C_A[1] + CONV_CH)
IN_AUG = _C_G[1]


def _in_proj_kernel(x_ref, lng_ref, lnb_ref, w1_ref, qg_ref, kvg_ref, wq_ref, wuk_ref, wuv_ref, cs_ref,
                    q_ref, k_ref, v_ref, glu_ref):
    h0 = _layer_norm(x_ref[...], lng_ref[...], lnb_ref[...])
    hb = h0.astype(BF16)
    cs = cs_ref[...]

    def rope(t128):
        t = t128 * cs
        return t + pltpu.roll(t, QK_ROPE, axis=1)

    cq = _dot(hb, w1_ref[:, _C_Q[0]:_C_Q[1]])
    cqn = _rms_norm(cq, qg_ref[...]).astype(BF16)
    ckv = _dot(hb, w1_ref[:, _C_KV[0]:_C_KV[1]])
    ckvn = _rms_norm(ckv, kvg_ref[...]).astype(BF16)
    kpe = rope(_dot(hb, w1_ref[:, _C_KPE[0]:_C_KPE[1]]))[:, :QK_ROPE].astype(BF16)

    a = _dot(hb, w1_ref[:, _C_A[0]:_C_A[1]])
    g = _dot(hb, w1_ref[:, _C_G[0]:_C_G[1]])
    glu_ref[...] = a * jax.nn.sigmoid(g)

    knope = _dot(ckvn, wuk_ref[...])
    v = _dot(ckvn, wuv_ref[...])
    for h in range(N_HEADS):
        k_ref[h, :, 0:QK_NOPE] = knope[:, h * QK_NOPE:(h + 1) * QK_NOPE].astype(BF16)
        k_ref[h, :, QK_NOPE:QK_DIM] = kpe
        v_ref[h] = v[:, h * V_DIM:(h + 1) * V_DIM].astype(BF16)

    qn = _dot(cqn, wq_ref[:, 0:N_HEADS * QK_NOPE])
    qp = _dot(cqn, wq_ref[:, N_HEADS * QK_NOPE:])
    for h in range(N_HEADS):
        q_ref[h, :, 0:QK_NOPE] = qn[:, h * QK_NOPE:(h + 1) * QK_NOPE].astype(BF16)
        q_ref[h, :, QK_NOPE:QK_DIM] = rope(qp[:, h * V7X_LANES:(h + 1) * V7X_LANES])[:, :QK_ROPE].astype(BF16)


def _in_proj(x2d, lng, lnb, w1, qg, kvg, wq, wuk, wuv, cs, tm):
    rows = x2d.shape[0]
    n_cs = cs.shape[0] // tm
    row = lambda i: (i, 0)
    head_row = lambda i: (0, i, 0)
    return pl.pallas_call(
        _in_proj_kernel,
        grid=(rows // tm,),
        in_specs=[
            pl.BlockSpec((tm, D_MODEL), row),
            _const_spec((1, D_MODEL)), _const_spec((1, D_MODEL)),
            _const_spec((D_MODEL, IN_AUG)),
            _const_spec((1, Q_LORA)), _const_spec((1, KV_LORA)),
            _const_spec((Q_LORA, 2 * N_HEADS * QK_NOPE)),
            _const_spec((KV_LORA, N_HEADS * QK_NOPE)), _const_spec((KV_LORA, MLA_WIDTH)),
            pl.BlockSpec((tm, V7X_LANES), lambda i: (i % n_cs, 0)),
        ],
        out_specs=[
            pl.BlockSpec((N_HEADS, tm, QK_DIM), head_row),
            pl.BlockSpec((N_HEADS, tm, QK_DIM), head_row),
            pl.BlockSpec((N_HEADS, tm, V_DIM), head_row),
            pl.BlockSpec((tm, CONV_CH), row),
        ],
        out_shape=[
            jax.ShapeDtypeStruct((N_HEADS, rows, QK_DIM), BF16),
            jax.ShapeDtypeStruct((N_HEADS, rows, QK_DIM), BF16),
            jax.ShapeDtypeStruct((N_HEADS, rows, V_DIM), BF16),
            jax.ShapeDtypeStruct((rows, CONV_CH), F32),
        ],
        compiler_params=pltpu.CompilerParams(
            dimension_semantics=("arbitrary",), vmem_limit_bytes=V7X_VMEM_LIMIT),
        name="in_proj",
    )(x2d, lng, lnb, w1, qg, kvg, wq, wuk, wuv, cs)


def _attention_kernel(q_ref, k_ref, v_ref, km_ref, vm_ref, o_ref):
    i = pl.program_id(2)
    scale = 1.0 / math.sqrt(QK_DIM)
    q = q_ref[0]

    s = _dot_nt(q, km_ref[0]) * scale
    m = jnp.max(s, axis=1, keepdims=True)
    p = jnp.exp(s - m)
    l = jnp.sum(p, axis=1, keepdims=True)
    acc = _dot(p.astype(BF16), vm_ref[0])

    def block(j, carry, masked):
        m, l, acc = carry
        start = pl.multiple_of(j * ATT_TILE, ATT_TILE)
        kb = k_ref[0, pl.ds(start, ATT_TILE), :]
        vb = v_ref[0, pl.ds(start, ATT_TILE), :]
        s = _dot_nt(q, kb) * scale
        if masked:
            r = lax.broadcasted_iota(jnp.int32, s.shape, 0)
            c = lax.broadcasted_iota(jnp.int32, s.shape, 1)
            s = jnp.where(c <= r, s, -1e30)
        m_new = jnp.maximum(m, jnp.max(s, axis=1, keepdims=True))
        alpha = jnp.exp(m - m_new)
        p = jnp.exp(s - m_new)
        l = alpha * l + jnp.sum(p, axis=1, keepdims=True)
        acc = alpha * acc + _dot(p.astype(BF16), vb)
        return m_new, l, acc

    carry = lax.fori_loop(0, i, lambda j, c: block(j, c, False), (m, l, acc))
    m, l, acc = block(i, carry, True)
    o_ref[...] = (acc / l).astype(BF16)


def _attention(q, k, v, km, vm, batch, seq):
    nq = seq // ATT_TILE
    return pl.pallas_call(
        _attention_kernel,
        grid=(batch, N_HEADS, nq),
        in_specs=[
            pl.BlockSpec((1, ATT_TILE, QK_DIM), lambda b, h, i: (h, b * nq + i, 0)),
            pl.BlockSpec((1, seq, QK_DIM), lambda b, h, i: (h, b, 0)),
            pl.BlockSpec((1, seq, V_DIM), lambda b, h, i: (h, b, 0)),
            pl.BlockSpec((1, N_META, QK_DIM), lambda b, h, i: (h, 0, 0)),
            pl.BlockSpec((1, N_META, V_DIM), lambda b, h, i: (h, 0, 0)),
        ],
        out_specs=pl.BlockSpec((ATT_TILE, V_DIM), lambda b, h, i: (b * nq + i, h)),
        out_shape=jax.ShapeDtypeStruct((batch * seq, MLA_WIDTH), BF16),
        compiler_params=pltpu.CompilerParams(
            dimension_semantics=("arbitrary", "arbitrary", "arbitrary"), vmem_limit_bytes=V7X_VMEM_LIMIT),
        name="attention",
    )(q, k, v, km, vm)


def _conv_kernel(cur_ref, prev_ref, meta_ref, w_ref, cb_ref, lng_ref, lnb_ref, o_ref, win_ref, acc_ref):
    i = pl.program_id(1)

    @pl.when(i == 0)
    def _():
        win_ref[0:CONV_HALO - N_META, :] = jnp.zeros((CONV_HALO - N_META, CONV_CH), F32)
        win_ref[CONV_HALO - N_META:CONV_HALO, :] = meta_ref[...]

    @pl.when(i > 0)
    def _():
        win_ref[0:CONV_HALO, :] = prev_ref[...]

    win_ref[CONV_HALO:, :] = cur_ref[...]

    base = CONV_HALO - (CONV_W - 1)

    for rc in range(CONV_TILE // CONV_ROWS):
        r0 = rc * CONV_ROWS
        for c in range(CONV_CH // CONV_LANES):
            lanes = pl.ds(c * CONV_LANES, CONV_LANES)
            acc = jnp.zeros((CONV_ROWS, CONV_LANES), F32)
            for k in range(CONV_W):
                acc = acc + win_ref[pl.ds(r0 + base + k, CONV_ROWS), lanes] * w_ref[k:k + 1, lanes]
            acc_ref[pl.ds(r0, CONV_ROWS), lanes] = acc

    y = _layer_norm(acc_ref[...] + cb_ref[...], lng_ref[...], lnb_ref[...])
    o_ref[...] = (y * jax.nn.sigmoid(y)).astype(BF16)


def _conv(glu, glu_meta, w, cb, lng, lnb, batch, seq):
    nt = seq // CONV_TILE
    per = CONV_TILE // CONV_HALO
    return pl.pallas_call(
        _conv_kernel,
        grid=(batch, nt),
        in_specs=[
            pl.BlockSpec((CONV_TILE, CONV_CH), lambda b, i: (b * nt + i, 0)),
            pl.BlockSpec((CONV_HALO, CONV_CH), lambda b, i: (jnp.maximum((b * nt + i) * per - 1, 0), 0)),
            _const_spec((N_META, CONV_CH)),
            _const_spec((CONV_HALO, CONV_CH)),
            _const_spec((1, CONV_CH)), _const_spec((1, CONV_CH)), _const_spec((1, CONV_CH)),
        ],
        out_specs=pl.BlockSpec((CONV_TILE, CONV_CH), lambda b, i: (b * nt + i, 0)),
        out_shape=jax.ShapeDtypeStruct((batch * seq, CONV_CH), BF16),
        scratch_shapes=[pltpu.VMEM((CONV_TILE + CONV_HALO, CONV_CH), F32),
                        pltpu.VMEM((CONV_TILE, CONV_CH), F32)],
        compiler_params=pltpu.CompilerParams(
            dimension_semantics=("arbitrary", "arbitrary"), vmem_limit_bytes=V7X_VMEM_LIMIT),
        name="conv",
    )(glu, glu, glu_meta, w, cb, lng, lnb)


def _out_proj_kernel(attn_ref, conv_ref, x_ref, lng_ref, lnb_ref, wo_ref, g1_ref, b1_ref,
                     wrh_ref, wrl_ref, br_ref,
                     h1_ref, idx_ref, rank_ref, gate_ref, cnt_ref, carry_ref):
    step = pl.program_id(0)
    tm = x_ref.shape[0]

    @pl.when(step == 0)
    def _():
        carry_ref[...] = jnp.zeros_like(carry_ref)

    h0 = _layer_norm(x_ref[...], lng_ref[...], lnb_ref[...])
    mix = _dot(attn_ref[...], wo_ref[0:MLA_WIDTH, :]) + _dot(conv_ref[...], wo_ref[MLA_WIDTH:, :])
    h1 = _layer_norm(DEEPNORM_ALPHA * h0 + mix, g1_ref[...], b1_ref[...])
    h1_ref[...] = h1

    hi = h1.astype(BF16)
    lo = (h1 - hi.astype(F32)).astype(BF16)
    logits = (_dot(hi, wrh_ref[...]) + (_dot(hi, wrl_ref[...]) + _dot(lo, wrh_ref[...]))) + br_ref[...]

    lane = lax.broadcasted_iota(jnp.int32, (tm, N_EXPERTS), 1)
    work = logits
    vals, idxs = [], []
    for _ in range(TOP_K):
        mx = jnp.max(work, axis=1, keepdims=True)
        ix = jnp.min(jnp.where(work == mx, lane, N_EXPERTS), axis=1, keepdims=True)
        vals.append(mx)
        idxs.append(ix)
        work = jnp.where(lane == ix, -jnp.inf, work)
    exps = [jnp.exp(v - vals[0]) for v in vals]
    denom = exps[0] + exps[1] + exps[2] + exps[3]

    onehots = [(lane == ix) for ix in idxs]
    chosen = (onehots[0] | onehots[1] | onehots[2] | onehots[3])
    chosen_f = jnp.where(chosen, 1.0, 0.0)
    r = lax.broadcasted_iota(jnp.int32, (tm, tm), 0)
    c = lax.broadcasted_iota(jnp.int32, (tm, tm), 1)
    lower = jnp.where(c < r, 1.0, 0.0).astype(BF16)
    before = _dot(lower, chosen_f.astype(BF16)) + carry_ref[...]

    out_lane = lax.broadcasted_iota(jnp.int32, (tm, V7X_LANES), 1)
    idx_out = jnp.zeros((tm, V7X_LANES), jnp.int32)
    rank_out = jnp.zeros((tm, V7X_LANES), jnp.int32)
    gate_out = jnp.zeros((tm, V7X_LANES), F32)
    for k in range(TOP_K):
        rank_k = jnp.sum(jnp.where(onehots[k], before, 0.0), axis=1, keepdims=True).astype(jnp.int32)
        idx_out = jnp.where(out_lane == k, idxs[k], idx_out)
        rank_out = jnp.where(out_lane == k, rank_k, rank_out)
        gate_out = jnp.where(out_lane == k, exps[k] / denom, gate_out)
    idx_ref[...] = idx_out
    rank_ref[...] = rank_out
    gate_ref[...] = gate_out

    carry_ref[...] = carry_ref[...] + jnp.sum(chosen_f, axis=0, keepdims=True)
    cnt_ref[...] = carry_ref[...].astype(jnp.int32)


def _out_proj(attn, conv, x2d, lng, lnb, wo, g1, b1, wrh, wrl, br):
    rows = x2d.shape[0]
    tm = ROW_TILE
    row = lambda i: (i, 0)
    return pl.pallas_call(
        _out_proj_kernel,
        grid=(rows // tm,),
        in_specs=[
            pl.BlockSpec((tm, MLA_WIDTH), row), pl.BlockSpec((tm, CONV_CH), row),
            pl.BlockSpec((tm, D_MODEL), row),
            _const_spec((1, D_MODEL)), _const_spec((1, D_MODEL)),
            _const_spec((D_MODEL, D_MODEL)),
            _const_spec((1, D_MODEL)), _const_spec((1, D_MODEL)),
            _const_spec((D_MODEL, N_EXPERTS)), _const_spec((D_MODEL, N_EXPERTS)),
            _const_spec((1, N_EXPERTS)),
        ],
        out_specs=[
            pl.BlockSpec((tm, D_MODEL), row),
            pl.BlockSpec((tm, V7X_LANES), row), pl.BlockSpec((tm, V7X_LANES), row),
            pl.BlockSpec((tm, V7X_LANES), row),
            _const_spec((1, N_EXPERTS)),
        ],
        out_shape=[
            jax.ShapeDtypeStruct((rows, D_MODEL), F32),
            jax.ShapeDtypeStruct((rows, V7X_LANES), jnp.int32),
            jax.ShapeDtypeStruct((rows, V7X_LANES), jnp.int32),
            jax.ShapeDtypeStruct((rows, V7X_LANES), F32),
            jax.ShapeDtypeStruct((1, N_EXPERTS), jnp.int32),
        ],
        scratch_shapes=[pltpu.VMEM((1, N_EXPERTS), F32)],
        compiler_params=pltpu.CompilerParams(
            dimension_semantics=("arbitrary",), vmem_limit_bytes=V7X_VMEM_LIMIT),
        name="out_proj_router",
    )(attn, conv, x2d, lng, lnb, wo, g1, b1, wrh, wrl, br)


def _row_copy(src_ref, src_row, dst_ref, dst_row, sem):
    return pltpu.make_async_copy(src_ref.at[pl.ds(src_row, 1), :], dst_ref.at[pl.ds(dst_row, 1), :], sem)


def _dispatch_kernel(dest_ref, padrow_ref, npad_ref, h1_ref, xs_ref, zero_ref, sem, zsem):
    step = pl.program_id(0)
    base = step * (MOVE_TILE * TOP_K)

    def start(a, _):
        _row_copy(h1_ref, a // TOP_K, xs_ref, dest_ref[base + a], sem).start()
        return 0

    lax.fori_loop(0, MOVE_TILE * TOP_K, start, 0)

    @pl.when(step == 0)
    def _():
        zero_ref[...] = jnp.zeros_like(zero_ref)
        n = npad_ref[0]
        tail_start = npad_ref[1]
        n_tail = (xs_ref.shape[0] - tail_start) // SEG_ALIGN

        def zstart(p, _):
            _row_copy(zero_ref, 0, xs_ref, padrow_ref[p], zsem).start()
            return 0

        def zwait(p, _):
            _row_copy(zero_ref, 0, xs_ref, padrow_ref[p], zsem).wait()
            return 0

        def tstart(b, _):
            _block_copy(zero_ref, 0, xs_ref, tail_start + b * SEG_ALIGN, zsem).start()
            return 0

        def twait(b, _):
            _block_copy(zero_ref, 0, xs_ref, tail_start + b * SEG_ALIGN, zsem).wait()
            return 0

        lax.fori_loop(0, n, zstart, 0)
        lax.fori_loop(0, n_tail, tstart, 0)
        lax.fori_loop(0, n, zwait, 0)
        lax.fori_loop(0, n_tail, twait, 0)

    def wait(a, _):
        _row_copy(h1_ref, a // TOP_K, xs_ref, dest_ref[base + a], sem).wait()
        return 0

    lax.fori_loop(0, MOVE_TILE * TOP_K, wait, 0)


def _dispatch(dest_flat, pad_rows, n_pad, h1, n_rows):
    tokens = h1.shape[0]
    return pl.pallas_call(
        _dispatch_kernel,
        grid_spec=pltpu.PrefetchScalarGridSpec(
            num_scalar_prefetch=3,
            grid=(tokens // MOVE_TILE,),
            in_specs=[pl.BlockSpec((MOVE_TILE, D_MODEL), lambda i, *_: (i, 0))],
            out_specs=pl.BlockSpec(memory_space=pl.ANY),
            scratch_shapes=[pltpu.VMEM((SEG_ALIGN, D_MODEL), F32),
                            pltpu.SemaphoreType.DMA, pltpu.SemaphoreType.DMA],
        ),
        out_shape=jax.ShapeDtypeStruct((n_rows, D_MODEL), F32),
        compiler_params=pltpu.CompilerParams(
            dimension_semantics=("arbitrary",), has_side_effects=True),
        name="dispatch",
    )(dest_flat, pad_rows, n_pad, h1)


N_FF = D_FF // FF_TILE
SUB = SEG_ALIGN
N_SUB = SUPER_ROWS // SUB


def _block_copy(src_ref, src_row, dst_ref, dst_row, sem):
    src = src_ref.at[pl.ds(pl.multiple_of(src_row, SUB), SUB), :]
    dst = dst_ref.at[pl.ds(pl.multiple_of(dst_row, SUB), SUB), :]
    return pltpu.make_async_copy(src, dst, sem)


def _experts_kernel(st_e_ref, st_start_ref, st_rows_ref, n_used_ref,
                    xs_ref, w1g_ref, w1u_ref, b1g_ref, b1u_ref, w2_ref, b2_ref,
                    ys_ref,
                    xb_ref, acc_ref, wg_ref, wu_ref, wd_ref, zero_ref, sem, zsem):
    s = pl.program_id(0)
    j = pl.program_id(1)

    @pl.when((s == 0) & (j == 0))
    def _():
        zero_ref[...] = jnp.zeros_like(zero_ref)
        tail_start = n_used_ref[1]
        n_tail = (ys_ref.shape[0] - tail_start) // SUB

        def tstart(b, _):
            _block_copy(zero_ref, 0, ys_ref, tail_start + b * SUB, zsem).start()
            return 0

        def twait(b, _):
            _block_copy(zero_ref, 0, ys_ref, tail_start + b * SUB, zsem).wait()
            return 0

        lax.fori_loop(0, n_tail, tstart, 0)
        lax.fori_loop(0, n_tail, twait, 0)
    rows = st_rows_ref[s]
    start = st_start_ref[s]
    n_sub = rows // SUB

    @pl.when((j == 0) & (rows > 0))
    def _():
        def issue(r, _):
            _block_copy(xs_ref, start + r * SUB, acc_ref, r * SUB, sem).start()
            return 0

        def land(r, _):
            _block_copy(xs_ref, start + r * SUB, acc_ref, r * SUB, sem).wait()
            return 0

        def convert(r, _):
            off = pl.multiple_of(r * SUB, SUB)
            xb_ref[pl.ds(off, SUB), :] = acc_ref[pl.ds(off, SUB), :].astype(BF16)
            acc_ref[pl.ds(off, SUB), :] = jnp.zeros((SUB, D_MODEL), F32)
            return 0

        lax.fori_loop(0, n_sub, issue, 0)
        lax.fori_loop(0, n_sub, land, 0)
        lax.fori_loop(0, n_sub, convert, 0)

    @pl.when(rows > 0)
    def _():
        wg_ref[...] = w1g_ref[0].astype(BF16)
        wu_ref[...] = w1u_ref[0].astype(BF16)
        wd_ref[...] = w2_ref[0].astype(BF16)
        bg = b1g_ref[0]
        bu = b1u_ref[0]

        def sub(r, _):
            off = pl.multiple_of(r * SUB, SUB)
            xb = xb_ref[pl.ds(off, SUB), :]
            g = _dot(xb, wg_ref[...]) + bg
            u = _dot(xb, wu_ref[...]) + bu
            g = jnp.minimum(g, SWIGLU_LIMIT)
            u = jnp.clip(u, -SWIGLU_LIMIT, SWIGLU_LIMIT)
            act = g * jax.nn.sigmoid(SWIGLU_ALPHA * g) * (u + 1.0)
            acc_ref[pl.ds(off, SUB), :] += _dot(act.astype(BF16), wd_ref[...])
            return 0

        lax.fori_loop(0, n_sub, sub, 0)

    @pl.when((j == N_FF - 1) & (rows > 0))
    def _():
        b2 = b2_ref[0]

        def issue(r, _):
            off = pl.multiple_of(r * SUB, SUB)
            acc_ref[pl.ds(off, SUB), :] += b2
            _block_copy(acc_ref, r * SUB, ys_ref, start + r * SUB, sem).start()
            return 0

        def land(r, _):
            _block_copy(acc_ref, r * SUB, ys_ref, start + r * SUB, sem).wait()
            return 0

        lax.fori_loop(0, n_sub, issue, 0)
        lax.fori_loop(0, n_sub, land, 0)


def _experts(st_e, st_start, st_rows, n_used, xs, w1, b1, w2, b2, n_super):
    n_rows = xs.shape[0]

    def ff(s, j, n_used_ref):
        return jnp.where(s < n_used_ref[0], j, N_FF - 1)

    w1g_map = lambda s, j, e, st, rw, nu: (e[s], 0, ff(s, j, nu))
    w1u_map = lambda s, j, e, st, rw, nu: (e[s], 0, N_FF + ff(s, j, nu))
    w2_map = lambda s, j, e, st, rw, nu: (e[s], ff(s, j, nu), 0)
    b2_map = lambda s, j, e, st, rw, nu: (e[s], 0, 0)
    return pl.pallas_call(
        _experts_kernel,
        grid_spec=pltpu.PrefetchScalarGridSpec(
            num_scalar_prefetch=4,
            grid=(n_super, N_FF),
            in_specs=[
                pl.BlockSpec(memory_space=pl.ANY),
                pl.BlockSpec((1, D_MODEL, FF_TILE), w1g_map),
                pl.BlockSpec((1, D_MODEL, FF_TILE), w1u_map),
                pl.BlockSpec((1, 1, FF_TILE), w1g_map),
                pl.BlockSpec((1, 1, FF_TILE), w1u_map),
                pl.BlockSpec((1, FF_TILE, D_MODEL), w2_map),
                pl.BlockSpec((1, 1, D_MODEL), b2_map),
            ],
            out_specs=pl.BlockSpec(memory_space=pl.ANY),
            scratch_shapes=[
                pltpu.VMEM((SUPER_ROWS, D_MODEL), BF16),
                pltpu.VMEM((SUPER_ROWS, D_MODEL), F32),
                pltpu.VMEM((D_MODEL, FF_TILE), BF16),
                pltpu.VMEM((D_MODEL, FF_TILE), BF16),
                pltpu.VMEM((FF_TILE, D_MODEL), BF16),
                pltpu.VMEM((SUB, D_MODEL), F32),
                pltpu.SemaphoreType.DMA, pltpu.SemaphoreType.DMA,
            ],
        ),
        out_shape=jax.ShapeDtypeStruct((n_rows, D_MODEL), F32),
        compiler_params=pltpu.CompilerParams(
            dimension_semantics=("arbitrary", "arbitrary"), vmem_limit_bytes=V7X_VMEM_LIMIT,
            has_side_effects=True),
        name="experts",
    )(st_e, st_start, st_rows, n_used, xs, w1, w1, b1, b1, w2, b2)


def _combine_kernel(dest_ref, ys_ref, gate_ref, h1_ref, g2_ref, b2_ref, o_ref, buf_ref, sem):
    step = pl.program_id(0)
    base = step * (MOVE_TILE * TOP_K)

    def copy(a):
        return _row_copy(ys_ref, dest_ref[base + a], buf_ref.at[a % TOP_K], a // TOP_K, sem)

    def start(a, _):
        copy(a).start()
        return 0

    def wait(a, _):
        copy(a).wait()
        return 0

    lax.fori_loop(0, MOVE_TILE * TOP_K, start, 0)
    lax.fori_loop(0, MOVE_TILE * TOP_K, wait, 0)

    gates = gate_ref[...]
    y = buf_ref[0] * gates[:, 0:1]
    for k in range(1, TOP_K):
        y = y + buf_ref[k] * gates[:, k:k + 1]
    o_ref[...] = _layer_norm(DEEPNORM_ALPHA * h1_ref[...] + y, g2_ref[...], b2_ref[...])


def _combine(dest_flat, ys, gates, h1, g2, b2):
    tokens = h1.shape[0]
    row = lambda i, *_: (i, 0)
    return pl.pallas_call(
        _combine_kernel,
        grid_spec=pltpu.PrefetchScalarGridSpec(
            num_scalar_prefetch=1,
            grid=(tokens // MOVE_TILE,),
            in_specs=[
                pl.BlockSpec(memory_space=pl.ANY),
                pl.BlockSpec((MOVE_TILE, V7X_LANES), row),
                pl.BlockSpec((MOVE_TILE, D_MODEL), row),
                pl.BlockSpec((1, D_MODEL), lambda i, *_: (0, 0)),
                pl.BlockSpec((1, D_MODEL), lambda i, *_: (0, 0)),
            ],
            out_specs=pl.BlockSpec((MOVE_TILE, D_MODEL), row),
            scratch_shapes=[pltpu.VMEM((TOP_K, MOVE_TILE, D_MODEL), F32), pltpu.SemaphoreType.DMA],
        ),
        out_shape=jax.ShapeDtypeStruct((tokens, D_MODEL), F32),
        compiler_params=pltpu.CompilerParams(
            dimension_semantics=("arbitrary",), vmem_limit_bytes=V7X_VMEM_LIMIT),
        name="combine",
    )(dest_flat, ys, gates, h1, g2, b2)


def _rotate_half_cols(w):
    half = QK_ROPE // 2
    return jnp.concatenate([-w[..., half:], w[..., :half]], axis=-1)


def _rope_table(length):
    inv_freq = 1.0 / (ROPE_THETA ** (jnp.arange(0, QK_ROPE, 2, dtype=F32) / QK_ROPE))
    freqs = jnp.arange(length, dtype=F32)[:, None] * inv_freq[None, :]
    emb = jnp.concatenate([freqs, freqs], axis=-1)
    return jnp.concatenate([jnp.cos(emb), jnp.sin(emb)], axis=-1)


def _routing_plan(idx, rank, counts, n_super):
    n_tok = idx.shape[0]
    padded = (counts + SEG_ALIGN - 1) // SEG_ALIGN * SEG_ALIGN
    pad_end = jnp.cumsum(padded)
    pad_start = pad_end - padded
    dest = (pad_start[idx] + rank).reshape(-1).astype(jnp.int32)

    n_padmax = N_EXPERTS * SEG_ALIGN
    padcnt = padded - counts
    padcum = jnp.cumsum(padcnt)
    p = jnp.arange(n_padmax, dtype=jnp.int32)
    pe = jnp.minimum(jnp.searchsorted(padcum, p, side='right'), N_EXPERTS - 1)
    pad_rows = (pad_start[pe] + counts[pe] + (p - (padcum[pe] - padcnt[pe]))).astype(jnp.int32)
    n_pad = jnp.stack([padcum[-1], pad_end[-1]]).astype(jnp.int32)
    pad_rows = jnp.where(p < n_pad[0], pad_rows, 0)

    n_st = (padded + SUPER_ROWS - 1) // SUPER_ROWS
    st_cum = jnp.cumsum(n_st)
    n_used = jnp.stack([st_cum[-1], pad_end[-1]]).astype(jnp.int32)
    s = jnp.arange(n_super, dtype=jnp.int32)
    s_eff = jnp.minimum(s, n_used[0] - 1)
    se = jnp.minimum(jnp.searchsorted(st_cum, s_eff, side='right'), N_EXPERTS - 1).astype(jnp.int32)
    local = s_eff - (st_cum[se] - n_st[se])
    st_start = (pad_start[se] + local * SUPER_ROWS).astype(jnp.int32)
    st_rows = jnp.clip(padded[se] - local * SUPER_ROWS, 0, SUPER_ROWS)
    st_rows = jnp.where(s < n_used[0], st_rows, 0).astype(jnp.int32)
    del n_tok
    return dest, pad_rows, n_pad, se, st_start, st_rows, n_used


def kernel(x, meta_tokens, ln_in_g, ln_in_b, w_in, q_norm_g, w_uq, kv_norm_g, w_uk, w_uv, conv_dw_w,
           conv_dw_b, conv_ln_g, conv_ln_b, w_out, ln1_g, ln1_b, w_router, b_router, w_mlp1, b_mlp1,
           w_mlp2, b_mlp2, ln2_g, ln2_b):
    batch, seq, _ = x.shape
    tokens = batch * seq
    row2 = lambda a: a.reshape(1, -1)

    wi = w_in[0]
    s_kpe = Q_LORA + KV_LORA
    kpe_w = wi[:, s_kpe:s_kpe + QK_ROPE]
    w1 = jnp.concatenate(
        [wi[:, :s_kpe], kpe_w, _rotate_half_cols(kpe_w), wi[:, s_kpe + QK_ROPE:]], axis=1).astype(BF16)
    wq3 = w_uq[0].reshape(Q_LORA, N_HEADS, QK_DIM)
    wq_nope = wq3[:, :, :QK_NOPE].reshape(Q_LORA, N_HEADS * QK_NOPE)
    wq_pe = wq3[:, :, QK_NOPE:]
    wq_pr = jnp.concatenate([wq_pe, _rotate_half_cols(wq_pe)], axis=-1).reshape(Q_LORA, N_HEADS * 2 * QK_ROPE)
    wq = jnp.concatenate([wq_nope, wq_pr], axis=1).astype(BF16)
    wuk = w_uk[0].astype(BF16)
    wuv = w_uv[0].astype(BF16)
    wo = w_out[0].astype(BF16)
    wr = w_router[0]
    wr_hi = wr.astype(BF16)
    wr_lo = (wr - wr_hi.astype(F32)).astype(BF16)
    cs = _rope_table(N_META + seq)
    conv_w = jnp.concatenate([conv_dw_w[0], jnp.zeros((CONV_HALO - CONV_W, CONV_CH), F32)], axis=0)

    x2d = x.reshape(tokens, D_MODEL)
    proj_args = (row2(ln_in_g), row2(ln_in_b), w1, row2(q_norm_g[0]), row2(kv_norm_g[0]), wq, wuk, wuv)

    _, k_meta, v_meta, glu_meta = _in_proj(meta_tokens, *proj_args, cs[:N_META], N_META)
    q, k, v, glu = _in_proj(x2d, *proj_args, cs[N_META:], ROW_TILE)
    attn = _attention(q, k, v, k_meta, v_meta, batch, seq)
    conv = _conv(glu, glu_meta, conv_w, row2(conv_dw_b[0]), row2(conv_ln_g[0]), row2(conv_ln_b[0]), batch, seq)

    h1, idx, rank, gates, counts = _out_proj(
        attn, conv, x2d, row2(ln_in_g), row2(ln_in_b), wo, row2(ln1_g[0]), row2(ln1_b[0]),
        wr_hi, wr_lo, row2(b_router[0]))

    n_assign = tokens * TOP_K
    n_rows = n_assign + N_EXPERTS * SEG_ALIGN
    n_super = N_EXPERTS + -(-n_assign // SUPER_ROWS)
    dest, pad_rows, n_pad, st_e, st_start, st_rows, n_used = _routing_plan(
        idx[:, :TOP_K], rank[:, :TOP_K], counts[0], n_super)

    xs = _dispatch(dest, pad_rows, n_pad, h1, n_rows)
    ys = _experts(st_e, st_start, st_rows, n_used, xs, w_mlp1[0], b_mlp1[0].reshape(N_EXPERTS, 1, 2 * D_FF),
                  w_mlp2[0], b_mlp2[0].reshape(N_EXPERTS, 1, D_MODEL), n_super)
    out = _combine(dest, ys, gates, h1, row2(ln2_g[0]), row2(ln2_b[0]))
    return out.reshape(batch, seq, D_MODEL)
```

```python
import functools
import math

import jax
import jax.numpy as jnp
from jax import lax
from jax.experimental import pallas as pl
from jax.experimental.pallas import tpu as pltpu

D_MODEL = 2048
N_META = 16
N_HEADS = 8
QK_NOPE = 128
QK_ROPE = 64
QK_DIM = QK_NOPE + QK_ROPE
V_DIM = 128
Q_LORA = 768
KV_LORA = 512
ROPE_THETA = 10000.0
MLA_WIDTH = N_HEADS * V_DIM
CONV_CH = 1024
CONV_W = 31
N_EXPERTS = 32
TOP_K = 4
D_FF = 2048
SWIGLU_LIMIT = 7.0
SWIGLU_ALPHA = 1.702
DEEPNORM_ALPHA = 2.0 ** 0.25
LN_EPS = 1e-5
RMS_EPS = 1e-6

V7X_LANES = 128
V7X_VMEM_LIMIT = 56 * 1024 * 1024

ROW_TILE = 256
ATT_TILE = 256
CONV_TILE = 256
CONV_HALO = 32
CONV_ROWS = 32
CONV_LANES = 256
MOVE_TILE = 128
SEG_ALIGN = 128
SUPER_ROWS = 1536
FF_TILE = 256
MM_ROWS = 512

F32 = jnp.float32
BF16 = jnp.bfloat16


def _dot(a, b):
    return jnp.dot(a, b, preferred_element_type=F32)


def _dot_nt(a, b):
    return lax.dot_general(a, b, (((1,), (1,)), ((), ())), preferred_element_type=F32)


def _layer_norm(x, g, b):
    mu = jnp.mean(x, axis=-1, keepdims=True)
    xc = x - mu
    var = jnp.mean(xc * xc, axis=-1, keepdims=True)
    return xc * lax.rsqrt(var + LN_EPS) * g + b


def _rms_norm(x, g):
    ms = jnp.mean(x * x, axis=-1, keepdims=True)
    return x * lax.rsqrt(ms + RMS_EPS) * g


def _const_spec(shape):
    zeros = (0,) * len(shape)
    return pl.BlockSpec(shape, lambda *_: zeros)


_C_Q = (0, Q_LORA)
_C_KV = (Q_LORA, Q_LORA + KV_LORA)
_C_KPE = (_C_KV[1], _C_KV[1] + 2 * QK_ROPE)
_C_A = (_C_KPE[1], _C_KPE[1] + CONV_CH)
_C_G = (_C_A[1], _C_A[1] + CONV_CH)
IN_AUG = _C_G[1]


def _in_proj_kernel(x_ref, lng_ref, lnb_ref, w1_ref, qg_ref, kvg_ref, wq_ref, wuk_ref, wuv_ref, cs_ref,
                    q_ref, k_ref, v_ref, glu_ref):
    h0 = _layer_norm(x_ref[...], lng_ref[...], lnb_ref[...])
    hb = h0.astype(BF16)
    cs = cs_ref[...]

    def rope(t128):
        t = t128 * cs
        return t + pltpu.roll(t, QK_ROPE, axis=1)

    cq = _dot(hb, w1_ref[:, _C_Q[0]:_C_Q[1]])
    cqn = _rms_norm(cq, qg_ref[...]).astype(BF16)
    ckv = _dot(hb, w1_ref[:, _C_KV[0]:_C_KV[1]])
    ckvn = _rms_norm(ckv, kvg_ref[...]).astype(BF16)
    kpe = rope(_dot(hb, w1_ref[:, _C_KPE[0]:_C_KPE[1]]))[:, :QK_ROPE].astype(BF16)

    a = _dot(hb, w1_ref[:, _C_A[0]:_C_A[1]])
    g = _dot(hb, w1_ref[:, _C_G[0]:_C_G[1]])
    glu_ref[...] = a * jax.nn.sigmoid(g)

    knope = _dot(ckvn, wuk_ref[...])
    v = _dot(ckvn, wuv_ref[...])
    for h in range(N_HEADS):
        k_ref[h, :, 0:QK_NOPE] = knope[:, h * QK_NOPE:(h + 1) * QK_NOPE].astype(BF16)
        k_ref[h, :, QK_NOPE:QK_DIM] = kpe
        v_ref[h] = v[:, h * V_DIM:(h + 1) * V_DIM].astype(BF16)

    qn = _dot(cqn, wq_ref[:, 0:N_HEADS * QK_NOPE])
    qp = _dot(cqn, wq_ref[:, N_HEADS * QK_NOPE:])
    for h in range(N_HEADS):
        q_ref[h, :, 0:QK_NOPE] = qn[:, h * QK_NOPE:(h + 1) * QK_NOPE].astype(BF16)
        q_ref[h, :, QK_NOPE:QK_DIM] = rope(qp[:, h * V7X_LANES:(h + 1) * V7X_LANES])[:, :QK_ROPE].astype(BF16)


def _in_proj(x2d, lng, lnb, w1, qg, kvg, wq, wuk, wuv, cs, tm):
    rows = x2d.shape[0]
    n_cs = cs.shape[0] // tm
    row = lambda i: (i, 0)
    head_row = lambda i: (0, i, 0)
    return pl.pallas_call(
        _in_proj_kernel,
        grid=(rows // tm,),
        in_specs=[
            pl.BlockSpec((tm, D_MODEL), row),
            _const_spec((1, D_MODEL)), _const_spec((1, D_MODEL)),
            _const_spec((D_MODEL, IN_AUG)),
            _const_spec((1, Q_LORA)), _const_spec((1, KV_LORA)),
            _const_spec((Q_LORA, 2 * N_HEADS * QK_NOPE)),
            _const_spec((KV_LORA, N_HEADS * QK_NOPE)), _const_spec((KV_LORA, MLA_WIDTH)),
            pl.BlockSpec((tm, V7X_LANES), lambda i: (i % n_cs, 0)),
        ],
        out_specs=[
            pl.BlockSpec((N_HEADS, tm, QK_DIM), head_row),
            pl.BlockSpec((N_HEADS, tm, QK_DIM), head_row),
            pl.BlockSpec((N_HEADS, tm, V_DIM), head_row),
            pl.BlockSpec((tm, CONV_CH), row),
        ],
        out_shape=[
            jax.ShapeDtypeStruct((N_HEADS, rows, QK_DIM), BF16),
            jax.ShapeDtypeStruct((N_HEADS, rows, QK_DIM), BF16),
            jax.ShapeDtypeStruct((N_HEADS, rows, V_DIM), BF16),
            jax.ShapeDtypeStruct((rows, CONV_CH), F32),
        ],
        compiler_params=pltpu.CompilerParams(
            dimension_semantics=("arbitrary",), vmem_limit_bytes=V7X_VMEM_LIMIT),
        name="in_proj",
    )(x2d, lng, lnb, w1, qg, kvg, wq, wuk, wuv, cs)


def _attention_kernel(q_ref, k_ref, v_ref, km_ref, vm_ref, o_ref):
    i = pl.program_id(2)
    scale = 1.0 / math.sqrt(QK_DIM)
    q = q_ref[0]

    s = _dot_nt(q, km_ref[0]) * scale
    m = jnp.max(s, axis=1, keepdims=True)
    p = jnp.exp(s - m)
    l = jnp.sum(p, axis=1, keepdims=True)
    acc = _dot(p.astype(BF16), vm_ref[0])

    def block(j, carry, masked):
        m, l, acc = carry
        start = pl.multiple_of(j * ATT_TILE, ATT_TILE)
        kb = k_ref[0, pl.ds(start, ATT_TILE), :]
        vb = v_ref[0, pl.ds(start, ATT_TILE), :]
        s = _dot_nt(q, kb) * scale
        if masked:
            r = lax.broadcasted_iota(jnp.int32, s.shape, 0)
            c = lax.broadcasted_iota(jnp.int32, s.shape, 1)
            s = jnp.where(c <= r, s, -1e30)
        m_new = jnp.maximum(m, jnp.max(s, axis=1, keepdims=True))
        alpha = jnp.exp(m - m_new)
        p = jnp.exp(s - m_new)
        l = alpha * l + jnp.sum(p, axis=1, keepdims=True)
        acc = alpha * acc + _dot(p.astype(BF16), vb)
        return m_new, l, acc

    carry = lax.fori_loop(0, i, lambda j, c: block(j, c, False), (m, l, acc))
    m, l, acc = block(i, carry, True)
    o_ref[...] = (acc / l).astype(BF16)


def _attention(q, k, v, km, vm, batch, seq):
    nq = seq // ATT_TILE
    return pl.pallas_call(
        _attention_kernel,
        grid=(batch, N_HEADS, nq),
        in_specs=[
            pl.BlockSpec((1, ATT_TILE, QK_DIM), lambda b, h, i: (h, b * nq + i, 0)),
            pl.BlockSpec((1, seq, QK_DIM), lambda b, h, i: (h, b, 0)),
            pl.BlockSpec((1, seq, V_DIM), lambda b, h, i: (h, b, 0)),
            pl.BlockSpec((1, N_META, QK_DIM), lambda b, h, i: (h, 0, 0)),
            pl.BlockSpec((1, N_META, V_DIM), lambda b, h, i: (h, 0, 0)),
        ],
        out_specs=pl.BlockSpec((ATT_TILE, V_DIM), lambda b, h, i: (b * nq + i, h)),
        out_shape=jax.ShapeDtypeStruct((batch * seq, MLA_WIDTH), BF16),
        compiler_params=pltpu.CompilerParams(
            dimension_semantics=("arbitrary", "arbitrary", "arbitrary"), vmem_limit_bytes=V7X_VMEM_LIMIT),
        name="attention",
    )(q, k, v, km, vm)


def _conv_kernel(cur_ref, prev_ref, meta_ref, w_ref, cb_ref, lng_ref, lnb_ref, o_ref, win_ref, acc_ref):
    i = pl.program_id(1)

    @pl.when(i == 0)
    def _():
        win_ref[0:CONV_HALO - N_META, :] = jnp.zeros((CONV_HALO - N_META, CONV_CH), F32)
        win_ref[CONV_HALO - N_META:CONV_HALO, :] = meta_ref[...]

    @pl.when(i > 0)
    def _():
        win_ref[0:CONV_HALO, :] = prev_ref[...]

    win_ref[CONV_HALO:, :] = cur_ref[...]

    base = CONV_HALO - (CONV_W - 1)

    for rc in range(CONV_TILE // CONV_ROWS):
        r0 = rc * CONV_ROWS
        for c in range(CONV_CH // CONV_LANES):
            lanes = pl.ds(c * CONV_LANES, CONV_LANES)
            acc = jnp.zeros((CONV_ROWS, CONV_LANES), F32)
            for k in range(CONV_W):
                acc = acc + win_ref[pl.ds(r0 + base + k, CONV_ROWS), lanes] * w_ref[k:k + 1, lanes]
            acc_ref[pl.ds(r0, CONV_ROWS), lanes] = acc

    y = _layer_norm(acc_ref[...] + cb_ref[...], lng_ref[...], lnb_ref[...])
    o_ref[...] = (y * jax.nn.sigmoid(y)).astype(BF16)


def _conv(glu, glu_meta, w, cb, lng, lnb, batch, seq):
    nt = seq // CONV_TILE
    per = CONV_TILE // CONV_HALO
    return pl.pallas_call(
        _conv_kernel,
        grid=(batch, nt),
        in_specs=[
            pl.BlockSpec((CONV_TILE, CONV_CH), lambda b, i: (b * nt + i, 0)),
            pl.BlockSpec((CONV_HALO, CONV_CH), lambda b, i: (jnp.maximum((b * nt + i) * per - 1, 0), 0)),
            _const_spec((N_META, CONV_CH)),
            _const_spec((CONV_HALO, CONV_CH)),
            _const_spec((1, CONV_CH)), _const_spec((1, CONV_CH)), _const_spec((1, CONV_CH)),
        ],
        out_specs=pl.BlockSpec((CONV_TILE, CONV_CH), lambda b, i: (b * nt + i, 0)),
        out_shape=jax.ShapeDtypeStruct((batch * seq, CONV_CH), BF16),
        scratch_shapes=[pltpu.VMEM((CONV_TILE + CONV_HALO, CONV_CH), F32),
                        pltpu.VMEM((CONV_TILE, CONV_CH), F32)],
        compiler_params=pltpu.CompilerParams(
            dimension_semantics=("arbitrary", "arbitrary"), vmem_limit_bytes=V7X_VMEM_LIMIT),
        name="conv",
    )(glu, glu, glu_meta, w, cb, lng, lnb)


def _out_proj_kernel(attn_ref, conv_ref, x_ref, lng_ref, lnb_ref, wo_ref, g1_ref, b1_ref,
                     wrh_ref, wrl_ref, br_ref,
                     h1_ref, idx_ref, rank_ref, gate_ref, cnt_ref, carry_ref):
    step = pl.program_id(0)
    tm = x_ref.shape[0]

    @pl.when(step == 0)
    def _():
        carry_ref[...] = jnp.zeros_like(carry_ref)

    h0 = _layer_norm(x_ref[...], lng_ref[...], lnb_ref[...])
    mix = _dot(attn_ref[...], wo_ref[0:MLA_WIDTH, :]) + _dot(conv_ref[...], wo_ref[MLA_WIDTH:, :])
    h1 = _layer_norm(DEEPNORM_ALPHA * h0 + mix, g1_ref[...], b1_ref[...])
    h1_ref[...] = h1

    hi = h1.astype(BF16)
    lo = (h1 - hi.astype(F32)).astype(BF16)
    logits = (_dot(hi, wrh_ref[...]) + (_dot(hi, wrl_ref[...]) + _dot(lo, wrh_ref[...]))) + br_ref[...]

    lane = lax.broadcasted_iota(jnp.int32, (tm, N_EXPERTS), 1)
    work = logits
    vals, idxs = [], []
    for _ in range(TOP_K):
        mx = jnp.max(work, axis=1, keepdims=True)
        ix = jnp.min(jnp.where(work == mx, lane, N_EXPERTS), axis=1, keepdims=True)
        vals.append(mx)
        idxs.append(ix)
        work = jnp.where(lane == ix, -jnp.inf, work)
    exps = [jnp.exp(v - vals[0]) for v in vals]
    denom = exps[0] + exps[1] + exps[2] + exps[3]

    onehots = [(lane == ix) for ix in idxs]
    chosen = (onehots[0] | onehots[1] | onehots[2] | onehots[3])
    chosen_f = jnp.where(chosen, 1.0, 0.0)
    r = lax.broadcasted_iota(jnp.int32, (tm, tm), 0)
    c = lax.broadcasted_iota(jnp.int32, (tm, tm), 1)
    lower = jnp.where(c < r, 1.0, 0.0).astype(BF16)
    before = _dot(lower, chosen_f.astype(BF16)) + carry_ref[...]

    out_lane = lax.broadcasted_iota(jnp.int32, (tm, V7X_LANES), 1)
    idx_out = jnp.zeros((tm, V7X_LANES), jnp.int32)
    rank_out = jnp.zeros((tm, V7X_LANES), jnp.int32)
    gate_out = jnp.zeros((tm, V7X_LANES), F32)
    for k in range(TOP_K):
        rank_k = jnp.sum(jnp.where(onehots[k], before, 0.0), axis=1, keepdims=True).astype(jnp.int32)
        idx_out = jnp.where(out_lane == k, idxs[k], idx_out)
        rank_out = jnp.where(out_lane == k, rank_k, rank_out)
        gate_out = jnp.where(out_lane == k, exps[k] / denom, gate_out)
    idx_ref[...] = idx_out
    rank_ref[...] = rank_out
    gate_ref[...] = gate_out

    carry_ref[...] = carry_ref[...] + jnp.sum(chosen_f, axis=0, keepdims=True)
    cnt_ref[...] = carry_ref[...].astype(jnp.int32)


def _out_proj(attn, conv, x2d, lng, lnb, wo, g1, b1, wrh, wrl, br):
    rows = x2d.shape[0]
    tm = ROW_TILE
    row = lambda i: (i, 0)
    return pl.pallas_call(
        _out_proj_kernel,
        grid=(rows // tm,),
        in_specs=[
            pl.BlockSpec((tm, MLA_WIDTH), row), pl.BlockSpec((tm, CONV_CH), row),
            pl.BlockSpec((tm, D_MODEL), row),
            _const_spec((1, D_MODEL)), _const_spec((1, D_MODEL)),
            _const_spec((D_MODEL, D_MODEL)),
            _const_spec((1, D_MODEL)), _const_spec((1, D_MODEL)),
            _const_spec((D_MODEL, N_EXPERTS)), _const_spec((D_MODEL, N_EXPERTS)),
            _const_spec((1, N_EXPERTS)),
        ],
        out_specs=[
            pl.BlockSpec((tm, D_MODEL), row),
            pl.BlockSpec((tm, V7X_LANES), row), pl.BlockSpec((tm, V7X_LANES), row),
            pl.BlockSpec((tm, V7X_LANES), row),
            _const_spec((1, N_EXPERTS)),
        ],
        out_shape=[
            jax.ShapeDtypeStruct((rows, D_MODEL), F32),
            jax.ShapeDtypeStruct((rows, V7X_LANES), jnp.int32),
            jax.ShapeDtypeStruct((rows, V7X_LANES), jnp.int32),
            jax.ShapeDtypeStruct((rows, V7X_LANES), F32),
            jax.ShapeDtypeStruct((1, N_EXPERTS), jnp.int32),
        ],
        scratch_shapes=[pltpu.VMEM((1, N_EXPERTS), F32)],
        compiler_params=pltpu.CompilerParams(
            dimension_semantics=("arbitrary",), vmem_limit_bytes=V7X_VMEM_LIMIT),
        name="out_proj_router",
    )(attn, conv, x2d, lng, lnb, wo, g1, b1, wrh, wrl, br)


def _row_copy(src_ref, src_row, dst_ref, dst_row, sem):
    return pltpu.make_async_copy(src_ref.at[pl.ds(src_row, 1), :], dst_ref.at[pl.ds(dst_row, 1), :], sem)


def _dispatch_kernel(dest_ref, padrow_ref, npad_ref, h1_ref, xs_ref, zero_ref, sem, zsem):
    step = pl.program_id(0)
    base = step * (MOVE_TILE * TOP_K)

    def start(a, _):
        _row_copy(h1_ref, a // TOP_K, xs_ref, dest_ref[base + a], sem).start()
        return 0

    lax.fori_loop(0, MOVE_TILE * TOP_K, start, 0)

    @pl.when(step == 0)
    def _():
        zero_ref[...] = jnp.zeros_like(zero_ref)
        n = npad_ref[0]
        tail_start = npad_ref[1]
        n_tail = (xs_ref.shape[0] - tail_start) // SEG_ALIGN

        def zstart(p, _):
            _row_copy(zero_ref, 0, xs_ref, padrow_ref[p], zsem).start()
            return 0

        def zwait(p, _):
            _row_copy(zero_ref, 0, xs_ref, padrow_ref[p], zsem).wait()
            return 0

        def tstart(b, _):
            _block_copy(zero_ref, 0, xs_ref, tail_start + b * SEG_ALIGN, zsem).start()
            return 0

        def twait(b, _):
            _block_copy(zero_ref, 0, xs_ref, tail_start + b * SEG_ALIGN, zsem).wait()
            return 0

        lax.fori_loop(0, n, zstart, 0)
        lax.fori_loop(0, n_tail, tstart, 0)
        lax.fori_loop(0, n, zwait, 0)
        lax.fori_loop(0, n_tail, twait, 0)

    def wait(a, _):
        _row_copy(h1_ref, a // TOP_K, xs_ref, dest_ref[base + a], sem).wait()
        return 0

    lax.fori_loop(0, MOVE_TILE * TOP_K, wait, 0)


def _dispatch(dest_flat, pad_rows, n_pad, h1, n_rows):
    tokens = h1.shape[0]
    return pl.pallas_call(
        _dispatch_kernel,
        grid_spec=pltpu.PrefetchScalarGridSpec(
            num_scalar_prefetch=3,
            grid=(tokens // MOVE_TILE,),
            in_specs=[pl.BlockSpec((MOVE_TILE, D_MODEL), lambda i, *_: (i, 0))],
            out_specs=pl.BlockSpec(memory_space=pl.ANY),
            scratch_shapes=[pltpu.VMEM((SEG_ALIGN, D_MODEL), F32),
                            pltpu.SemaphoreType.DMA, pltpu.SemaphoreType.DMA],
        ),
        out_shape=jax.ShapeDtypeStruct((n_rows, D_MODEL), F32),
        compiler_params=pltpu.CompilerParams(
            dimension_semantics=("arbitrary",), has_side_effects=True),
        name="dispatch",
    )(dest_flat, pad_rows, n_pad, h1)


N_FF = D_FF // FF_TILE
SUB = SEG_ALIGN
N_SUB = SUPER_ROWS // SUB


def _block_copy(src_ref, src_row, dst_ref, dst_row, sem):
    src = src_ref.at[pl.ds(pl.multiple_of(src_row, SUB), SUB), :]
    dst = dst_ref.at[pl.ds(pl.multiple_of(dst_row, SUB), SUB), :]
    return pltpu.make_async_copy(src, dst, sem)


def _experts_kernel(st_e_ref, st_start_ref, st_rows_ref, n_used_ref,
                    xs_ref, w1g_ref, w1u_ref, b1g_ref, b1u_ref, w2_ref, b2_ref,
                    ys_ref,
                    xb_ref, acc_ref, wg_ref, wu_ref, wd_ref, zero_ref, sem, zsem):
    s = pl.program_id(0)
    j = pl.program_id(1)

    @pl.when((s == 0) & (j == 0))
    def _():
        zero_ref[...] = jnp.zeros_like(zero_ref)
        tail_start = n_used_ref[1]
        n_tail = (ys_ref.shape[0] - tail_start) // SUB

        def tstart(b, _):
            _block_copy(zero_ref, 0, ys_ref, tail_start + b * SUB, zsem).start()
            return 0

        def twait(b, _):
            _block_copy(zero_ref, 0, ys_ref, tail_start + b * SUB, zsem).wait()
            return 0

        lax.fori_loop(0, n_tail, tstart, 0)
        lax.fori_loop(0, n_tail, twait, 0)
    rows = st_rows_ref[s]
    start = st_start_ref[s]
    n_sub = rows // SUB

    @pl.when((j == 0) & (rows > 0))
    def _():
        def issue(r, _):
            _block_copy(xs_ref, start + r * SUB, acc_ref, r * SUB, sem).start()
            return 0

        def land(r, _):
            _block_copy(xs_ref, start + r * SUB, acc_ref, r * SUB, sem).wait()
            return 0

        def convert(r, _):
            off = pl.multiple_of(r * SUB, SUB)
            xb_ref[pl.ds(off, SUB), :] = acc_ref[pl.ds(off, SUB), :].astype(BF16)
            acc_ref[pl.ds(off, SUB), :] = jnp.zeros((SUB, D_MODEL), F32)
            return 0

        lax.fori_loop(0, n_sub, issue, 0)
        lax.fori_loop(0, n_sub, land, 0)
        lax.fori_loop(0, n_sub, convert, 0)

    @pl.when(rows > 0)
    def _():
        wg_ref[...] = w1g_ref[0].astype(BF16)
        wu_ref[...] = w1u_ref[0].astype(BF16)
        wd_ref[...] = w2_ref[0].astype(BF16)
        bg = b1g_ref[0]
        bu = b1u_ref[0]

        def chunk(row0, m):
            off = pl.multiple_of(row0, SUB)
            xb = xb_ref[pl.ds(off, m), :]
            g = _dot(xb, wg_ref[...]) + bg
            u = _dot(xb, wu_ref[...]) + bu
            g = jnp.minimum(g, SWIGLU_LIMIT)
            u = jnp.clip(u, -SWIGLU_LIMIT, SWIGLU_LIMIT)
            act = g * jax.nn.sigmoid(SWIGLU_ALPHA * g) * (u + 1.0)
            acc_ref[pl.ds(off, m), :] += _dot(act.astype(BF16), wd_ref[...])

        n_big = rows // MM_ROWS

        def big(r, _):
            chunk(r * MM_ROWS, MM_ROWS)
            return 0

        lax.fori_loop(0, n_big, big, 0)
        done = n_big * MM_ROWS
        m = MM_ROWS // 2
        while m >= SUB:
            take = ((rows - done) & m) != 0

            @pl.when(take)
            def _(done=done, m=m):
                chunk(done, m)

            done = done + jnp.where(take, m, 0)
            m //= 2

    @pl.when((j == N_FF - 1) & (rows > 0))
    def _():
        b2 = b2_ref[0]

        def issue(r, _):
            off = pl.multiple_of(r * SUB, SUB)
            acc_ref[pl.ds(off, SUB), :] += b2
            _block_copy(acc_ref, r * SUB, ys_ref, start + r * SUB, sem).start()
            return 0

        def land(r, _):
            _block_copy(acc_ref, r * SUB, ys_ref, start + r * SUB, sem).wait()
            return 0

        lax.fori_loop(0, n_sub, issue, 0)
        lax.fori_loop(0, n_sub, land, 0)


def _experts(st_e, st_start, st_rows, n_used, xs, w1, b1, w2, b2, n_super):
    n_rows = xs.shape[0]

    def ff(s, j, n_used_ref):
        return jnp.where(s < n_used_ref[0], j, N_FF - 1)

    w1g_map = lambda s, j, e, st, rw, nu: (e[s], 0, ff(s, j, nu))
    w1u_map = lambda s, j, e, st, rw, nu: (e[s], 0, N_FF + ff(s, j, nu))
    w2_map = lambda s, j, e, st, rw, nu: (e[s], ff(s, j, nu), 0)
    b2_map = lambda s, j, e, st, rw, nu: (e[s], 0, 0)
    return pl.pallas_call(
        _experts_kernel,
        grid_spec=pltpu.PrefetchScalarGridSpec(
            num_scalar_prefetch=4,
            grid=(n_super, N_FF),
            in_specs=[
                pl.BlockSpec(memory_space=pl.ANY),
                pl.BlockSpec((1, D_MODEL, FF_TILE), w1g_map),
                pl.BlockSpec((1, D_MODEL, FF_TILE), w1u_map),
                pl.BlockSpec((1, 1, FF_TILE), w1g_map),
                pl.BlockSpec((1, 1, FF_TILE), w1u_map),
                pl.BlockSpec((1, FF_TILE, D_MODEL), w2_map),
                pl.BlockSpec((1, 1, D_MODEL), b2_map),
            ],
            out_specs=pl.BlockSpec(memory_space=pl.ANY),
            scratch_shapes=[
                pltpu.VMEM((SUPER_ROWS, D_MODEL), BF16),
                pltpu.VMEM((SUPER_ROWS, D_MODEL), F32),
                pltpu.VMEM((D_MODEL, FF_TILE), BF16),
                pltpu.VMEM((D_MODEL, FF_TILE), BF16),
                pltpu.VMEM((FF_TILE, D_MODEL), BF16),
                pltpu.VMEM((SUB, D_MODEL), F32),
                pltpu.SemaphoreType.DMA, pltpu.SemaphoreType.DMA,
            ],
        ),
        out_shape=jax.ShapeDtypeStruct((n_rows, D_MODEL), F32),
        compiler_params=pltpu.CompilerParams(
            dimension_semantics=("arbitrary", "arbitrary"), vmem_limit_bytes=V7X_VMEM_LIMIT,
            has_side_effects=True),
        name="experts",
    )(st_e, st_start, st_rows, n_used, xs, w1, w1, b1, b1, w2, b2)


def _combine_kernel(dest_ref, ys_ref, gate_ref, h1_ref, g2_ref, b2_ref, o_ref, buf_ref, sem):
    step = pl.program_id(0)
    base = step * (MOVE_TILE * TOP_K)

    def copy(a):
        return _row_copy(ys_ref, dest_ref[base + a], buf_ref.at[a % TOP_K], a // TOP_K, sem)

    def start(a, _):
        copy(a).start()
        return 0

    def wait(a, _):
        copy(a).wait()
        return 0

    lax.fori_loop(0, MOVE_TILE * TOP_K, start, 0)
    lax.fori_loop(0, MOVE_TILE * TOP_K, wait, 0)

    gates = gate_ref[...]
    y = buf_ref[0] * gates[:, 0:1]
    for k in range(1, TOP_K):
        y = y + buf_ref[k] * gates[:, k:k + 1]
    o_ref[...] = _layer_norm(DEEPNORM_ALPHA * h1_ref[...] + y, g2_ref[...], b2_ref[...])


def _combine(dest_flat, ys, gates, h1, g2, b2):
    tokens = h1.shape[0]
    row = lambda i, *_: (i, 0)
    return pl.pallas_call(
        _combine_kernel,
        grid_spec=pltpu.PrefetchScalarGridSpec(
            num_scalar_prefetch=1,
            grid=(tokens // MOVE_TILE,),
            in_specs=[
                pl.BlockSpec(memory_space=pl.ANY),
                pl.BlockSpec((MOVE_TILE, V7X_LANES), row),
                pl.BlockSpec((MOVE_TILE, D_MODEL), row),
                pl.BlockSpec((1, D_MODEL), lambda i, *_: (0, 0)),
                pl.BlockSpec((1, D_MODEL), lambda i, *_: (0, 0)),
            ],
            out_specs=pl.BlockSpec((MOVE_TILE, D_MODEL), row),
            scratch_shapes=[pltpu.VMEM((TOP_K, MOVE_TILE, D_MODEL), F32), pltpu.SemaphoreType.DMA],
        ),
        out_shape=jax.ShapeDtypeStruct((tokens, D_MODEL), F32),
        compiler_params=pltpu.CompilerParams(
            dimension_semantics=("arbitrary",), vmem_limit_bytes=V7X_VMEM_LIMIT),
        name="combine",
    )(dest_flat, ys, gates, h1, g2, b2)


def _rotate_half_cols(w):
    half = QK_ROPE // 2
    return jnp.concatenate([-w[..., half:], w[..., :half]], axis=-1)


def _rope_table(length):
    inv_freq = 1.0 / (ROPE_THETA ** (jnp.arange(0, QK_ROPE, 2, dtype=F32) / QK_ROPE))
    freqs = jnp.arange(length, dtype=F32)[:, None] * inv_freq[None, :]
    emb = jnp.concatenate([freqs, freqs], axis=-1)
    return jnp.concatenate([jnp.cos(emb), jnp.sin(emb)], axis=-1)


def _routing_plan(idx, rank, counts, n_super):
    experts = jnp.arange(N_EXPERTS, dtype=jnp.int32)

    def lookup(table, i):
        return jnp.sum(jnp.where(i[..., None] == experts, table, 0), axis=-1)

    def bucket(cum, i):
        return jnp.minimum(jnp.sum((cum <= i[..., None]).astype(jnp.int32), axis=-1), N_EXPERTS - 1)

    counts = counts.astype(jnp.int32)
    padded = (counts + SEG_ALIGN - 1) // SEG_ALIGN * SEG_ALIGN
    pad_end = jnp.cumsum(padded)
    pad_start = pad_end - padded
    dest = (lookup(pad_start, idx) + rank).reshape(-1).astype(jnp.int32)

    n_padmax = N_EXPERTS * SEG_ALIGN
    padcnt = padded - counts
    padcum = jnp.cumsum(padcnt)
    p = jnp.arange(n_padmax, dtype=jnp.int32)
    pe = bucket(padcum, p)
    pad_rows = lookup(pad_start + counts - (padcum - padcnt), pe) + p
    n_pad = jnp.stack([padcum[-1], pad_end[-1]]).astype(jnp.int32)
    pad_rows = jnp.where(p < n_pad[0], pad_rows, 0).astype(jnp.int32)

    n_st = (padded + SUPER_ROWS - 1) // SUPER_ROWS
    st_cum = jnp.cumsum(n_st)
    n_used = jnp.stack([st_cum[-1], pad_end[-1]]).astype(jnp.int32)
    s = jnp.arange(n_super, dtype=jnp.int32)
    s_eff = jnp.minimum(s, n_used[0] - 1)
    se = bucket(st_cum, s_eff).astype(jnp.int32)
    local = s_eff - lookup(st_cum - n_st, se)
    st_start = (lookup(pad_start, se) + local * SUPER_ROWS).astype(jnp.int32)
    st_rows = jnp.clip(lookup(padded, se) - local * SUPER_ROWS, 0, SUPER_ROWS)
    st_rows = jnp.where(s < n_used[0], st_rows, 0).astype(jnp.int32)
    return dest, pad_rows, n_pad, se, st_start, st_rows, n_used


def kernel(x, meta_tokens, ln_in_g, ln_in_b, w_in, q_norm_g, w_uq, kv_norm_g, w_uk, w_uv, conv_dw_w,
           conv_dw_b, conv_ln_g, conv_ln_b, w_out, ln1_g, ln1_b, w_router, b_router, w_mlp1, b_mlp1,
           w_mlp2, b_mlp2, ln2_g, ln2_b):
    batch, seq, _ = x.shape
    tokens = batch * seq
    row2 = lambda a: a.reshape(1, -1)

    wi = w_in[0]
    s_kpe = Q_LORA + KV_LORA
    kpe_w = wi[:, s_kpe:s_kpe + QK_ROPE]
    w1 = jnp.concatenate(
        [wi[:, :s_kpe], kpe_w, _rotate_half_cols(kpe_w), wi[:, s_kpe + QK_ROPE:]], axis=1).astype(BF16)
    wq3 = w_uq[0].reshape(Q_LORA, N_HEADS, QK_DIM)
    wq_nope = wq3[:, :, :QK_NOPE].reshape(Q_LORA, N_HEADS * QK_NOPE)
    wq_pe = wq3[:, :, QK_NOPE:]
    wq_pr = jnp.concatenate([wq_pe, _rotate_half_cols(wq_pe)], axis=-1).reshape(Q_LORA, N_HEADS * 2 * QK_ROPE)
    wq = jnp.concatenate([wq_nope, wq_pr], axis=1).astype(BF16)
    wuk = w_uk[0].astype(BF16)
    wuv = w_uv[0].astype(BF16)
    wo = w_out[0].astype(BF16)
    wr = w_router[0]
    wr_hi = wr.astype(BF16)
    wr_lo = (wr - wr_hi.astype(F32)).astype(BF16)
    cs = _rope_table(N_META + seq)
    conv_w = jnp.concatenate([conv_dw_w[0], jnp.zeros((CONV_HALO - CONV_W, CONV_CH), F32)], axis=0)

    x2d = x.reshape(tokens, D_MODEL)
    proj_args = (row2(ln_in_g), row2(ln_in_b), w1, row2(q_norm_g[0]), row2(kv_norm_g[0]), wq, wuk, wuv)

    _, k_meta, v_meta, glu_meta = _in_proj(meta_tokens, *proj_args, cs[:N_META], N_META)
    q, k, v, glu = _in_proj(x2d, *proj_args, cs[N_META:], ROW_TILE)
    attn = _attention(q, k, v, k_meta, v_meta, batch, seq)
    conv = _conv(glu, glu_meta, conv_w, row2(conv_dw_b[0]), row2(conv_ln_g[0]), row2(conv_ln_b[0]), batch, seq)

    h1, idx, rank, gates, counts = _out_proj(
        attn, conv, x2d, row2(ln_in_g), row2(ln_in_b), wo, row2(ln1_g[0]), row2(ln1_b[0]),
        wr_hi, wr_lo, row2(b_router[0]))

    n_assign = tokens * TOP_K
    n_rows = n_assign + N_EXPERTS * SEG_ALIGN
    n_super = N_EXPERTS + -(-n_assign // SUPER_ROWS)
    dest, pad_rows, n_pad, st_e, st_start, st_rows, n_used = _routing_plan(
        idx[:, :TOP_K], rank[:, :TOP_K], counts[0], n_super)

    xs = _dispatch(dest, pad_rows, n_pad, h1, n_rows)
    ys = _experts(st_e, st_start, st_rows, n_used, xs, w_mlp1[0], b_mlp1[0].reshape(N_EXPERTS, 1, 2 * D_FF),
                  w_mlp2[0], b_mlp2[0].reshape(N_EXPERTS, 1, D_MODEL), n_super)
    out = _combine(dest, ys, gates, h1, row2(ln2_g[0]), row2(ln2_b[0]))
    return out.reshape(batch, seq, D_MODEL)
```

```python
import functools
import math

import jax
import jax.numpy as jnp
from jax import lax
from jax.experimental import pallas as pl
from jax.experimental.pallas import tpu as pltpu

D_MODEL = 2048
N_META = 16
N_HEADS = 8
QK_NOPE = 128
QK_ROPE = 64
QK_DIM = QK_NOPE + QK_ROPE
V_DIM = 128
Q_LORA = 768
KV_LORA = 512
ROPE_THETA = 10000.0
MLA_WIDTH = N_HEADS * V_DIM
CONV_CH = 1024
CONV_W = 31
N_EXPERTS = 32
TOP_K = 4
D_FF = 2048
SWIGLU_LIMIT = 7.0
SWIGLU_ALPHA = 1.702
DEEPNORM_ALPHA = 2.0 ** 0.25
LN_EPS = 1e-5
RMS_EPS = 1e-6

V7X_LANES = 128
V7X_VMEM_LIMIT = 56 * 1024 * 1024

ROW_TILE = 256
ATT_TILE = 256
CONV_TILE = 256
CONV_HALO = 32
CONV_ROWS = 32
CONV_LANES = 256
DISPATCH_TILE = 256
COMBINE_TILE = 128
SEG_ALIGN = 128
SUPER_ROWS = 1536
FF_TILE = 256
MM_ROWS = 512

F32 = jnp.float32
BF16 = jnp.bfloat16


def _dot(a, b):
    return jnp.dot(a, b, preferred_element_type=F32)


def _dot_nt(a, b):
    return lax.dot_general(a, b, (((1,), (1,)), ((), ())), preferred_element_type=F32)


def _layer_norm(x, g, b):
    mu = jnp.mean(x, axis=-1, keepdims=True)
    xc = x - mu
    var = jnp.mean(xc * xc, axis=-1, keepdims=True)
    return xc * lax.rsqrt(var + LN_EPS) * g + b


def _rms_norm(x, g):
    ms = jnp.mean(x * x, axis=-1, keepdims=True)
    return x * lax.rsqrt(ms + RMS_EPS) * g


def _const_spec(shape):
    zeros = (0,) * len(shape)
    return pl.BlockSpec(shape, lambda *_: zeros)


TOK_ROWS = D_MODEL // V7X_LANES


def _load_token_major(ref, row0, n_tok, j):
    return ref[pl.ds(row0 + j, n_tok, stride=TOK_ROWS), :]


def _store_token_major(ref, row0, x):
    n_tok = x.shape[0]
    for j in range(TOK_ROWS):
        ref[pl.ds(row0 + j, n_tok, stride=TOK_ROWS), :] = x[:, j * V7X_LANES:(j + 1) * V7X_LANES]


_C_Q = (0, Q_LORA)
_C_KV = (Q_LORA, Q_LORA + KV_LORA)
_C_KPE = (_C_KV[1], _C_KV[1] + 2 * QK_ROPE)
_C_A = (_C_KPE[1], _C_KPE[1] + CONV_CH)
_C_G = (_C_A[1], _C_A[1] + CONV_CH)
IN_AUG = _C_G[1]


def _in_proj_kernel(x_ref, lng_ref, lnb_ref, w1_ref, qg_ref, kvg_ref, wq_ref, wuk_ref, wuv_ref, cs_ref,
                    q_ref, k_ref, v_ref, glu_ref):
    h0 = _layer_norm(x_ref[...], lng_ref[...], lnb_ref[...])
    hb = h0.astype(BF16)
    cs = cs_ref[...]

    def rope(t128):
        t = t128 * cs
        return t + pltpu.roll(t, QK_ROPE, axis=1)

    cq = _dot(hb, w1_ref[:, _C_Q[0]:_C_Q[1]])
    cqn = _rms_norm(cq, qg_ref[...]).astype(BF16)
    ckv = _dot(hb, w1_ref[:, _C_KV[0]:_C_KV[1]])
    ckvn = _rms_norm(ckv, kvg_ref[...]).astype(BF16)
    kpe = rope(_dot(hb, w1_ref[:, _C_KPE[0]:_C_KPE[1]]))[:, :QK_ROPE].astype(BF16)

    a = _dot(hb, w1_ref[:, _C_A[0]:_C_A[1]])
    g = _dot(hb, w1_ref[:, _C_G[0]:_C_G[1]])
    glu_ref[...] = a * jax.nn.sigmoid(g)

    knope = _dot(ckvn, wuk_ref[...])
    v = _dot(ckvn, wuv_ref[...])
    for h in range(N_HEADS):
        k_ref[h, :, 0:QK_NOPE] = knope[:, h * QK_NOPE:(h + 1) * QK_NOPE].astype(BF16)
        k_ref[h, :, QK_NOPE:QK_DIM] = kpe
        v_ref[h] = v[:, h * V_DIM:(h + 1) * V_DIM].astype(BF16)

    qn = _dot(cqn, wq_ref[:, 0:N_HEADS * QK_NOPE])
    qp = _dot(cqn, wq_ref[:, N_HEADS * QK_NOPE:])
    for h in range(N_HEADS):
        q_ref[h, :, 0:QK_NOPE] = qn[:, h * QK_NOPE:(h + 1) * QK_NOPE].astype(BF16)
        q_ref[h, :, QK_NOPE:QK_DIM] = rope(qp[:, h * V7X_LANES:(h + 1) * V7X_LANES])[:, :QK_ROPE].astype(BF16)


def _in_proj(x2d, lng, lnb, w1, qg, kvg, wq, wuk, wuv, cs, tm):
    rows = x2d.shape[0]
    n_cs = cs.shape[0] // tm
    row = lambda i: (i, 0)
    head_row = lambda i: (0, i, 0)
    return pl.pallas_call(
        _in_proj_kernel,
        grid=(rows // tm,),
        in_specs=[
            pl.BlockSpec((tm, D_MODEL), row),
            _const_spec((1, D_MODEL)), _const_spec((1, D_MODEL)),
            _const_spec((D_MODEL, IN_AUG)),
            _const_spec((1, Q_LORA)), _const_spec((1, KV_LORA)),
            _const_spec((Q_LORA, 2 * N_HEADS * QK_NOPE)),
            _const_spec((KV_LORA, N_HEADS * QK_NOPE)), _const_spec((KV_LORA, MLA_WIDTH)),
            pl.BlockSpec((tm, V7X_LANES), lambda i: (i % n_cs, 0)),
        ],
        out_specs=[
            pl.BlockSpec((N_HEADS, tm, QK_DIM), head_row),
            pl.BlockSpec((N_HEADS, tm, QK_DIM), head_row),
            pl.BlockSpec((N_HEADS, tm, V_DIM), head_row),
            pl.BlockSpec((tm, CONV_CH), row),
        ],
        out_shape=[
            jax.ShapeDtypeStruct((N_HEADS, rows, QK_DIM), BF16),
            jax.ShapeDtypeStruct((N_HEADS, rows, QK_DIM), BF16),
            jax.ShapeDtypeStruct((N_HEADS, rows, V_DIM), BF16),
            jax.ShapeDtypeStruct((rows, CONV_CH), F32),
        ],
        compiler_params=pltpu.CompilerParams(
            dimension_semantics=("arbitrary",), vmem_limit_bytes=V7X_VMEM_LIMIT),
        name="in_proj",
    )(x2d, lng, lnb, w1, qg, kvg, wq, wuk, wuv, cs)


def _attention_kernel(q_ref, k_ref, v_ref, km_ref, vm_ref, o_ref):
    i = pl.program_id(2)
    scale = 1.0 / math.sqrt(QK_DIM)
    q = q_ref[0]

    s = _dot_nt(q, km_ref[0]) * scale
    m = jnp.max(s, axis=1, keepdims=True)
    p = jnp.exp(s - m)
    l = jnp.sum(p, axis=1, keepdims=True)
    acc = _dot(p.astype(BF16), vm_ref[0])

    def block(j, carry, masked):
        m, l, acc = carry
        start = pl.multiple_of(j * ATT_TILE, ATT_TILE)
        kb = k_ref[0, pl.ds(start, ATT_TILE), :]
        vb = v_ref[0, pl.ds(start, ATT_TILE), :]
        s = _dot_nt(q, kb) * scale
        if masked:
            r = lax.broadcasted_iota(jnp.int32, s.shape, 0)
            c = lax.broadcasted_iota(jnp.int32, s.shape, 1)
            s = jnp.where(c <= r, s, -1e30)
        m_new = jnp.maximum(m, jnp.max(s, axis=1, keepdims=True))
        alpha = jnp.exp(m - m_new)
        p = jnp.exp(s - m_new)
        l = alpha * l + jnp.sum(p, axis=1, keepdims=True)
        acc = alpha * acc + _dot(p.astype(BF16), vb)
        return m_new, l, acc

    carry = lax.fori_loop(0, i, lambda j, c: block(j, c, False), (m, l, acc))
    m, l, acc = block(i, carry, True)
    o_ref[...] = (acc / l).astype(BF16)


def _attention(q, k, v, km, vm, batch, seq):
    nq = seq // ATT_TILE
    return pl.pallas_call(
        _attention_kernel,
        grid=(batch, N_HEADS, nq),
        in_specs=[
            pl.BlockSpec((1, ATT_TILE, QK_DIM), lambda b, h, i: (h, b * nq + i, 0)),
            pl.BlockSpec((1, seq, QK_DIM), lambda b, h, i: (h, b, 0)),
            pl.BlockSpec((1, seq, V_DIM), lambda b, h, i: (h, b, 0)),
            pl.BlockSpec((1, N_META, QK_DIM), lambda b, h, i: (h, 0, 0)),
            pl.BlockSpec((1, N_META, V_DIM), lambda b, h, i: (h, 0, 0)),
        ],
        out_specs=pl.BlockSpec((ATT_TILE, V_DIM), lambda b, h, i: (b * nq + i, h)),
        out_shape=jax.ShapeDtypeStruct((batch * seq, MLA_WIDTH), BF16),
        compiler_params=pltpu.CompilerParams(
            dimension_semantics=("arbitrary", "arbitrary", "arbitrary"), vmem_limit_bytes=V7X_VMEM_LIMIT),
        name="attention",
    )(q, k, v, km, vm)


def _conv_kernel(cur_ref, prev_ref, meta_ref, w_ref, cb_ref, lng_ref, lnb_ref, o_ref, win_ref, acc_ref):
    i = pl.program_id(1)

    @pl.when(i == 0)
    def _():
        win_ref[0:CONV_HALO - N_META, :] = jnp.zeros((CONV_HALO - N_META, CONV_CH), F32)
        win_ref[CONV_HALO - N_META:CONV_HALO, :] = meta_ref[...]

    @pl.when(i > 0)
    def _():
        win_ref[0:CONV_HALO, :] = prev_ref[...]

    win_ref[CONV_HALO:, :] = cur_ref[...]

    base = CONV_HALO - (CONV_W - 1)

    for rc in range(CONV_TILE // CONV_ROWS):
        r0 = rc * CONV_ROWS
        for c in range(CONV_CH // CONV_LANES):
            lanes = pl.ds(c * CONV_LANES, CONV_LANES)
            acc = jnp.zeros((CONV_ROWS, CONV_LANES), F32)
            for k in range(CONV_W):
                acc = acc + win_ref[pl.ds(r0 + base + k, CONV_ROWS), lanes] * w_ref[k:k + 1, lanes]
            acc_ref[pl.ds(r0, CONV_ROWS), lanes] = acc

    y = _layer_norm(acc_ref[...] + cb_ref[...], lng_ref[...], lnb_ref[...])
    o_ref[...] = (y * jax.nn.sigmoid(y)).astype(BF16)


def _conv(glu, glu_meta, w, cb, lng, lnb, batch, seq):
    nt = seq // CONV_TILE
    per = CONV_TILE // CONV_HALO
    return pl.pallas_call(
        _conv_kernel,
        grid=(batch, nt),
        in_specs=[
            pl.BlockSpec((CONV_TILE, CONV_CH), lambda b, i: (b * nt + i, 0)),
            pl.BlockSpec((CONV_HALO, CONV_CH), lambda b, i: (jnp.maximum((b * nt + i) * per - 1, 0), 0)),
            _const_spec((N_META, CONV_CH)),
            _const_spec((CONV_HALO, CONV_CH)),
            _const_spec((1, CONV_CH)), _const_spec((1, CONV_CH)), _const_spec((1, CONV_CH)),
        ],
        out_specs=pl.BlockSpec((CONV_TILE, CONV_CH), lambda b, i: (b * nt + i, 0)),
        out_shape=jax.ShapeDtypeStruct((batch * seq, CONV_CH), BF16),
        scratch_shapes=[pltpu.VMEM((CONV_TILE + CONV_HALO, CONV_CH), F32),
                        pltpu.VMEM((CONV_TILE, CONV_CH), F32)],
        compiler_params=pltpu.CompilerParams(
            dimension_semantics=("arbitrary", "arbitrary"), vmem_limit_bytes=V7X_VMEM_LIMIT),
        name="conv",
    )(glu, glu, glu_meta, w, cb, lng, lnb)


def _out_proj_kernel(attn_ref, conv_ref, x_ref, lng_ref, lnb_ref, wo_ref, g1_ref, b1_ref,
                     wrh_ref, wrl_ref, br_ref,
                     h1_ref, idx_ref, rank_ref, gate_ref, cnt_ref, carry_ref):
    step = pl.program_id(0)
    tm = x_ref.shape[0]

    @pl.when(step == 0)
    def _():
        carry_ref[...] = jnp.zeros_like(carry_ref)

    h0 = _layer_norm(x_ref[...], lng_ref[...], lnb_ref[...])
    mix = _dot(attn_ref[...], wo_ref[0:MLA_WIDTH, :]) + _dot(conv_ref[...], wo_ref[MLA_WIDTH:, :])
    h1 = _layer_norm(DEEPNORM_ALPHA * h0 + mix, g1_ref[...], b1_ref[...])
    _store_token_major(h1_ref, 0, h1)

    hi = h1.astype(BF16)
    lo = (h1 - hi.astype(F32)).astype(BF16)
    logits = (_dot(hi, wrh_ref[...]) + (_dot(hi, wrl_ref[...]) + _dot(lo, wrh_ref[...]))) + br_ref[...]

    lane = lax.broadcasted_iota(jnp.int32, (tm, N_EXPERTS), 1)
    work = logits
    vals, idxs = [], []
    for _ in range(TOP_K):
        mx = jnp.max(work, axis=1, keepdims=True)
        ix = jnp.min(jnp.where(work == mx, lane, N_EXPERTS), axis=1, keepdims=True)
        vals.append(mx)
        idxs.append(ix)
        work = jnp.where(lane == ix, -jnp.inf, work)
    exps = [jnp.exp(v - vals[0]) for v in vals]
    denom = exps[0] + exps[1] + exps[2] + exps[3]

    onehots = [(lane == ix) for ix in idxs]
    chosen = (onehots[0] | onehots[1] | onehots[2] | onehots[3])
    chosen_f = jnp.where(chosen, 1.0, 0.0)
    r = lax.broadcasted_iota(jnp.int32, (tm, tm), 0)
    c = lax.broadcasted_iota(jnp.int32, (tm, tm), 1)
    lower = jnp.where(c < r, 1.0, 0.0).astype(BF16)
    before = _dot(lower, chosen_f.astype(BF16)) + carry_ref[...]

    out_lane = lax.broadcasted_iota(jnp.int32, (tm, V7X_LANES), 1)
    idx_out = jnp.zeros((tm, V7X_LANES), jnp.int32)
    rank_out = jnp.zeros((tm, V7X_LANES), jnp.int32)
    gate_out = jnp.zeros((tm, V7X_LANES), F32)
    for k in range(TOP_K):
        rank_k = jnp.sum(jnp.where(onehots[k], before, 0.0), axis=1, keepdims=True).astype(jnp.int32)
        idx_out = jnp.where(out_lane == k, idxs[k], idx_out)
        rank_out = jnp.where(out_lane == k, rank_k, rank_out)
        gate_out = jnp.where(out_lane == k, exps[k] / denom, gate_out)
    idx_ref[...] = idx_out
    rank_ref[...] = rank_out
    gate_ref[...] = gate_out

    carry_ref[...] = carry_ref[...] + jnp.sum(chosen_f, axis=0, keepdims=True)
    cnt_ref[...] = carry_ref[...].astype(jnp.int32)


def _out_proj(attn, conv, x2d, lng, lnb, wo, g1, b1, wrh, wrl, br):
    rows = x2d.shape[0]
    tm = ROW_TILE
    row = lambda i: (i, 0)
    return pl.pallas_call(
        _out_proj_kernel,
        grid=(rows // tm,),
        in_specs=[
            pl.BlockSpec((tm, MLA_WIDTH), row), pl.BlockSpec((tm, CONV_CH), row),
            pl.BlockSpec((tm, D_MODEL), row),
            _const_spec((1, D_MODEL)), _const_spec((1, D_MODEL)),
            _const_spec((D_MODEL, D_MODEL)),
            _const_spec((1, D_MODEL)), _const_spec((1, D_MODEL)),
            _const_spec((D_MODEL, N_EXPERTS)), _const_spec((D_MODEL, N_EXPERTS)),
            _const_spec((1, N_EXPERTS)),
        ],
        out_specs=[
            pl.BlockSpec((tm * TOK_ROWS, V7X_LANES), row),
            pl.BlockSpec((tm, V7X_LANES), row), pl.BlockSpec((tm, V7X_LANES), row),
            pl.BlockSpec((tm, V7X_LANES), row),
            _const_spec((1, N_EXPERTS)),
        ],
        out_shape=[
            jax.ShapeDtypeStruct((rows * TOK_ROWS, V7X_LANES), F32),
            jax.ShapeDtypeStruct((rows, V7X_LANES), jnp.int32),
            jax.ShapeDtypeStruct((rows, V7X_LANES), jnp.int32),
            jax.ShapeDtypeStruct((rows, V7X_LANES), F32),
            jax.ShapeDtypeStruct((1, N_EXPERTS), jnp.int32),
        ],
        scratch_shapes=[pltpu.VMEM((1, N_EXPERTS), F32)],
        compiler_params=pltpu.CompilerParams(
            dimension_semantics=("arbitrary",), vmem_limit_bytes=V7X_VMEM_LIMIT),
        name="out_proj_router",
    )(attn, conv, x2d, lng, lnb, wo, g1, b1, wrh, wrl, br)


def _token_copy(src_ref, src_tok, dst_ref, dst_tok, sem, n_tok=1):
    rows = n_tok * TOK_ROWS
    src = src_ref.at[pl.ds(pl.multiple_of(src_tok * TOK_ROWS, TOK_ROWS), rows), :]
    dst = dst_ref.at[pl.ds(pl.multiple_of(dst_tok * TOK_ROWS, TOK_ROWS), rows), :]
    return pltpu.make_async_copy(src, dst, sem)


def _dispatch_kernel(dest_ref, padrow_ref, npad_ref, h1_ref, xs_ref, zero_ref, sem, zsem):
    step = pl.program_id(0)
    base = step * (DISPATCH_TILE * TOP_K)

    def copies(t):
        return [_token_copy(h1_ref, t, xs_ref, dest_ref[base + t * TOP_K + k], sem) for k in range(TOP_K)]

    def start(t, _):
        for c in copies(t):
            c.start()
        return 0

    def wait(t, _):
        for c in copies(t):
            c.wait()
        return 0

    lax.fori_loop(0, DISPATCH_TILE, start, 0)

    @pl.when(step == 0)
    def _():
        zero_ref[...] = jnp.zeros_like(zero_ref)
        n = npad_ref[0]
        tail_start = npad_ref[1]
        n_tail = (xs_ref.shape[0] // TOK_ROWS - tail_start) // SEG_ALIGN

        def zstart(p, _):
            _token_copy(zero_ref, 0, xs_ref, padrow_ref[p], zsem).start()
            return 0

        def zwait(p, _):
            _token_copy(zero_ref, 0, xs_ref, padrow_ref[p], zsem).wait()
            return 0

        def tstart(b, _):
            _token_copy(zero_ref, 0, xs_ref, tail_start + b * SEG_ALIGN, zsem, SEG_ALIGN).start()
            return 0

        def twait(b, _):
            _token_copy(zero_ref, 0, xs_ref, tail_start + b * SEG_ALIGN, zsem, SEG_ALIGN).wait()
            return 0

        lax.fori_loop(0, n, zstart, 0)
        lax.fori_loop(0, n, zwait, 0)
        lax.fori_loop(0, n_tail, tstart, 0)
        lax.fori_loop(0, n_tail, twait, 0)

    lax.fori_loop(0, DISPATCH_TILE, wait, 0)


def _dispatch(dest_flat, pad_rows, n_pad, h1t, n_rows):
    tokens = h1t.shape[0] // TOK_ROWS
    return pl.pallas_call(
        _dispatch_kernel,
        grid_spec=pltpu.PrefetchScalarGridSpec(
            num_scalar_prefetch=3,
            grid=(tokens // DISPATCH_TILE,),
            in_specs=[pl.BlockSpec((DISPATCH_TILE * TOK_ROWS, V7X_LANES), lambda i, *_: (i, 0))],
            out_specs=pl.BlockSpec(memory_space=pl.ANY),
            scratch_shapes=[pltpu.VMEM((SEG_ALIGN * TOK_ROWS, V7X_LANES), F32),
                            pltpu.SemaphoreType.DMA, pltpu.SemaphoreType.DMA],
        ),
        out_shape=jax.ShapeDtypeStruct((n_rows * TOK_ROWS, V7X_LANES), F32),
        compiler_params=pltpu.CompilerParams(
            dimension_semantics=("arbitrary",), has_side_effects=True),
        name="dispatch",
    )(dest_flat, pad_rows, n_pad, h1t)


N_FF = D_FF // FF_TILE
SUB = SEG_ALIGN
N_SUB = SUPER_ROWS // SUB
PREFETCH_SLOTS = -(-N_SUB // N_FF)
OUT_SLOTS = MM_ROWS // SUB


def _experts_kernel(st_e_ref, st_start_ref, st_rows_ref, n_used_ref,
                    xs_ref, w1g_ref, w1u_ref, b1g_ref, b1u_ref, w2_ref, b2_ref,
                    ys_ref,
                    xb_ref, acc_ref, wg_ref, wu_ref, wd_ref, xstage_ref, ostage_ref, xsem, osem):
    s = pl.program_id(0)
    j = pl.program_id(1)
    parity = s % 2
    rows = st_rows_ref[s]
    start = st_start_ref[s]
    n_blk = rows // SUB
    next_start = st_start_ref[s + 1]
    next_blk = st_rows_ref[s + 1] // SUB

    def x_copy(tok0, slot):
        return _token_copy(xs_ref, tok0, xstage_ref.at[slot], 0, xsem.at[slot], SUB)

    def y_copy(slot, tok0):
        return _token_copy(ostage_ref.at[slot], 0, ys_ref, tok0, osem.at[slot], SUB)

    def convert(slot, par, blk):
        off = pl.multiple_of(blk * SUB, SUB)
        for c in range(TOK_ROWS):
            xb_ref[par, pl.ds(off, SUB), c * V7X_LANES:(c + 1) * V7X_LANES] = (
                _load_token_major(xstage_ref.at[slot], 0, SUB, c).astype(BF16))

    @pl.when((s == 0) & (j == 0))
    def _():
        ostage_ref[0] = jnp.zeros(ostage_ref.shape[1:], F32)
        tail_start = n_used_ref[1]
        n_tail = (ys_ref.shape[0] // TOK_ROWS - tail_start) // SUB

        def tstart(b, _):
            y_copy(0, tail_start + b * SUB).start()
            return 0

        def twait(b, _):
            y_copy(0, tail_start + b * SUB).wait()
            return 0

        lax.fori_loop(0, n_tail, tstart, 0)
        lax.fori_loop(0, n_tail, twait, 0)

        def first(b, _):
            x_copy(start + b * SUB, 0).start()
            x_copy(start + b * SUB, 0).wait()
            convert(0, 0, b)
            return 0

        lax.fori_loop(0, n_blk, first, 0)

    for p in range(PREFETCH_SLOTS):
        @pl.when(j * PREFETCH_SLOTS + p < next_blk)
        def _(p=p):
            x_copy(next_start + (j * PREFETCH_SLOTS + p) * SUB, p).start()

    def compute(last):
        wg_ref[...] = w1g_ref[0].astype(BF16)
        wu_ref[...] = w1u_ref[0].astype(BF16)
        wd_ref[...] = w2_ref[0].astype(BF16)
        bg = b1g_ref[0]
        bu = b1u_ref[0]
        b2 = b2_ref[0]

        def chunk(row0, m):
            off = pl.multiple_of(row0, SUB)
            xb = xb_ref[parity, pl.ds(off, m), :]
            g = _dot(xb, wg_ref[...]) + bg
            u = _dot(xb, wu_ref[...]) + bu
            g = jnp.minimum(g, SWIGLU_LIMIT)
            u = jnp.clip(u, -SWIGLU_LIMIT, SWIGLU_LIMIT)
            act = g * jax.nn.sigmoid(SWIGLU_ALPHA * g) * (u + 1.0)
            y = _dot(act.astype(BF16), wd_ref[...])
            if not last:
                acc_ref[pl.ds(off, m), :] += y
                return
            y = acc_ref[pl.ds(off, m), :] + y + b2
            for i in range(m // SUB):
                blk = row0 // SUB + i
                slot = blk % OUT_SLOTS

                @pl.when(blk >= OUT_SLOTS)
                def _(blk=blk, slot=slot):
                    y_copy(slot, start + (blk - OUT_SLOTS) * SUB).wait()

                _store_token_major(ostage_ref.at[slot], 0, y[i * SUB:(i + 1) * SUB, :])
                y_copy(slot, start + blk * SUB).start()

        n_big = rows // MM_ROWS

        def big(r, _):
            chunk(r * MM_ROWS, MM_ROWS)
            return 0

        lax.fori_loop(0, n_big, big, 0)
        done = n_big * MM_ROWS
        m = MM_ROWS // 2
        while m >= SUB:
            take = ((rows - done) & m) != 0

            @pl.when(take)
            def _(done=done, m=m):
                chunk(done, m)

            done = done + jnp.where(take, m, 0)
            m //= 2

        if last:
            for i in range(OUT_SLOTS):
                @pl.when(i < n_blk)
                def _(i=i):
                    blk = n_blk - 1 - i
                    y_copy(blk % OUT_SLOTS, start + blk * SUB).wait()

    @pl.when((j == 0) & (rows > 0))
    def _():
        def zero(b, _):
            acc_ref[pl.ds(pl.multiple_of(b * SUB, SUB), SUB), :] = jnp.zeros((SUB, D_MODEL), F32)
            return 0

        lax.fori_loop(0, n_blk, zero, 0)

    @pl.when((j < N_FF - 1) & (rows > 0))
    def _():
        compute(False)

    @pl.when((j == N_FF - 1) & (rows > 0))
    def _():
        compute(True)

    for p in range(PREFETCH_SLOTS):
        @pl.when(j * PREFETCH_SLOTS + p < next_blk)
        def _(p=p):
            blk = j * PREFETCH_SLOTS + p
            x_copy(next_start + blk * SUB, p).wait()
            convert(p, 1 - parity, blk)


def _experts(st_e, st_start, st_rows, n_used, xs, w1, b1, w2, b2, n_super):
    n_rows = xs.shape[0] // TOK_ROWS

    def ff(s, j, n_used_ref):
        return jnp.where(s < n_used_ref[0], j, N_FF - 1)

    w1g_map = lambda s, j, e, st, rw, nu: (e[s], 0, ff(s, j, nu))
    w1u_map = lambda s, j, e, st, rw, nu: (e[s], 0, N_FF + ff(s, j, nu))
    w2_map = lambda s, j, e, st, rw, nu: (e[s], ff(s, j, nu), 0)
    b2_map = lambda s, j, e, st, rw, nu: (e[s], 0, 0)
    return pl.pallas_call(
        _experts_kernel,
        grid_spec=pltpu.PrefetchScalarGridSpec(
            num_scalar_prefetch=4,
            grid=(n_super, N_FF),
            in_specs=[
                pl.BlockSpec(memory_space=pl.ANY),
                pl.BlockSpec((1, D_MODEL, FF_TILE), w1g_map),
                pl.BlockSpec((1, D_MODEL, FF_TILE), w1u_map),
                pl.BlockSpec((1, 1, FF_TILE), w1g_map),
                pl.BlockSpec((1, 1, FF_TILE), w1u_map),
                pl.BlockSpec((1, FF_TILE, D_MODEL), w2_map),
                pl.BlockSpec((1, 1, D_MODEL), b2_map),
            ],
            out_specs=pl.BlockSpec(memory_space=pl.ANY),
            scratch_shapes=[
                pltpu.VMEM((2, SUPER_ROWS, D_MODEL), BF16),
                pltpu.VMEM((SUPER_ROWS, D_MODEL), F32),
                pltpu.VMEM((D_MODEL, FF_TILE), BF16),
                pltpu.VMEM((D_MODEL, FF_TILE), BF16),
                pltpu.VMEM((FF_TILE, D_MODEL), BF16),
                pltpu.VMEM((PREFETCH_SLOTS, SUB * TOK_ROWS, V7X_LANES), F32),
                pltpu.VMEM((OUT_SLOTS, SUB * TOK_ROWS, V7X_LANES), F32),
                pltpu.SemaphoreType.DMA((PREFETCH_SLOTS,)),
                pltpu.SemaphoreType.DMA((OUT_SLOTS,)),
            ],
        ),
        out_shape=jax.ShapeDtypeStruct((n_rows * TOK_ROWS, V7X_LANES), F32),
        compiler_params=pltpu.CompilerParams(
            dimension_semantics=("arbitrary", "arbitrary"), vmem_limit_bytes=V7X_VMEM_LIMIT,
            has_side_effects=True),
        name="experts",
    )(st_e, st_start, st_rows, n_used, xs, w1, w1, b1, b1, w2, b2)


def _combine_kernel(dest_ref, ys_ref, gate_ref, h1_ref, g2_ref, b2_ref, o_ref, buf_ref, sem):
    step = pl.program_id(0)
    n_steps = pl.num_programs(0)
    tm = COMBINE_TILE

    def copies(at_step, slot, t):
        base = at_step * (tm * TOP_K)
        return [_token_copy(ys_ref, dest_ref[base + t * TOP_K + k], buf_ref.at[slot, k], t, sem.at[slot])
                for k in range(TOP_K)]

    def gather(at_step, slot):
        def start(t, _):
            for c in copies(at_step, slot, t):
                c.start()
            return 0

        lax.fori_loop(0, tm, start, 0)

    @pl.when(step == 0)
    def _():
        gather(0, 0)

    @pl.when(step + 1 < n_steps)
    def _():
        gather(step + 1, (step + 1) % 2)

    slot = step % 2

    def wait(t, _):
        for c in copies(step, slot, t):
            c.wait()
        return 0

    lax.fori_loop(0, tm, wait, 0)

    gates = gate_ref[...]
    for c in range(TOK_ROWS):
        z = DEEPNORM_ALPHA * _load_token_major(h1_ref, 0, tm, c)
        for k in range(TOP_K):
            z = z + _load_token_major(buf_ref.at[slot, k], 0, tm, c) * gates[:, k:k + 1]
        o_ref[:, c * V7X_LANES:(c + 1) * V7X_LANES] = z
    o_ref[...] = _layer_norm(o_ref[...], g2_ref[...], b2_ref[...])


def _combine(dest_flat, ys, gates, h1t, g2, b2):
    tokens = h1t.shape[0] // TOK_ROWS
    tm = COMBINE_TILE
    row = lambda i, *_: (i, 0)
    return pl.pallas_call(
        _combine_kernel,
        grid_spec=pltpu.PrefetchScalarGridSpec(
            num_scalar_prefetch=1,
            grid=(tokens // tm,),
            in_specs=[
                pl.BlockSpec(memory_space=pl.ANY),
                pl.BlockSpec((tm, V7X_LANES), row),
                pl.BlockSpec((tm * TOK_ROWS, V7X_LANES), row),
                pl.BlockSpec((1, D_MODEL), lambda i, *_: (0, 0)),
                pl.BlockSpec((1, D_MODEL), lambda i, *_: (0, 0)),
            ],
            out_specs=pl.BlockSpec((tm, D_MODEL), row),
            scratch_shapes=[pltpu.VMEM((2, TOP_K, tm * TOK_ROWS, V7X_LANES), F32),
                            pltpu.SemaphoreType.DMA((2,))],
        ),
        out_shape=jax.ShapeDtypeStruct((tokens, D_MODEL), F32),
        compiler_params=pltpu.CompilerParams(
            dimension_semantics=("arbitrary",), vmem_limit_bytes=V7X_VMEM_LIMIT),
        name="combine",
    )(dest_flat, ys, gates, h1t, g2, b2)


def _rotate_half_cols(w):
    half = QK_ROPE // 2
    return jnp.concatenate([-w[..., half:], w[..., :half]], axis=-1)


def _rope_table(length):
    inv_freq = 1.0 / (ROPE_THETA ** (jnp.arange(0, QK_ROPE, 2, dtype=F32) / QK_ROPE))
    freqs = jnp.arange(length, dtype=F32)[:, None] * inv_freq[None, :]
    emb = jnp.concatenate([freqs, freqs], axis=-1)
    return jnp.concatenate([jnp.cos(emb), jnp.sin(emb)], axis=-1)


def _routing_plan(idx, rank, counts, n_super):
    experts = jnp.arange(N_EXPERTS, dtype=jnp.int32)

    def lookup(table, i):
        return jnp.sum(jnp.where(i[..., None] == experts, table, 0), axis=-1)

    def bucket(cum, i):
        return jnp.minimum(jnp.sum((cum <= i[..., None]).astype(jnp.int32), axis=-1), N_EXPERTS - 1)

    counts = counts.astype(jnp.int32)
    padded = (counts + SEG_ALIGN - 1) // SEG_ALIGN * SEG_ALIGN
    pad_end = jnp.cumsum(padded)
    pad_start = pad_end - padded
    dest = (lookup(pad_start, idx) + rank).reshape(-1).astype(jnp.int32)

    n_padmax = N_EXPERTS * SEG_ALIGN
    padcnt = padded - counts
    padcum = jnp.cumsum(padcnt)
    p = jnp.arange(n_padmax, dtype=jnp.int32)
    pe = bucket(padcum, p)
    pad_rows = lookup(pad_start + counts - (padcum - padcnt), pe) + p
    n_pad = jnp.stack([padcum[-1], pad_end[-1]]).astype(jnp.int32)
    pad_rows = jnp.where(p < n_pad[0], pad_rows, 0).astype(jnp.int32)

    n_st = (padded + SUPER_ROWS - 1) // SUPER_ROWS
    st_cum = jnp.cumsum(n_st)
    n_used = jnp.stack([st_cum[-1], pad_end[-1]]).astype(jnp.int32)
    s = jnp.arange(n_super + 1, dtype=jnp.int32)
    s_eff = jnp.minimum(s, n_used[0] - 1)
    se = bucket(st_cum, s_eff).astype(jnp.int32)
    local = s_eff - lookup(st_cum - n_st, se)
    st_start = (lookup(pad_start, se) + local * SUPER_ROWS).astype(jnp.int32)
    st_rows = jnp.clip(lookup(padded, se) - local * SUPER_ROWS, 0, SUPER_ROWS)
    st_rows = jnp.where(s < n_used[0], st_rows, 0).astype(jnp.int32)
    return dest, pad_rows, n_pad, se, st_start, st_rows, n_used


def kernel(x, meta_tokens, ln_in_g, ln_in_b, w_in, q_norm_g, w_uq, kv_norm_g, w_uk, w_uv, conv_dw_w,
           conv_dw_b, conv_ln_g, conv_ln_b, w_out, ln1_g, ln1_b, w_router, b_router, w_mlp1, b_mlp1,
           w_mlp2, b_mlp2, ln2_g, ln2_b):
    batch, seq, _ = x.shape
    tokens = batch * seq
    row2 = lambda a: a.reshape(1, -1)

    wi = w_in[0]
    s_kpe = Q_LORA + KV_LORA
    kpe_w = wi[:, s_kpe:s_kpe + QK_ROPE]
    w1 = jnp.concatenate(
        [wi[:, :s_kpe], kpe_w, _rotate_half_cols(kpe_w), wi[:, s_kpe + QK_ROPE:]], axis=1).astype(BF16)
    wq3 = w_uq[0].reshape(Q_LORA, N_HEADS, QK_DIM)
    wq_nope = wq3[:, :, :QK_NOPE].reshape(Q_LORA, N_HEADS * QK_NOPE)
    wq_pe = wq3[:, :, QK_NOPE:]
    wq_pr = jnp.concatenate([wq_pe, _rotate_half_cols(wq_pe)], axis=-1).reshape(Q_LORA, N_HEADS * 2 * QK_ROPE)
    wq = jnp.concatenate([wq_nope, wq_pr], axis=1).astype(BF16)
    wuk = w_uk[0].astype(BF16)
    wuv = w_uv[0].astype(BF16)
    wo = w_out[0].astype(BF16)
    wr = w_router[0]
    wr_hi = wr.astype(BF16)
    wr_lo = (wr - wr_hi.astype(F32)).astype(BF16)
    cs = _rope_table(N_META + seq)
    conv_w = jnp.concatenate([conv_dw_w[0], jnp.zeros((CONV_HALO - CONV_W, CONV_CH), F32)], axis=0)

    x2d = x.reshape(tokens, D_MODEL)
    proj_args = (row2(ln_in_g), row2(ln_in_b), w1, row2(q_norm_g[0]), row2(kv_norm_g[0]), wq, wuk, wuv)

    _, k_meta, v_meta, glu_meta = _in_proj(meta_tokens, *proj_args, cs[:N_META], N_META)
    q, k, v, glu = _in_proj(x2d, *proj_args, cs[N_META:], ROW_TILE)
    attn = _attention(q, k, v, k_meta, v_meta, batch, seq)
    conv = _conv(glu, glu_meta, conv_w, row2(conv_dw_b[0]), row2(conv_ln_g[0]), row2(conv_ln_b[0]), batch, seq)

    h1t, idx, rank, gates, counts = _out_proj(
        attn, conv, x2d, row2(ln_in_g), row2(ln_in_b), wo, row2(ln1_g[0]), row2(ln1_b[0]),
        wr_hi, wr_lo, row2(b_router[0]))

    n_assign = tokens * TOP_K
    n_rows = n_assign + N_EXPERTS * SEG_ALIGN
    n_super = N_EXPERTS + -(-n_assign // SUPER_ROWS)
    dest, pad_rows, n_pad, st_e, st_start, st_rows, n_used = _routing_plan(
        idx[:, :TOP_K], rank[:, :TOP_K], counts[0], n_super)

    xs = _dispatch(dest, pad_rows, n_pad, h1t, n_rows)
    ys = _experts(st_e, st_start, st_rows, n_used, xs, w_mlp1[0], b_mlp1[0].reshape(N_EXPERTS, 1, 2 * D_FF),
                  w_mlp2[0], b_mlp2[0].reshape(N_EXPERTS, 1, D_MODEL), n_super)
    out = _combine(dest, ys, gates, h1t, row2(ln2_g[0]), row2(ln2_b[0]))
    return out.reshape(batch, seq, D_MODEL)
```

```python
import functools
import math

import jax
import jax.numpy as jnp
from jax import lax
from jax.experimental import pallas as pl
from jax.experimental.pallas import tpu as pltpu

D_MODEL = 2048
N_META = 16
N_HEADS = 8
QK_NOPE = 128
QK_ROPE = 64
QK_DIM = QK_NOPE + QK_ROPE
V_DIM = 128
Q_LORA = 768
KV_LORA = 512
ROPE_THETA = 10000.0
MLA_WIDTH = N_HEADS * V_DIM
CONV_CH = 1024
CONV_W = 31
N_EXPERTS = 32
TOP_K = 4
D_FF = 2048
SWIGLU_LIMIT = 7.0
SWIGLU_ALPHA = 1.702
DEEPNORM_ALPHA = 2.0 ** 0.25
LN_EPS = 1e-5
RMS_EPS = 1e-6

V7X_LANES = 128
V7X_VMEM_LIMIT = 56 * 1024 * 1024

ROW_TILE = 256
ATT_TILE = 512
ATT_CHAINS = 2
CONV_TILE = 256
CONV_HALO = 32
CONV_ROWS = 32
CONV_LANES = 256
DISPATCH_TILE = 256
COMBINE_TILE = 128
SEG_ALIGN = 128
SUPER_ROWS = 1536
FF_TILE = 256
MM_ROWS = 512

F32 = jnp.float32
BF16 = jnp.bfloat16


def _dot(a, b):
    return jnp.dot(a, b, preferred_element_type=F32)


def _dot_nt(a, b):
    return lax.dot_general(a, b, (((1,), (1,)), ((), ())), preferred_element_type=F32)


def _layer_norm(x, g, b):
    mu = jnp.mean(x, axis=-1, keepdims=True)
    xc = x - mu
    var = jnp.mean(xc * xc, axis=-1, keepdims=True)
    return xc * lax.rsqrt(var + LN_EPS) * g + b


def _rms_norm(x, g):
    ms = jnp.mean(x * x, axis=-1, keepdims=True)
    return x * lax.rsqrt(ms + RMS_EPS) * g


def _const_spec(shape):
    zeros = (0,) * len(shape)
    return pl.BlockSpec(shape, lambda *_: zeros)


TOK_ROWS = D_MODEL // V7X_LANES


def _load_token_major(ref, row0, n_tok, j):
    return ref[pl.ds(row0 + j, n_tok, stride=TOK_ROWS), :]


def _store_token_major(ref, row0, x):
    n_tok = x.shape[0]
    for j in range(TOK_ROWS):
        ref[pl.ds(row0 + j, n_tok, stride=TOK_ROWS), :] = x[:, j * V7X_LANES:(j + 1) * V7X_LANES]


_C_Q = (0, Q_LORA)
_C_KV = (Q_LORA, Q_LORA + KV_LORA)
_C_KPE = (_C_KV[1], _C_KV[1] + 2 * QK_ROPE)
_C_A = (_C_KPE[1], _C_KPE[1] + CONV_CH)
_C_G = (_C_A[1], _C_A[1] + CONV_CH)
IN_AUG = _C_G[1]


def _in_proj_kernel(x_ref, lng_ref, lnb_ref, w1_ref, qg_ref, kvg_ref, wq_ref, wuk_ref, wuv_ref, cs_ref,
                    q_ref, k_ref, v_ref, glu_ref):
    h0 = _layer_norm(x_ref[...], lng_ref[...], lnb_ref[...])
    hb = h0.astype(BF16)
    cs = cs_ref[...]

    def rope(t128):
        t = t128 * cs
        return t + pltpu.roll(t, QK_ROPE, axis=1)

    cq = _dot(hb, w1_ref[:, _C_Q[0]:_C_Q[1]])
    cqn = _rms_norm(cq, qg_ref[...]).astype(BF16)
    ckv = _dot(hb, w1_ref[:, _C_KV[0]:_C_KV[1]])
    ckvn = _rms_norm(ckv, kvg_ref[...]).astype(BF16)
    kpe = rope(_dot(hb, w1_ref[:, _C_KPE[0]:_C_KPE[1]]))

    a = _dot(hb, w1_ref[:, _C_A[0]:_C_A[1]])
    g = _dot(hb, w1_ref[:, _C_G[0]:_C_G[1]])
    glu_ref[...] = a * jax.nn.sigmoid(g)

    knope = _dot(ckvn, wuk_ref[...])
    v = _dot(ckvn, wuv_ref[...])
    kpe_t = kpe.T[0:QK_ROPE, :].astype(BF16)
    for h in range(N_HEADS):
        k_ref[h, 0:QK_NOPE, :] = knope[:, h * QK_NOPE:(h + 1) * QK_NOPE].T.astype(BF16)
        k_ref[h, QK_NOPE:QK_DIM, :] = kpe_t
        v_ref[h] = v[:, h * V_DIM:(h + 1) * V_DIM].astype(BF16)

    qn = _dot(cqn, wq_ref[:, 0:N_HEADS * QK_NOPE])
    qp = _dot(cqn, wq_ref[:, N_HEADS * QK_NOPE:])
    for h in range(N_HEADS):
        q_ref[h, :, 0:QK_NOPE] = qn[:, h * QK_NOPE:(h + 1) * QK_NOPE].astype(BF16)
        q_ref[h, :, QK_NOPE:QK_DIM] = rope(qp[:, h * V7X_LANES:(h + 1) * V7X_LANES])[:, :QK_ROPE].astype(BF16)


def _in_proj(x2d, lng, lnb, w1, qg, kvg, wq, wuk, wuv, cs, tm):
    rows = x2d.shape[0]
    n_cs = cs.shape[0] // tm
    row = lambda i: (i, 0)
    head_row = lambda i: (0, i, 0)
    return pl.pallas_call(
        _in_proj_kernel,
        grid=(rows // tm,),
        in_specs=[
            pl.BlockSpec((tm, D_MODEL), row),
            _const_spec((1, D_MODEL)), _const_spec((1, D_MODEL)),
            _const_spec((D_MODEL, IN_AUG)),
            _const_spec((1, Q_LORA)), _const_spec((1, KV_LORA)),
            _const_spec((Q_LORA, 2 * N_HEADS * QK_NOPE)),
            _const_spec((KV_LORA, N_HEADS * QK_NOPE)), _const_spec((KV_LORA, MLA_WIDTH)),
            pl.BlockSpec((tm, V7X_LANES), lambda i: (i % n_cs, 0)),
        ],
        out_specs=[
            pl.BlockSpec((N_HEADS, tm, QK_DIM), head_row),
            pl.BlockSpec((N_HEADS, QK_DIM, tm), lambda i: (0, 0, i)),
            pl.BlockSpec((N_HEADS, tm, V_DIM), head_row),
            pl.BlockSpec((tm, CONV_CH), row),
        ],
        out_shape=[
            jax.ShapeDtypeStruct((N_HEADS, rows, QK_DIM), BF16),
            jax.ShapeDtypeStruct((N_HEADS, QK_DIM, rows), BF16),
            jax.ShapeDtypeStruct((N_HEADS, rows, V_DIM), BF16),
            jax.ShapeDtypeStruct((rows, CONV_CH), F32),
        ],
        compiler_params=pltpu.CompilerParams(
            dimension_semantics=("arbitrary",), vmem_limit_bytes=V7X_VMEM_LIMIT),
        name="in_proj",
    )(x2d, lng, lnb, w1, qg, kvg, wq, wuk, wuv, cs)


def _attention_kernel(q_ref, k_ref, v_ref, km_ref, vm_ref, o_ref, *state):
    i = pl.program_id(2)
    c_exp = (1.0 / math.sqrt(QK_DIM)) * math.log2(math.e)
    chain_rows = ATT_TILE // ATT_CHAINS
    chains = range(ATT_CHAINS)
    m_refs, l_refs, acc_refs = (state[n * ATT_CHAINS:(n + 1) * ATT_CHAINS] for n in range(3))

    def lane_tiles(x):
        return [x[:, t * V7X_LANES:(t + 1) * V7X_LANES] for t in range(x.shape[1] // V7X_LANES)]

    def row_max(x):
        if x.shape[1] % V7X_LANES:
            return jnp.max(x, axis=1, keepdims=True)
        return jnp.max(functools.reduce(jnp.maximum, lane_tiles(x)), axis=1, keepdims=True)

    def lane_partial_sum(x):
        if x.shape[1] % V7X_LANES:
            lane = lax.broadcasted_iota(jnp.int32, (x.shape[0], V7X_LANES), 1)
            return jnp.where(lane == 0, jnp.sum(x, axis=1, keepdims=True), 0.0)
        return functools.reduce(jnp.add, lane_tiles(x))

    def update(h, s, vb, first):
        s_max = jnp.broadcast_to(row_max(s), (s.shape[0], V7X_LANES))
        if first:
            m_new = s_max
        else:
            m_old = m_refs[h][...]
            m_new = jnp.maximum(m_old, s_max)
            alpha = jnp.exp2(c_exp * (m_old - m_new))
        if s.shape[1] % V7X_LANES:
            p = jnp.exp2(c_exp * (s - m_new[:, :s.shape[1]]))
        else:
            p = jnp.concatenate([jnp.exp2(c_exp * (t - m_new)) for t in lane_tiles(s)], axis=1)
        p_sum = lane_partial_sum(p)
        pv = _dot(p.astype(BF16), vb)
        m_refs[h][...] = m_new
        if first:
            l_refs[h][...] = p_sum
            acc_refs[h][...] = pv
        else:
            l_refs[h][...] = alpha * l_refs[h][...] + p_sum
            acc_refs[h][...] = alpha * acc_refs[h][...] + pv

    def chain(x, h):
        return x[h * chain_rows:(h + 1) * chain_rows, :]

    q = q_ref[0]

    start = pl.multiple_of(i * ATT_TILE, ATT_TILE)
    s = _dot(q, jnp.concatenate([k_ref[0, :, pl.ds(start, ATT_TILE)], km_ref[0]], axis=1))
    r = lax.broadcasted_iota(jnp.int32, s.shape, 0)
    c = lax.broadcasted_iota(jnp.int32, s.shape, 1)
    last_visible = jnp.where(c >= ATT_TILE, ATT_TILE + N_META - 1, r)
    s = jnp.where(c <= last_visible, s, -1e30)
    for h in chains:
        cols = (h + 1) * chain_rows
        s_h = jnp.concatenate([chain(s, h)[:, :cols], chain(s, h)[:, ATT_TILE:]], axis=1)
        v_h = jnp.concatenate([v_ref[0, pl.ds(start, cols), :], vm_ref[0]], axis=0)
        update(h, s_h, v_h, True)

    def block(j, _):
        start = pl.multiple_of(j * ATT_TILE, ATT_TILE)
        s = _dot(q, k_ref[0, :, pl.ds(start, ATT_TILE)])
        vb = v_ref[0, pl.ds(start, ATT_TILE), :]
        for h in chains:
            update(h, chain(s, h), vb, False)
        return 0

    lax.fori_loop(0, i, block, 0)

    for h in chains:
        l = jnp.sum(l_refs[h][...], axis=1, keepdims=True)
        o_ref[pl.ds(h * chain_rows, chain_rows), :] = (acc_refs[h][...] / l).astype(BF16)


def _attention(q, k, v, km, vm, batch, seq):
    nq = seq // ATT_TILE
    return pl.pallas_call(
        _attention_kernel,
        grid=(batch, N_HEADS, nq),
        in_specs=[
            pl.BlockSpec((1, ATT_TILE, QK_DIM), lambda b, h, i: (h, b * nq + i, 0)),
            pl.BlockSpec((1, QK_DIM, seq), lambda b, h, i: (h, 0, b)),
            pl.BlockSpec((1, seq, V_DIM), lambda b, h, i: (h, b, 0)),
            pl.BlockSpec((1, QK_DIM, V7X_LANES), lambda b, h, i: (h, 0, 0)),
            pl.BlockSpec((1, V7X_LANES, V_DIM), lambda b, h, i: (h, 0, 0)),
        ],
        out_specs=pl.BlockSpec((ATT_TILE, V_DIM), lambda b, h, i: (b * nq + i, h)),
        out_shape=jax.ShapeDtypeStruct((batch * seq, MLA_WIDTH), BF16),
        scratch_shapes=[pltpu.VMEM((ATT_TILE // ATT_CHAINS, V7X_LANES), F32)] * (3 * ATT_CHAINS),
        compiler_params=pltpu.CompilerParams(
            dimension_semantics=("arbitrary", "arbitrary", "arbitrary"), vmem_limit_bytes=V7X_VMEM_LIMIT),
        name="attention",
    )(q, k, v, km, vm)


def _conv_kernel(cur_ref, prev_ref, meta_ref, w_ref, cb_ref, lng_ref, lnb_ref, o_ref, win_ref, acc_ref):
    i = pl.program_id(1)

    @pl.when(i == 0)
    def _():
        win_ref[0:CONV_HALO - N_META, :] = jnp.zeros((CONV_HALO - N_META, CONV_CH), F32)
        win_ref[CONV_HALO - N_META:CONV_HALO, :] = meta_ref[...]

    @pl.when(i > 0)
    def _():
        win_ref[0:CONV_HALO, :] = prev_ref[...]

    win_ref[CONV_HALO:, :] = cur_ref[...]

    base = CONV_HALO - (CONV_W - 1)

    for rc in range(CONV_TILE // CONV_ROWS):
        r0 = rc * CONV_ROWS
        for c in range(CONV_CH // CONV_LANES):
            lanes = pl.ds(c * CONV_LANES, CONV_LANES)
            acc = jnp.zeros((CONV_ROWS, CONV_LANES), F32)
            for k in range(CONV_W):
                acc = acc + win_ref[pl.ds(r0 + base + k, CONV_ROWS), lanes] * w_ref[k:k + 1, lanes]
            acc_ref[pl.ds(r0, CONV_ROWS), lanes] = acc

    y = _layer_norm(acc_ref[...] + cb_ref[...], lng_ref[...], lnb_ref[...])
    o_ref[...] = (y * jax.nn.sigmoid(y)).astype(BF16)


def _conv(glu, glu_meta, w, cb, lng, lnb, batch, seq):
    nt = seq // CONV_TILE
    per = CONV_TILE // CONV_HALO
    return pl.pallas_call(
        _conv_kernel,
        grid=(batch, nt),
        in_specs=[
            pl.BlockSpec((CONV_TILE, CONV_CH), lambda b, i: (b * nt + i, 0)),
            pl.BlockSpec((CONV_HALO, CONV_CH), lambda b, i: (jnp.maximum((b * nt + i) * per - 1, 0), 0)),
            _const_spec((N_META, CONV_CH)),
            _const_spec((CONV_HALO, CONV_CH)),
            _const_spec((1, CONV_CH)), _const_spec((1, CONV_CH)), _const_spec((1, CONV_CH)),
        ],
        out_specs=pl.BlockSpec((CONV_TILE, CONV_CH), lambda b, i: (b * nt + i, 0)),
        out_shape=jax.ShapeDtypeStruct((batch * seq, CONV_CH), BF16),
        scratch_shapes=[pltpu.VMEM((CONV_TILE + CONV_HALO, CONV_CH), F32),
                        pltpu.VMEM((CONV_TILE, CONV_CH), F32)],
        compiler_params=pltpu.CompilerParams(
            dimension_semantics=("arbitrary", "arbitrary"), vmem_limit_bytes=V7X_VMEM_LIMIT),
        name="conv",
    )(glu, glu, glu_meta, w, cb, lng, lnb)


def _out_proj_kernel(attn_ref, conv_ref, x_ref, lng_ref, lnb_ref, wo_ref, g1_ref, b1_ref,
                     wrh_ref, wrl_ref, br_ref,
                     h1_ref, idx_ref, rank_ref, gate_ref, cnt_ref, carry_ref):
    step = pl.program_id(0)
    tm = x_ref.shape[0]

    @pl.when(step == 0)
    def _():
        carry_ref[...] = jnp.zeros_like(carry_ref)

    h0 = _layer_norm(x_ref[...], lng_ref[...], lnb_ref[...])
    mix = _dot(attn_ref[...], wo_ref[0:MLA_WIDTH, :]) + _dot(conv_ref[...], wo_ref[MLA_WIDTH:, :])
    h1 = _layer_norm(DEEPNORM_ALPHA * h0 + mix, g1_ref[...], b1_ref[...])
    _store_token_major(h1_ref, 0, h1)

    hi = h1.astype(BF16)
    lo = (h1 - hi.astype(F32)).astype(BF16)
    logits = (_dot(hi, wrh_ref[...]) + (_dot(hi, wrl_ref[...]) + _dot(lo, wrh_ref[...]))) + br_ref[...]

    lane = lax.broadcasted_iota(jnp.int32, (tm, N_EXPERTS), 1)
    work = logits
    vals, idxs = [], []
    for _ in range(TOP_K):
        mx = jnp.max(work, axis=1, keepdims=True)
        ix = jnp.min(jnp.where(work == mx, lane, N_EXPERTS), axis=1, keepdims=True)
        vals.append(mx)
        idxs.append(ix)
        work = jnp.where(lane == ix, -jnp.inf, work)
    exps = [jnp.exp(v - vals[0]) for v in vals]
    denom = exps[0] + exps[1] + exps[2] + exps[3]

    onehots = [(lane == ix) for ix in idxs]
    chosen = (onehots[0] | onehots[1] | onehots[2] | onehots[3])
    chosen_f = jnp.where(chosen, 1.0, 0.0)
    r = lax.broadcasted_iota(jnp.int32, (tm, tm), 0)
    c = lax.broadcasted_iota(jnp.int32, (tm, tm), 1)
    lower = jnp.where(c < r, 1.0, 0.0).astype(BF16)
    before = _dot(lower, chosen_f.astype(BF16)) + carry_ref[...]

    out_lane = lax.broadcasted_iota(jnp.int32, (tm, V7X_LANES), 1)
    idx_out = jnp.zeros((tm, V7X_LANES), jnp.int32)
    rank_out = jnp.zeros((tm, V7X_LANES), jnp.int32)
    gate_out = jnp.zeros((tm, V7X_LANES), F32)
    for k in range(TOP_K):
        rank_k = jnp.sum(jnp.where(onehots[k], before, 0.0), axis=1, keepdims=True).astype(jnp.int32)
        idx_out = jnp.where(out_lane == k, idxs[k], idx_out)
        rank_out = jnp.where(out_lane == k, rank_k, rank_out)
        gate_out = jnp.where(out_lane == k, exps[k] / denom, gate_out)
    idx_ref[...] = idx_out
    rank_ref[...] = rank_out
    gate_ref[...] = gate_out

    carry_ref[...] = carry_ref[...] + jnp.sum(chosen_f, axis=0, keepdims=True)
    cnt_ref[...] = carry_ref[...].astype(jnp.int32)


def _out_proj(attn, conv, x2d, lng, lnb, wo, g1, b1, wrh, wrl, br):
    rows = x2d.shape[0]
    tm = ROW_TILE
    row = lambda i: (i, 0)
    return pl.pallas_call(
        _out_proj_kernel,
        grid=(rows // tm,),
        in_specs=[
            pl.BlockSpec((tm, MLA_WIDTH), row), pl.BlockSpec((tm, CONV_CH), row),
            pl.BlockSpec((tm, D_MODEL), row),
            _const_spec((1, D_MODEL)), _const_spec((1, D_MODEL)),
            _const_spec((D_MODEL, D_MODEL)),
            _const_spec((1, D_MODEL)), _const_spec((1, D_MODEL)),
            _const_spec((D_MODEL, N_EXPERTS)), _const_spec((D_MODEL, N_EXPERTS)),
            _const_spec((1, N_EXPERTS)),
        ],
        out_specs=[
            pl.BlockSpec((tm * TOK_ROWS, V7X_LANES), row),
            pl.BlockSpec((tm, V7X_LANES), row), pl.BlockSpec((tm, V7X_LANES), row),
            pl.BlockSpec((tm, V7X_LANES), row),
            _const_spec((1, N_EXPERTS)),
        ],
        out_shape=[
            jax.ShapeDtypeStruct((rows * TOK_ROWS, V7X_LANES), F32),
            jax.ShapeDtypeStruct((rows, V7X_LANES), jnp.int32),
            jax.ShapeDtypeStruct((rows, V7X_LANES), jnp.int32),
            jax.ShapeDtypeStruct((rows, V7X_LANES), F32),
            jax.ShapeDtypeStruct((1, N_EXPERTS), jnp.int32),
        ],
        scratch_shapes=[pltpu.VMEM((1, N_EXPERTS), F32)],
        compiler_params=pltpu.CompilerParams(
            dimension_semantics=("arbitrary",), vmem_limit_bytes=V7X_VMEM_LIMIT),
        name="out_proj_router",
    )(attn, conv, x2d, lng, lnb, wo, g1, b1, wrh, wrl, br)


def _token_copy(src_ref, src_tok, dst_ref, dst_tok, sem, n_tok=1):
    rows = n_tok * TOK_ROWS
    src = src_ref.at[pl.ds(pl.multiple_of(src_tok * TOK_ROWS, TOK_ROWS), rows), :]
    dst = dst_ref.at[pl.ds(pl.multiple_of(dst_tok * TOK_ROWS, TOK_ROWS), rows), :]
    return pltpu.make_async_copy(src, dst, sem)


def _dispatch_kernel(dest_ref, padrow_ref, npad_ref, h1_ref, xs_ref, zero_ref, sem, zsem):
    step = pl.program_id(0)
    base = step * (DISPATCH_TILE * TOP_K)

    def copies(t):
        return [_token_copy(h1_ref, t, xs_ref, dest_ref[base + t * TOP_K + k], sem) for k in range(TOP_K)]

    def start(t, _):
        for c in copies(t):
            c.start()
        return 0

    def wait(t, _):
        for c in copies(t):
            c.wait()
        return 0

    lax.fori_loop(0, DISPATCH_TILE, start, 0)

    @pl.when(step == 0)
    def _():
        zero_ref[...] = jnp.zeros_like(zero_ref)
        n = npad_ref[0]
        tail_start = npad_ref[1]
        n_tail = (xs_ref.shape[0] // TOK_ROWS - tail_start) // SEG_ALIGN

        def zstart(p, _):
            _token_copy(zero_ref, 0, xs_ref, padrow_ref[p], zsem).start()
            return 0

        def zwait(p, _):
            _token_copy(zero_ref, 0, xs_ref, padrow_ref[p], zsem).wait()
            return 0

        def tstart(b, _):
            _token_copy(zero_ref, 0, xs_ref, tail_start + b * SEG_ALIGN, zsem, SEG_ALIGN).start()
            return 0

        def twait(b, _):
            _token_copy(zero_ref, 0, xs_ref, tail_start + b * SEG_ALIGN, zsem, SEG_ALIGN).wait()
            return 0

        lax.fori_loop(0, n, zstart, 0)
        lax.fori_loop(0, n, zwait, 0)
        lax.fori_loop(0, n_tail, tstart, 0)
        lax.fori_loop(0, n_tail, twait, 0)

    lax.fori_loop(0, DISPATCH_TILE, wait, 0)


def _dispatch(dest_flat, pad_rows, n_pad, h1t, n_rows):
    tokens = h1t.shape[0] // TOK_ROWS
    return pl.pallas_call(
        _dispatch_kernel,
        grid_spec=pltpu.PrefetchScalarGridSpec(
            num_scalar_prefetch=3,
            grid=(tokens // DISPATCH_TILE,),
            in_specs=[pl.BlockSpec((DISPATCH_TILE * TOK_ROWS, V7X_LANES), lambda i, *_: (i, 0))],
            out_specs=pl.BlockSpec(memory_space=pl.ANY),
            scratch_shapes=[pltpu.VMEM((SEG_ALIGN * TOK_ROWS, V7X_LANES), F32),
                            pltpu.SemaphoreType.DMA, pltpu.SemaphoreType.DMA],
        ),
        out_shape=jax.ShapeDtypeStruct((n_rows * TOK_ROWS, V7X_LANES), F32),
        compiler_params=pltpu.CompilerParams(
            dimension_semantics=("arbitrary",), has_side_effects=True),
        name="dispatch",
    )(dest_flat, pad_rows, n_pad, h1t)


N_FF = D_FF // FF_TILE
SUB = SEG_ALIGN
N_SUB = SUPER_ROWS // SUB
PREFETCH_SLOTS = -(-N_SUB // N_FF)
OUT_SLOTS = MM_ROWS // SUB


def _experts_kernel(st_e_ref, st_start_ref, st_rows_ref, n_used_ref,
                    xs_ref, w1g_ref, w1u_ref, b1g_ref, b1u_ref, w2_ref, b2_ref,
                    ys_ref,
                    xb_ref, acc_ref, wg_ref, wu_ref, wd_ref, xstage_ref, ostage_ref, xsem, osem):
    s = pl.program_id(0)
    j = pl.program_id(1)
    parity = s % 2
    rows = st_rows_ref[s]
    start = st_start_ref[s]
    n_blk = rows // SUB
    next_start = st_start_ref[s + 1]
    next_blk = st_rows_ref[s + 1] // SUB

    def x_copy(tok0, slot):
        return _token_copy(xs_ref, tok0, xstage_ref.at[slot], 0, xsem.at[slot], SUB)

    def y_copy(slot, tok0):
        return _token_copy(ostage_ref.at[slot], 0, ys_ref, tok0, osem.at[slot], SUB)

    def convert(slot, par, blk):
        off = pl.multiple_of(blk * SUB, SUB)
        for c in range(TOK_ROWS):
            xb_ref[par, pl.ds(off, SUB), c * V7X_LANES:(c + 1) * V7X_LANES] = (
                _load_token_major(xstage_ref.at[slot], 0, SUB, c).astype(BF16))

    @pl.when((s == 0) & (j == 0))
    def _():
        ostage_ref[0] = jnp.zeros(ostage_ref.shape[1:], F32)
        tail_start = n_used_ref[1]
        n_tail = (ys_ref.shape[0] // TOK_ROWS - tail_start) // SUB

        def tstart(b, _):
            y_copy(0, tail_start + b * SUB).start()
            return 0

        def twait(b, _):
            y_copy(0, tail_start + b * SUB).wait()
            return 0

        lax.fori_loop(0, n_tail, tstart, 0)
        lax.fori_loop(0, n_tail, twait, 0)

        def first(b, _):
            x_copy(start + b * SUB, 0).start()
            x_copy(start + b * SUB, 0).wait()
            convert(0, 0, b)
            return 0

        lax.fori_loop(0, n_blk, first, 0)

    for p in range(PREFETCH_SLOTS):
        @pl.when(j * PREFETCH_SLOTS + p < next_blk)
        def _(p=p):
            x_copy(next_start + (j * PREFETCH_SLOTS + p) * SUB, p).start()

    def compute(last):
        wg_ref[...] = w1g_ref[0].astype(BF16)
        wu_ref[...] = w1u_ref[0].astype(BF16)
        wd_ref[...] = w2_ref[0].astype(BF16)
        bg = b1g_ref[0]
        bu = b1u_ref[0]
        b2 = b2_ref[0]

        def chunk(row0, m):
            off = pl.multiple_of(row0, SUB)
            xb = xb_ref[parity, pl.ds(off, m), :]
            g = _dot(xb, wg_ref[...]) + bg
            u = _dot(xb, wu_ref[...]) + bu
            g = jnp.minimum(g, SWIGLU_LIMIT)
            u = jnp.clip(u, -SWIGLU_LIMIT, SWIGLU_LIMIT)
            act = g * jax.nn.sigmoid(SWIGLU_ALPHA * g) * (u + 1.0)
            y = _dot(act.astype(BF16), wd_ref[...])
            if not last:
                acc_ref[pl.ds(off, m), :] += y
                return
            y = acc_ref[pl.ds(off, m), :] + y + b2
            for i in range(m // SUB):
                blk = row0 // SUB + i
                slot = blk % OUT_SLOTS

                @pl.when(blk >= OUT_SLOTS)
                def _(blk=blk, slot=slot):
                    y_copy(slot, start + (blk - OUT_SLOTS) * SUB).wait()

                _store_token_major(ostage_ref.at[slot], 0, y[i * SUB:(i + 1) * SUB, :])
                y_copy(slot, start + blk * SUB).start()

        n_big = rows // MM_ROWS

        def big(r, _):
            chunk(r * MM_ROWS, MM_ROWS)
            return 0

        lax.fori_loop(0, n_big, big, 0)
        done = n_big * MM_ROWS
        m = MM_ROWS // 2
        while m >= SUB:
            take = ((rows - done) & m) != 0

            @pl.when(take)
            def _(done=done, m=m):
                chunk(done, m)

            done = done + jnp.where(take, m, 0)
            m //= 2

        if last:
            for i in range(OUT_SLOTS):
                @pl.when(i < n_blk)
                def _(i=i):
                    blk = n_blk - 1 - i
                    y_copy(blk % OUT_SLOTS, start + blk * SUB).wait()

    @pl.when((j == 0) & (rows > 0))
    def _():
        def zero(b, _):
            acc_ref[pl.ds(pl.multiple_of(b * SUB, SUB), SUB), :] = jnp.zeros((SUB, D_MODEL), F32)
            return 0

        lax.fori_loop(0, n_blk, zero, 0)

    @pl.when((j < N_FF - 1) & (rows > 0))
    def _():
        compute(False)

    @pl.when((j == N_FF - 1) & (rows > 0))
    def _():
        compute(True)

    for p in range(PREFETCH_SLOTS):
        @pl.when(j * PREFETCH_SLOTS + p < next_blk)
        def _(p=p):
            blk = j * PREFETCH_SLOTS + p
            x_copy(next_start + blk * SUB, p).wait()
            convert(p, 1 - parity, blk)


def _experts(st_e, st_start, st_rows, n_used, xs, w1, b1, w2, b2, n_super):
    n_rows = xs.shape[0] // TOK_ROWS

    def ff(s, j, n_used_ref):
        return jnp.where(s < n_used_ref[0], j, N_FF - 1)

    w1g_map = lambda s, j, e, st, rw, nu: (e[s], 0, ff(s, j, nu))
    w1u_map = lambda s, j, e, st, rw, nu: (e[s], 0, N_FF + ff(s, j, nu))
    w2_map = lambda s, j, e, st, rw, nu: (e[s], ff(s, j, nu), 0)
    b2_map = lambda s, j, e, st, rw, nu: (e[s], 0, 0)
    return pl.pallas_call(
        _experts_kernel,
        grid_spec=pltpu.PrefetchScalarGridSpec(
            num_scalar_prefetch=4,
            grid=(n_super, N_FF),
            in_specs=[
                pl.BlockSpec(memory_space=pl.ANY),
                pl.BlockSpec((1, D_MODEL, FF_TILE), w1g_map),
                pl.BlockSpec((1, D_MODEL, FF_TILE), w1u_map),
                pl.BlockSpec((1, 1, FF_TILE), w1g_map),
                pl.BlockSpec((1, 1, FF_TILE), w1u_map),
                pl.BlockSpec((1, FF_TILE, D_MODEL), w2_map),
                pl.BlockSpec((1, 1, D_MODEL), b2_map),
            ],
            out_specs=pl.BlockSpec(memory_space=pl.ANY),
            scratch_shapes=[
                pltpu.VMEM((2, SUPER_ROWS, D_MODEL), BF16),
                pltpu.VMEM((SUPER_ROWS, D_MODEL), F32),
                pltpu.VMEM((D_MODEL, FF_TILE), BF16),
                pltpu.VMEM((D_MODEL, FF_TILE), BF16),
                pltpu.VMEM((FF_TILE, D_MODEL), BF16),
                pltpu.VMEM((PREFETCH_SLOTS, SUB * TOK_ROWS, V7X_LANES), F32),
                pltpu.VMEM((OUT_SLOTS, SUB * TOK_ROWS, V7X_LANES), F32),
                pltpu.SemaphoreType.DMA((PREFETCH_SLOTS,)),
                pltpu.SemaphoreType.DMA((OUT_SLOTS,)),
            ],
        ),
        out_shape=jax.ShapeDtypeStruct((n_rows * TOK_ROWS, V7X_LANES), F32),
        compiler_params=pltpu.CompilerParams(
            dimension_semantics=("arbitrary", "arbitrary"), vmem_limit_bytes=V7X_VMEM_LIMIT,
            has_side_effects=True),
        name="experts",
    )(st_e, st_start, st_rows, n_used, xs, w1, w1, b1, b1, w2, b2)


def _combine_kernel(dest_ref, ys_ref, gate_ref, h1_ref, g2_ref, b2_ref, o_ref, buf_ref, sem):
    step = pl.program_id(0)
    n_steps = pl.num_programs(0)
    tm = COMBINE_TILE

    def copies(at_step, slot, t):
        base = at_step * (tm * TOP_K)
        return [_token_copy(ys_ref, dest_ref[base + t * TOP_K + k], buf_ref.at[slot, k], t, sem.at[slot])
                for k in range(TOP_K)]

    def gather(at_step, slot):
        def start(t, _):
            for c in copies(at_step, slot, t):
                c.start()
            return 0

        lax.fori_loop(0, tm, start, 0)

    @pl.when(step == 0)
    def _():
        gather(0, 0)

    @pl.when(step + 1 < n_steps)
    def _():
        gather(step + 1, (step + 1) % 2)

    slot = step % 2

    def wait(t, _):
        for c in copies(step, slot, t):
            c.wait()
        return 0

    lax.fori_loop(0, tm, wait, 0)

    gates = gate_ref[...]
    for c in range(TOK_ROWS):
        z = DEEPNORM_ALPHA * _load_token_major(h1_ref, 0, tm, c)
        for k in range(TOP_K):
            z = z + _load_token_major(buf_ref.at[slot, k], 0, tm, c) * gates[:, k:k + 1]
        o_ref[:, c * V7X_LANES:(c + 1) * V7X_LANES] = z
    o_ref[...] = _layer_norm(o_ref[...], g2_ref[...], b2_ref[...])


def _combine(dest_flat, ys, gates, h1t, g2, b2):
    tokens = h1t.shape[0] // TOK_ROWS
    tm = COMBINE_TILE
    row = lambda i, *_: (i, 0)
    return pl.pallas_call(
        _combine_kernel,
        grid_spec=pltpu.PrefetchScalarGridSpec(
            num_scalar_prefetch=1,
            grid=(tokens // tm,),
            in_specs=[
                pl.BlockSpec(memory_space=pl.ANY),
                pl.BlockSpec((tm, V7X_LANES), row),
                pl.BlockSpec((tm * TOK_ROWS, V7X_LANES), row),
                pl.BlockSpec((1, D_MODEL), lambda i, *_: (0, 0)),
                pl.BlockSpec((1, D_MODEL), lambda i, *_: (0, 0)),
            ],
            out_specs=pl.BlockSpec((tm, D_MODEL), row),
            scratch_shapes=[pltpu.VMEM((2, TOP_K, tm * TOK_ROWS, V7X_LANES), F32),
                            pltpu.SemaphoreType.DMA((2,))],
        ),
        out_shape=jax.ShapeDtypeStruct((tokens, D_MODEL), F32),
        compiler_params=pltpu.CompilerParams(
            dimension_semantics=("arbitrary",), vmem_limit_bytes=V7X_VMEM_LIMIT),
        name="combine",
    )(dest_flat, ys, gates, h1t, g2, b2)


def _rotate_half_cols(w):
    half = QK_ROPE // 2
    return jnp.concatenate([-w[..., half:], w[..., :half]], axis=-1)


def _rope_table(length):
    inv_freq = 1.0 / (ROPE_THETA ** (jnp.arange(0, QK_ROPE, 2, dtype=F32) / QK_ROPE))
    freqs = jnp.arange(length, dtype=F32)[:, None] * inv_freq[None, :]
    emb = jnp.concatenate([freqs, freqs], axis=-1)
    return jnp.concatenate([jnp.cos(emb), jnp.sin(emb)], axis=-1)


def _routing_plan(idx, rank, counts, n_super):
    experts = jnp.arange(N_EXPERTS, dtype=jnp.int32)

    def lookup(table, i):
        return jnp.sum(jnp.where(i[..., None] == experts, table, 0), axis=-1)

    def bucket(cum, i):
        return jnp.minimum(jnp.sum((cum <= i[..., None]).astype(jnp.int32), axis=-1), N_EXPERTS - 1)

    counts = counts.astype(jnp.int32)
    padded = (counts + SEG_ALIGN - 1) // SEG_ALIGN * SEG_ALIGN
    pad_end = jnp.cumsum(padded)
    pad_start = pad_end - padded
    dest = (lookup(pad_start, idx) + rank).reshape(-1).astype(jnp.int32)

    n_padmax = N_EXPERTS * SEG_ALIGN
    padcnt = padded - counts
    padcum = jnp.cumsum(padcnt)
    p = jnp.arange(n_padmax, dtype=jnp.int32)
    pe = bucket(padcum, p)
    pad_rows = lookup(pad_start + counts - (padcum - padcnt), pe) + p
    n_pad = jnp.stack([padcum[-1], pad_end[-1]]).astype(jnp.int32)
    pad_rows = jnp.where(p < n_pad[0], pad_rows, 0).astype(jnp.int32)

    n_st = (padded + SUPER_ROWS - 1) // SUPER_ROWS
    st_cum = jnp.cumsum(n_st)
    n_used = jnp.stack([st_cum[-1], pad_end[-1]]).astype(jnp.int32)
    s = jnp.arange(n_super + 1, dtype=jnp.int32)
    s_eff = jnp.minimum(s, n_used[0] - 1)
    se = bucket(st_cum, s_eff).astype(jnp.int32)
    local = s_eff - lookup(st_cum - n_st, se)
    st_start = (lookup(pad_start, se) + local * SUPER_ROWS).astype(jnp.int32)
    st_rows = jnp.clip(lookup(padded, se) - local * SUPER_ROWS, 0, SUPER_ROWS)
    st_rows = jnp.where(s < n_used[0], st_rows, 0).astype(jnp.int32)
    return dest, pad_rows, n_pad, se, st_start, st_rows, n_used


def kernel(x, meta_tokens, ln_in_g, ln_in_b, w_in, q_norm_g, w_uq, kv_norm_g, w_uk, w_uv, conv_dw_w,
           conv_dw_b, conv_ln_g, conv_ln_b, w_out, ln1_g, ln1_b, w_router, b_router, w_mlp1, b_mlp1,
           w_mlp2, b_mlp2, ln2_g, ln2_b):
    batch, seq, _ = x.shape
    tokens = batch * seq
    row2 = lambda a: a.reshape(1, -1)

    wi = w_in[0]
    s_kpe = Q_LORA + KV_LORA
    kpe_w = wi[:, s_kpe:s_kpe + QK_ROPE]
    w1 = jnp.concatenate(
        [wi[:, :s_kpe], kpe_w, _rotate_half_cols(kpe_w), wi[:, s_kpe + QK_ROPE:]], axis=1).astype(BF16)
    wq3 = w_uq[0].reshape(Q_LORA, N_HEADS, QK_DIM)
    wq_nope = wq3[:, :, :QK_NOPE].reshape(Q_LORA, N_HEADS * QK_NOPE)
    wq_pe = wq3[:, :, QK_NOPE:]
    wq_pr = jnp.concatenate([wq_pe, _rotate_half_cols(wq_pe)], axis=-1).reshape(Q_LORA, N_HEADS * 2 * QK_ROPE)
    wq = jnp.concatenate([wq_nope, wq_pr], axis=1).astype(BF16)
    wuk = w_uk[0].astype(BF16)
    wuv = w_uv[0].astype(BF16)
    wo = w_out[0].astype(BF16)
    wr = w_router[0]
    wr_hi = wr.astype(BF16)
    wr_lo = (wr - wr_hi.astype(F32)).astype(BF16)
    cs = _rope_table(N_META + seq)
    conv_w = jnp.concatenate([conv_dw_w[0], jnp.zeros((CONV_HALO - CONV_W, CONV_CH), F32)], axis=0)

    x2d = x.reshape(tokens, D_MODEL)
    proj_args = (row2(ln_in_g), row2(ln_in_b), w1, row2(q_norm_g[0]), row2(kv_norm_g[0]), wq, wuk, wuv)

    _, k_meta, v_meta, glu_meta = _in_proj(meta_tokens, *proj_args, cs[:N_META], N_META)
    q, k, v, glu = _in_proj(x2d, *proj_args, cs[N_META:], ROW_TILE)
    k_meta = jnp.pad(k_meta, ((0, 0), (0, 0), (0, V7X_LANES - N_META)))
    v_meta = jnp.pad(v_meta, ((0, 0), (0, V7X_LANES - N_META), (0, 0)))
    attn = _attention(q, k, v, k_meta, v_meta, batch, seq)
    conv = _conv(glu, glu_meta, conv_w, row2(conv_dw_b[0]), row2(conv_ln_g[0]), row2(conv_ln_b[0]), batch, seq)

    h1t, idx, rank, gates, counts = _out_proj(
        attn, conv, x2d, row2(ln_in_g), row2(ln_in_b), wo, row2(ln1_g[0]), row2(ln1_b[0]),
        wr_hi, wr_lo, row2(b_router[0]))

    n_assign = tokens * TOP_K
    n_rows = n_assign + N_EXPERTS * SEG_ALIGN
    n_super = N_EXPERTS + -(-n_assign // SUPER_ROWS)
    dest, pad_rows, n_pad, st_e, st_start, st_rows, n_used = _routing_plan(
        idx[:, :TOP_K], rank[:, :TOP_K], counts[0], n_super)

    xs = _dispatch(dest, pad_rows, n_pad, h1t, n_rows)
    ys = _experts(st_e, st_start, st_rows, n_used, xs, w_mlp1[0], b_mlp1[0].reshape(N_EXPERTS, 1, 2 * D_FF),
                  w_mlp2[0], b_mlp2[0].reshape(N_EXPERTS, 1, D_MODEL), n_super)
    out = _combine(dest, ys, gates, h1t, row2(ln2_g[0]), row2(ln2_b[0]))
    return out.reshape(batch, seq, D_MODEL)
```

```python
import functools
import math

import jax
import jax.numpy as jnp
from jax import lax
from jax.experimental import pallas as pl
from jax.experimental.pallas import tpu as pltpu

D_MODEL = 2048
N_META = 16
N_HEADS = 8
QK_NOPE = 128
QK_ROPE = 64
QK_DIM = QK_NOPE + QK_ROPE
V_DIM = 128
Q_LORA = 768
KV_LORA = 512
ROPE_THETA = 10000.0
MLA_WIDTH = N_HEADS * V_DIM
CONV_CH = 1024
CONV_W = 31
N_EXPERTS = 32
TOP_K = 4
D_FF = 2048
SWIGLU_LIMIT = 7.0
SWIGLU_ALPHA = 1.702
DEEPNORM_ALPHA = 2.0 ** 0.25
LN_EPS = 1e-5
RMS_EPS = 1e-6

V7X_LANES = 128
V7X_SUBLANES = 8
V7X_VMEM_LIMIT = 56 * 1024 * 1024

ROW_TILE = 256
ATT_TILE = 512
ATT_CHAINS = 2
CONV_TILE = 256
CONV_HALO = 32
CONV_ROWS = 32
CONV_LANES = 256
DISPATCH_TILE = 256
COMBINE_TILE = 128
SEG_ALIGN = 128
SUPER_ROWS = 1536
FF_TILE = 256
MM_ROWS = 512

F32 = jnp.float32
BF16 = jnp.bfloat16


def _dot(a, b):
    return jnp.dot(a, b, preferred_element_type=F32)


def _dot_nt(a, b):
    return lax.dot_general(a, b, (((1,), (1,)), ((), ())), preferred_element_type=F32)


def _layer_norm(x, g, b):
    mu = jnp.mean(x, axis=-1, keepdims=True)
    xc = x - mu
    var = jnp.mean(xc * xc, axis=-1, keepdims=True)
    return xc * lax.rsqrt(var + LN_EPS) * g + b


def _rms_norm(x, g):
    ms = jnp.mean(x * x, axis=-1, keepdims=True)
    return x * lax.rsqrt(ms + RMS_EPS) * g


def _const_spec(shape):
    zeros = (0,) * len(shape)
    return pl.BlockSpec(shape, lambda *_: zeros)


TOK_ROWS = D_MODEL // V7X_LANES


def _load_token_major(ref, row0, n_tok, j):
    return ref[pl.ds(row0 + j, n_tok, stride=TOK_ROWS), :]


def _store_token_major(ref, row0, x):
    n_tok = x.shape[0]
    for j in range(TOK_ROWS):
        ref[pl.ds(row0 + j, n_tok, stride=TOK_ROWS), :] = x[:, j * V7X_LANES:(j + 1) * V7X_LANES]


_C_Q = (0, Q_LORA)
_C_KV = (Q_LORA, Q_LORA + KV_LORA)
_C_KPE = (_C_KV[1], _C_KV[1] + 2 * QK_ROPE)
_C_A = (_C_KPE[1], _C_KPE[1] + CONV_CH)
_C_G = (_C_A[1], _C_A[1] + CONV_CH)
IN_AUG = _C_G[1]


def _in_proj_kernel(x_ref, lng_ref, lnb_ref, w1_ref, qg_ref, kvg_ref, wq_ref, wuk_ref, wuv_ref, cs_ref,
                    q_ref, k_ref, v_ref, glu_ref):
    h0 = _layer_norm(x_ref[...], lng_ref[...], lnb_ref[...])
    hb = h0.astype(BF16)
    cs = cs_ref[...]

    def rope(t128):
        t = t128 * cs
        return t + pltpu.roll(t, QK_ROPE, axis=1)

    cq = _dot(hb, w1_ref[:, _C_Q[0]:_C_Q[1]])
    cqn = _rms_norm(cq, qg_ref[...]).astype(BF16)
    ckv = _dot(hb, w1_ref[:, _C_KV[0]:_C_KV[1]])
    ckvn = _rms_norm(ckv, kvg_ref[...]).astype(BF16)
    kpe = rope(_dot(hb, w1_ref[:, _C_KPE[0]:_C_KPE[1]]))

    a = _dot(hb, w1_ref[:, _C_A[0]:_C_A[1]])
    g = _dot(hb, w1_ref[:, _C_G[0]:_C_G[1]])
    glu_ref[...] = a * jax.nn.sigmoid(g)

    knope = _dot(ckvn, wuk_ref[...])
    v = _dot(ckvn, wuv_ref[...])
    kpe_t = kpe.T[0:QK_ROPE, :].astype(BF16)
    for h in range(N_HEADS):
        k_ref[h, 0:QK_NOPE, :] = knope[:, h * QK_NOPE:(h + 1) * QK_NOPE].T.astype(BF16)
        k_ref[h, QK_NOPE:QK_DIM, :] = kpe_t
        v_ref[h] = v[:, h * V_DIM:(h + 1) * V_DIM].astype(BF16)

    qn = _dot(cqn, wq_ref[:, 0:N_HEADS * QK_NOPE])
    qp = _dot(cqn, wq_ref[:, N_HEADS * QK_NOPE:])
    for h in range(N_HEADS):
        q_ref[h, :, 0:QK_NOPE] = qn[:, h * QK_NOPE:(h + 1) * QK_NOPE].astype(BF16)
        q_ref[h, :, QK_NOPE:QK_DIM] = rope(qp[:, h * V7X_LANES:(h + 1) * V7X_LANES])[:, :QK_ROPE].astype(BF16)


def _in_proj(x2d, lng, lnb, w1, qg, kvg, wq, wuk, wuv, cs, tm):
    rows = x2d.shape[0]
    n_cs = cs.shape[0] // tm
    row = lambda i: (i, 0)
    head_row = lambda i: (0, i, 0)
    return pl.pallas_call(
        _in_proj_kernel,
        grid=(rows // tm,),
        in_specs=[
            pl.BlockSpec((tm, D_MODEL), row),
            _const_spec((1, D_MODEL)), _const_spec((1, D_MODEL)),
            _const_spec((D_MODEL, IN_AUG)),
            _const_spec((1, Q_LORA)), _const_spec((1, KV_LORA)),
            _const_spec((Q_LORA, 2 * N_HEADS * QK_NOPE)),
            _const_spec((KV_LORA, N_HEADS * QK_NOPE)), _const_spec((KV_LORA, MLA_WIDTH)),
            pl.BlockSpec((tm, V7X_LANES), lambda i: (i % n_cs, 0)),
        ],
        out_specs=[
            pl.BlockSpec((N_HEADS, tm, QK_DIM), head_row),
            pl.BlockSpec((N_HEADS, QK_DIM, tm), lambda i: (0, 0, i)),
            pl.BlockSpec((N_HEADS, tm, V_DIM), head_row),
            pl.BlockSpec((tm, CONV_CH), row),
        ],
        out_shape=[
            jax.ShapeDtypeStruct((N_HEADS, rows, QK_DIM), BF16),
            jax.ShapeDtypeStruct((N_HEADS, QK_DIM, rows), BF16),
            jax.ShapeDtypeStruct((N_HEADS, rows, V_DIM), BF16),
            jax.ShapeDtypeStruct((rows, CONV_CH), F32),
        ],
        compiler_params=pltpu.CompilerParams(
            dimension_semantics=("arbitrary",), vmem_limit_bytes=V7X_VMEM_LIMIT),
        name="in_proj",
    )(x2d, lng, lnb, w1, qg, kvg, wq, wuk, wuv, cs)


def _attention_kernel(q_ref, k_ref, v_ref, km_ref, vm_ref, o_ref, *state):
    i = pl.program_id(2)
    c_exp = (1.0 / math.sqrt(QK_DIM)) * math.log2(math.e)
    chain_rows = ATT_TILE // ATT_CHAINS
    chains = range(ATT_CHAINS)
    m_refs, l_refs, acc_refs = (state[n * ATT_CHAINS:(n + 1) * ATT_CHAINS] for n in range(3))

    def lane_tiles(x):
        return [x[:, t * V7X_LANES:(t + 1) * V7X_LANES] for t in range(x.shape[1] // V7X_LANES)]

    def row_max(x):
        if x.shape[1] % V7X_LANES:
            return jnp.max(x, axis=1, keepdims=True)
        return jnp.max(functools.reduce(jnp.maximum, lane_tiles(x)), axis=1, keepdims=True)

    def lane_partial_sum(x):
        if x.shape[1] % V7X_LANES:
            lane = lax.broadcasted_iota(jnp.int32, (x.shape[0], V7X_LANES), 1)
            return jnp.where(lane == 0, jnp.sum(x, axis=1, keepdims=True), 0.0)
        return functools.reduce(jnp.add, lane_tiles(x))

    def update(h, s, vb, first):
        s_max = jnp.broadcast_to(row_max(s), (s.shape[0], V7X_LANES))
        if first:
            m_new = s_max
        else:
            m_old = m_refs[h][...]
            m_new = jnp.maximum(m_old, s_max)
            alpha = jnp.exp2(c_exp * (m_old - m_new))
        if s.shape[1] % V7X_LANES:
            p = jnp.exp2(c_exp * (s - m_new[:, :s.shape[1]]))
        else:
            p = jnp.concatenate([jnp.exp2(c_exp * (t - m_new)) for t in lane_tiles(s)], axis=1)
        p_sum = lane_partial_sum(p)
        pv = _dot(p.astype(BF16), vb)
        m_refs[h][...] = m_new
        if first:
            l_refs[h][...] = p_sum
            acc_refs[h][...] = pv
        else:
            l_refs[h][...] = alpha * l_refs[h][...] + p_sum
            acc_refs[h][...] = alpha * acc_refs[h][...] + pv

    def chain(x, h):
        return x[h * chain_rows:(h + 1) * chain_rows, :]

    q = q_ref[0]

    start = pl.multiple_of(i * ATT_TILE, ATT_TILE)
    s = _dot(q, jnp.concatenate([k_ref[0, :, pl.ds(start, ATT_TILE)], km_ref[0]], axis=1))
    r = lax.broadcasted_iota(jnp.int32, s.shape, 0)
    c = lax.broadcasted_iota(jnp.int32, s.shape, 1)
    last_visible = jnp.where(c >= ATT_TILE, ATT_TILE + N_META - 1, r)
    s = jnp.where(c <= last_visible, s, -1e30)
    for h in chains:
        cols = (h + 1) * chain_rows
        s_h = jnp.concatenate([chain(s, h)[:, :cols], chain(s, h)[:, ATT_TILE:]], axis=1)
        v_h = jnp.concatenate([v_ref[0, pl.ds(start, cols), :], vm_ref[0]], axis=0)
        update(h, s_h, v_h, True)

    def block(j, _):
        start = pl.multiple_of(j * ATT_TILE, ATT_TILE)
        s = _dot(q, k_ref[0, :, pl.ds(start, ATT_TILE)])
        vb = v_ref[0, pl.ds(start, ATT_TILE), :]
        for h in chains:
            update(h, chain(s, h), vb, False)
        return 0

    lax.fori_loop(0, i, block, 0)

    for h in chains:
        l = jnp.sum(l_refs[h][...], axis=1, keepdims=True)
        o_ref[pl.ds(h * chain_rows, chain_rows), :] = (acc_refs[h][...] / l).astype(BF16)


def _attention(q, k, v, km, vm, batch, seq):
    nq = seq // ATT_TILE
    return pl.pallas_call(
        _attention_kernel,
        grid=(batch, N_HEADS, nq),
        in_specs=[
            pl.BlockSpec((1, ATT_TILE, QK_DIM), lambda b, h, i: (h, b * nq + i, 0)),
            pl.BlockSpec((1, QK_DIM, seq), lambda b, h, i: (h, 0, b)),
            pl.BlockSpec((1, seq, V_DIM), lambda b, h, i: (h, b, 0)),
            pl.BlockSpec((1, QK_DIM, V7X_LANES), lambda b, h, i: (h, 0, 0)),
            pl.BlockSpec((1, V7X_LANES, V_DIM), lambda b, h, i: (h, 0, 0)),
        ],
        out_specs=pl.BlockSpec((ATT_TILE, V_DIM), lambda b, h, i: (b * nq + i, h)),
        out_shape=jax.ShapeDtypeStruct((batch * seq, MLA_WIDTH), BF16),
        scratch_shapes=[pltpu.VMEM((ATT_TILE // ATT_CHAINS, V7X_LANES), F32)] * (3 * ATT_CHAINS),
        compiler_params=pltpu.CompilerParams(
            dimension_semantics=("arbitrary", "arbitrary", "arbitrary"), vmem_limit_bytes=V7X_VMEM_LIMIT),
        name="attention",
    )(q, k, v, km, vm)


def _conv_kernel(cur_ref, prev_ref, meta_ref, w_ref, cb_ref, lng_ref, lnb_ref, o_ref,
                 win_ref, shift_ref, acc_ref):
    i = pl.program_id(1)

    @pl.when(i == 0)
    def _():
        win_ref[0:CONV_HALO - N_META, :] = jnp.zeros((CONV_HALO - N_META, CONV_CH), F32)
        win_ref[CONV_HALO - N_META:CONV_HALO, :] = meta_ref[...]

    @pl.when(i > 0)
    def _():
        win_ref[0:CONV_HALO, :] = prev_ref[...]

    win_ref[CONV_HALO:CONV_HALO + CONV_TILE, :] = cur_ref[...]
    win_ref[CONV_HALO + CONV_TILE:, :] = jnp.zeros((V7X_SUBLANES, CONV_CH), F32)

    base = CONV_HALO - (CONV_W - 1)
    win_rows = CONV_HALO + CONV_TILE
    for shift in range(1, V7X_SUBLANES):
        for r in range(0, win_rows, CONV_ROWS):
            shift_ref[shift - 1, pl.ds(r, CONV_ROWS), :] = win_ref[pl.ds(r + shift, CONV_ROWS), :]

    def window(shift, row, lanes):
        if shift == 0:
            return win_ref[pl.ds(row, CONV_ROWS), lanes]
        return shift_ref[shift - 1, pl.ds(row, CONV_ROWS), lanes]

    for rc in range(CONV_TILE // CONV_ROWS):
        r0 = rc * CONV_ROWS
        for c in range(CONV_CH // CONV_LANES):
            lanes = pl.ds(c * CONV_LANES, CONV_LANES)
            acc = jnp.zeros((CONV_ROWS, CONV_LANES), F32)
            for k in range(CONV_W):
                shift = (base + k) % V7X_SUBLANES
                w_k = jnp.concatenate([w_ref[k * V7X_SUBLANES:(k + 1) * V7X_SUBLANES, lanes]]
                                      * (CONV_ROWS // V7X_SUBLANES), axis=0)
                acc = acc + window(shift, r0 + base + k - shift, lanes) * w_k
            acc_ref[pl.ds(r0, CONV_ROWS), lanes] = acc

    y = _layer_norm(acc_ref[...] + cb_ref[...], lng_ref[...], lnb_ref[...])
    o_ref[...] = (y * jax.nn.sigmoid(y)).astype(BF16)


def _conv(glu, glu_meta, w, cb, lng, lnb, batch, seq):
    nt = seq // CONV_TILE
    per = CONV_TILE // CONV_HALO
    return pl.pallas_call(
        _conv_kernel,
        grid=(batch, nt),
        in_specs=[
            pl.BlockSpec((CONV_TILE, CONV_CH), lambda b, i: (b * nt + i, 0)),
            pl.BlockSpec((CONV_HALO, CONV_CH), lambda b, i: (jnp.maximum((b * nt + i) * per - 1, 0), 0)),
            _const_spec((N_META, CONV_CH)),
            _const_spec((CONV_W * V7X_SUBLANES, CONV_CH)),
            _const_spec((1, CONV_CH)), _const_spec((1, CONV_CH)), _const_spec((1, CONV_CH)),
        ],
        out_specs=pl.BlockSpec((CONV_TILE, CONV_CH), lambda b, i: (b * nt + i, 0)),
        out_shape=jax.ShapeDtypeStruct((batch * seq, CONV_CH), BF16),
        scratch_shapes=[pltpu.VMEM((CONV_HALO + CONV_TILE + V7X_SUBLANES, CONV_CH), F32),
                        pltpu.VMEM((V7X_SUBLANES - 1, CONV_HALO + CONV_TILE, CONV_CH), F32),
                        pltpu.VMEM((CONV_TILE, CONV_CH), F32)],
        compiler_params=pltpu.CompilerParams(
            dimension_semantics=("arbitrary", "arbitrary"), vmem_limit_bytes=V7X_VMEM_LIMIT),
        name="conv",
    )(glu, glu, glu_meta, w, cb, lng, lnb)


def _out_proj_kernel(attn_ref, conv_ref, x_ref, lng_ref, lnb_ref, wo_ref, g1_ref, b1_ref,
                     wrh_ref, wrl_ref, br_ref,
                     h1_ref, idx_ref, rank_ref, gate_ref, cnt_ref, carry_ref):
    step = pl.program_id(0)
    tm = x_ref.shape[0]

    @pl.when(step == 0)
    def _():
        carry_ref[...] = jnp.zeros_like(carry_ref)

    h0 = _layer_norm(x_ref[...], lng_ref[...], lnb_ref[...])
    mix = _dot(attn_ref[...], wo_ref[0:MLA_WIDTH, :]) + _dot(conv_ref[...], wo_ref[MLA_WIDTH:, :])
    h1 = _layer_norm(DEEPNORM_ALPHA * h0 + mix, g1_ref[...], b1_ref[...])
    _store_token_major(h1_ref, 0, h1)

    hi = h1.astype(BF16)
    lo = (h1 - hi.astype(F32)).astype(BF16)
    logits = (_dot(hi, wrh_ref[...]) + (_dot(hi, wrl_ref[...]) + _dot(lo, wrh_ref[...]))) + br_ref[...]

    lane = lax.broadcasted_iota(jnp.int32, (tm, N_EXPERTS), 1)
    work = logits
    vals, idxs = [], []
    for _ in range(TOP_K):
        mx = jnp.max(work, axis=1, keepdims=True)
        ix = jnp.min(jnp.where(work == mx, lane, N_EXPERTS), axis=1, keepdims=True)
        vals.append(mx)
        idxs.append(ix)
        work = jnp.where(lane == ix, -jnp.inf, work)
    exps = [jnp.exp(v - vals[0]) for v in vals]
    denom = exps[0] + exps[1] + exps[2] + exps[3]

    onehots = [(lane == ix) for ix in idxs]
    chosen = (onehots[0] | onehots[1] | onehots[2] | onehots[3])
    chosen_f = jnp.where(chosen, 1.0, 0.0)
    r = lax.broadcasted_iota(jnp.int32, (tm, tm), 0)
    c = lax.broadcasted_iota(jnp.int32, (tm, tm), 1)
    lower = jnp.where(c < r, 1.0, 0.0).astype(BF16)
    before = _dot(lower, chosen_f.astype(BF16)) + carry_ref[...]

    out_lane = lax.broadcasted_iota(jnp.int32, (tm, V7X_LANES), 1)
    idx_out = jnp.zeros((tm, V7X_LANES), jnp.int32)
    rank_out = jnp.zeros((tm, V7X_LANES), jnp.int32)
    gate_out = jnp.zeros((tm, V7X_LANES), F32)
    for k in range(TOP_K):
        rank_k = jnp.sum(jnp.where(onehots[k], before, 0.0), axis=1, keepdims=True).astype(jnp.int32)
        idx_out = jnp.where(out_lane == k, idxs[k], idx_out)
        rank_out = jnp.where(out_lane == k, rank_k, rank_out)
        gate_out = jnp.where(out_lane == k, exps[k] / denom, gate_out)
    idx_ref[...] = idx_out
    rank_ref[...] = rank_out
    gate_ref[...] = gate_out

    carry_ref[...] = carry_ref[...] + jnp.sum(chosen_f, axis=0, keepdims=True)
    cnt_ref[...] = carry_ref[...].astype(jnp.int32)


def _out_proj(attn, conv, x2d, lng, lnb, wo, g1, b1, wrh, wrl, br):
    rows = x2d.shape[0]
    tm = ROW_TILE
    row = lambda i: (i, 0)
    return pl.pallas_call(
        _out_proj_kernel,
        grid=(rows // tm,),
        in_specs=[
            pl.BlockSpec((tm, MLA_WIDTH), row), pl.BlockSpec((tm, CONV_CH), row),
            pl.BlockSpec((tm, D_MODEL), row),
            _const_spec((1, D_MODEL)), _const_spec((1, D_MODEL)),
            _const_spec((D_MODEL, D_MODEL)),
            _const_spec((1, D_MODEL)), _const_spec((1, D_MODEL)),
            _const_spec((D_MODEL, N_EXPERTS)), _const_spec((D_MODEL, N_EXPERTS)),
            _const_spec((1, N_EXPERTS)),
        ],
        out_specs=[
            pl.BlockSpec((tm * TOK_ROWS, V7X_LANES), row),
            pl.BlockSpec((tm, V7X_LANES), row), pl.BlockSpec((tm, V7X_LANES), row),
            pl.BlockSpec((tm, V7X_LANES), row),
            _const_spec((1, N_EXPERTS)),
        ],
        out_shape=[
            jax.ShapeDtypeStruct((rows * TOK_ROWS, V7X_LANES), F32),
            jax.ShapeDtypeStruct((rows, V7X_LANES), jnp.int32),
            jax.ShapeDtypeStruct((rows, V7X_LANES), jnp.int32),
            jax.ShapeDtypeStruct((rows, V7X_LANES), F32),
            jax.ShapeDtypeStruct((1, N_EXPERTS), jnp.int32),
        ],
        scratch_shapes=[pltpu.VMEM((1, N_EXPERTS), F32)],
        compiler_params=pltpu.CompilerParams(
            dimension_semantics=("arbitrary",), vmem_limit_bytes=V7X_VMEM_LIMIT),
        name="out_proj_router",
    )(attn, conv, x2d, lng, lnb, wo, g1, b1, wrh, wrl, br)


def _token_copy(src_ref, src_tok, dst_ref, dst_tok, sem, n_tok=1):
    rows = n_tok * TOK_ROWS
    src = src_ref.at[pl.ds(pl.multiple_of(src_tok * TOK_ROWS, TOK_ROWS), rows), :]
    dst = dst_ref.at[pl.ds(pl.multiple_of(dst_tok * TOK_ROWS, TOK_ROWS), rows), :]
    return pltpu.make_async_copy(src, dst, sem)


def _dispatch_kernel(dest_ref, padrow_ref, npad_ref, h1_ref, xs_ref, zero_ref, sem, zsem):
    step = pl.program_id(0)
    base = step * (DISPATCH_TILE * TOP_K)

    def copies(t):
        return [_token_copy(h1_ref, t, xs_ref, dest_ref[base + t * TOP_K + k], sem) for k in range(TOP_K)]

    def start(t, _):
        for c in copies(t):
            c.start()
        return 0

    def wait(t, _):
        for c in copies(t):
            c.wait()
        return 0

    lax.fori_loop(0, DISPATCH_TILE, start, 0)

    @pl.when(step == 0)
    def _():
        zero_ref[...] = jnp.zeros_like(zero_ref)
        n = npad_ref[0]
        tail_start = npad_ref[1]
        n_tail = (xs_ref.shape[0] // TOK_ROWS - tail_start) // SEG_ALIGN

        def zstart(p, _):
            _token_copy(zero_ref, 0, xs_ref, padrow_ref[p], zsem).start()
            return 0

        def zwait(p, _):
            _token_copy(zero_ref, 0, xs_ref, padrow_ref[p], zsem).wait()
            return 0

        def tstart(b, _):
            _token_copy(zero_ref, 0, xs_ref, tail_start + b * SEG_ALIGN, zsem, SEG_ALIGN).start()
            return 0

        def twait(b, _):
            _token_copy(zero_ref, 0, xs_ref, tail_start + b * SEG_ALIGN, zsem, SEG_ALIGN).wait()
            return 0

        lax.fori_loop(0, n, zstart, 0)
        lax.fori_loop(0, n, zwait, 0)
        lax.fori_loop(0, n_tail, tstart, 0)
        lax.fori_loop(0, n_tail, twait, 0)

    lax.fori_loop(0, DISPATCH_TILE, wait, 0)


def _dispatch(dest_flat, pad_rows, n_pad, h1t, n_rows):
    tokens = h1t.shape[0] // TOK_ROWS
    return pl.pallas_call(
        _dispatch_kernel,
        grid_spec=pltpu.PrefetchScalarGridSpec(
            num_scalar_prefetch=3,
            grid=(tokens // DISPATCH_TILE,),
            in_specs=[pl.BlockSpec((DISPATCH_TILE * TOK_ROWS, V7X_LANES), lambda i, *_: (i, 0))],
            out_specs=pl.BlockSpec(memory_space=pl.ANY),
            scratch_shapes=[pltpu.VMEM((SEG_ALIGN * TOK_ROWS, V7X_LANES), F32),
                            pltpu.SemaphoreType.DMA, pltpu.SemaphoreType.DMA],
        ),
        out_shape=jax.ShapeDtypeStruct((n_rows * TOK_ROWS, V7X_LANES), F32),
        compiler_params=pltpu.CompilerParams(
            dimension_semantics=("arbitrary",), has_side_effects=True),
        name="dispatch",
    )(dest_flat, pad_rows, n_pad, h1t)


N_FF = D_FF // FF_TILE
SUB = SEG_ALIGN
N_SUB = SUPER_ROWS // SUB
PREFETCH_SLOTS = -(-N_SUB // N_FF)
OUT_SLOTS = MM_ROWS // SUB


def _experts_kernel(st_e_ref, st_start_ref, st_rows_ref, n_used_ref,
                    xs_ref, w1g_ref, w1u_ref, b1g_ref, b1u_ref, w2_ref, b2_ref,
                    ys_ref,
                    xb_ref, acc_ref, wg_ref, wu_ref, wd_ref, xstage_ref, ostage_ref, xsem, osem):
    s = pl.program_id(0)
    j = pl.program_id(1)
    parity = s % 2
    rows = st_rows_ref[s]
    start = st_start_ref[s]
    n_blk = rows // SUB
    next_start = st_start_ref[s + 1]
    next_blk = st_rows_ref[s + 1] // SUB

    def x_copy(tok0, slot):
        return _token_copy(xs_ref, tok0, xstage_ref.at[slot], 0, xsem.at[slot], SUB)

    def y_copy(slot, tok0):
        return _token_copy(ostage_ref.at[slot], 0, ys_ref, tok0, osem.at[slot], SUB)

    def convert(slot, par, blk):
        off = pl.multiple_of(blk * SUB, SUB)
        for c in range(TOK_ROWS):
            xb_ref[par, pl.ds(off, SUB), c * V7X_LANES:(c + 1) * V7X_LANES] = (
                _load_token_major(xstage_ref.at[slot], 0, SUB, c).astype(BF16))

    @pl.when((s == 0) & (j == 0))
    def _():
        ostage_ref[0] = jnp.zeros(ostage_ref.shape[1:], F32)
        tail_start = n_used_ref[1]
        n_tail = (ys_ref.shape[0] // TOK_ROWS - tail_start) // SUB

        def tstart(b, _):
            y_copy(0, tail_start + b * SUB).start()
            return 0

        def twait(b, _):
            y_copy(0, tail_start + b * SUB).wait()
            return 0

        lax.fori_loop(0, n_tail, tstart, 0)
        lax.fori_loop(0, n_tail, twait, 0)

        def first(b, _):
            x_copy(start + b * SUB, 0).start()
            x_copy(start + b * SUB, 0).wait()
            convert(0, 0, b)
            return 0

        lax.fori_loop(0, n_blk, first, 0)

    for p in range(PREFETCH_SLOTS):
        @pl.when(j * PREFETCH_SLOTS + p < next_blk)
        def _(p=p):
            x_copy(next_start + (j * PREFETCH_SLOTS + p) * SUB, p).start()

    def compute(last):
        wg_ref[...] = w1g_ref[0].astype(BF16)
        wu_ref[...] = w1u_ref[0].astype(BF16)
        wd_ref[...] = w2_ref[0].astype(BF16)
        bg = b1g_ref[0]
        bu = b1u_ref[0]
        b2 = b2_ref[0]

        def chunk(row0, m):
            off = pl.multiple_of(row0, SUB)
            xb = xb_ref[parity, pl.ds(off, m), :]
            g = _dot(xb, wg_ref[...]) + bg
            u = _dot(xb, wu_ref[...]) + bu
            g = jnp.minimum(g, SWIGLU_LIMIT)
            u = jnp.clip(u, -SWIGLU_LIMIT, SWIGLU_LIMIT)
            act = g * jax.nn.sigmoid(SWIGLU_ALPHA * g) * (u + 1.0)
            y = _dot(act.astype(BF16), wd_ref[...])
            if not last:
                acc_ref[pl.ds(off, m), :] += y
                return
            y = acc_ref[pl.ds(off, m), :] + y + b2
            for i in range(m // SUB):
                blk = row0 // SUB + i
                slot = blk % OUT_SLOTS

                @pl.when(blk >= OUT_SLOTS)
                def _(blk=blk, slot=slot):
                    y_copy(slot, start + (blk - OUT_SLOTS) * SUB).wait()

                _store_token_major(ostage_ref.at[slot], 0, y[i * SUB:(i + 1) * SUB, :])
                y_copy(slot, start + blk * SUB).start()

        n_big = rows // MM_ROWS

        def big(r, _):
            chunk(r * MM_ROWS, MM_ROWS)
            return 0

        lax.fori_loop(0, n_big, big, 0)
        done = n_big * MM_ROWS
        m = MM_ROWS // 2
        while m >= SUB:
            take = ((rows - done) & m) != 0

            @pl.when(take)
            def _(done=done, m=m):
                chunk(done, m)

            done = done + jnp.where(take, m, 0)
            m //= 2

        if last:
            for i in range(OUT_SLOTS):
                @pl.when(i < n_blk)
                def _(i=i):
                    blk = n_blk - 1 - i
                    y_copy(blk % OUT_SLOTS, start + blk * SUB).wait()

    @pl.when((j == 0) & (rows > 0))
    def _():
        def zero(b, _):
            acc_ref[pl.ds(pl.multiple_of(b * SUB, SUB), SUB), :] = jnp.zeros((SUB, D_MODEL), F32)
            return 0

        lax.fori_loop(0, n_blk, zero, 0)

    @pl.when((j < N_FF - 1) & (rows > 0))
    def _():
        compute(False)

    @pl.when((j == N_FF - 1) & (rows > 0))
    def _():
        compute(True)

    for p in range(PREFETCH_SLOTS):
        @pl.when(j * PREFETCH_SLOTS + p < next_blk)
        def _(p=p):
            blk = j * PREFETCH_SLOTS + p
            x_copy(next_start + blk * SUB, p).wait()
            convert(p, 1 - parity, blk)


def _experts(st_e, st_start, st_rows, n_used, xs, w1, b1, w2, b2, n_super):
    n_rows = xs.shape[0] // TOK_ROWS

    def ff(s, j, n_used_ref):
        return jnp.where(s < n_used_ref[0], j, N_FF - 1)

    w1g_map = lambda s, j, e, st, rw, nu: (e[s], 0, ff(s, j, nu))
    w1u_map = lambda s, j, e, st, rw, nu: (e[s], 0, N_FF + ff(s, j, nu))
    w2_map = lambda s, j, e, st, rw, nu: (e[s], ff(s, j, nu), 0)
    b2_map = lambda s, j, e, st, rw, nu: (e[s], 0, 0)
    return pl.pallas_call(
        _experts_kernel,
        grid_spec=pltpu.PrefetchScalarGridSpec(
            num_scalar_prefetch=4,
            grid=(n_super, N_FF),
            in_specs=[
                pl.BlockSpec(memory_space=pl.ANY),
                pl.BlockSpec((1, D_MODEL, FF_TILE), w1g_map),
                pl.BlockSpec((1, D_MODEL, FF_TILE), w1u_map),
                pl.BlockSpec((1, 1, FF_TILE), w1g_map),
                pl.BlockSpec((1, 1, FF_TILE), w1u_map),
                pl.BlockSpec((1, FF_TILE, D_MODEL), w2_map),
                pl.BlockSpec((1, 1, D_MODEL), b2_map),
            ],
            out_specs=pl.BlockSpec(memory_space=pl.ANY),
            scratch_shapes=[
                pltpu.VMEM((2, SUPER_ROWS, D_MODEL), BF16),
                pltpu.VMEM((SUPER_ROWS, D_MODEL), F32),
                pltpu.VMEM((D_MODEL, FF_TILE), BF16),
                pltpu.VMEM((D_MODEL, FF_TILE), BF16),
                pltpu.VMEM((FF_TILE, D_MODEL), BF16),
                pltpu.VMEM((PREFETCH_SLOTS, SUB * TOK_ROWS, V7X_LANES), F32),
                pltpu.VMEM((OUT_SLOTS, SUB * TOK_ROWS, V7X_LANES), F32),
                pltpu.SemaphoreType.DMA((PREFETCH_SLOTS,)),
                pltpu.SemaphoreType.DMA((OUT_SLOTS,)),
            ],
        ),
        out_shape=jax.ShapeDtypeStruct((n_rows * TOK_ROWS, V7X_LANES), F32),
        compiler_params=pltpu.CompilerParams(
            dimension_semantics=("arbitrary", "arbitrary"), vmem_limit_bytes=V7X_VMEM_LIMIT,
            has_side_effects=True),
        name="experts",
    )(st_e, st_start, st_rows, n_used, xs, w1, w1, b1, b1, w2, b2)


def _combine_kernel(dest_ref, ys_ref, gate_ref, h1_ref, g2_ref, b2_ref, o_ref, buf_ref, sem):
    step = pl.program_id(0)
    n_steps = pl.num_programs(0)
    tm = COMBINE_TILE

    def copies(at_step, slot, t):
        base = at_step * (tm * TOP_K)
        return [_token_copy(ys_ref, dest_ref[base + t * TOP_K + k], buf_ref.at[slot, k], t, sem.at[slot])
                for k in range(TOP_K)]

    def gather(at_step, slot):
        def start(t, _):
            for c in copies(at_step, slot, t):
                c.start()
            return 0

        lax.fori_loop(0, tm, start, 0)

    @pl.when(step == 0)
    def _():
        gather(0, 0)

    @pl.when(step + 1 < n_steps)
    def _():
        gather(step + 1, (step + 1) % 2)

    slot = step % 2

    def wait(t, _):
        for c in copies(step, slot, t):
            c.wait()
        return 0

    lax.fori_loop(0, tm, wait, 0)

    gates = gate_ref[...]
    for c in range(TOK_ROWS):
        z = DEEPNORM_ALPHA * _load_token_major(h1_ref, 0, tm, c)
        for k in range(TOP_K):
            z = z + _load_token_major(buf_ref.at[slot, k], 0, tm, c) * gates[:, k:k + 1]
        o_ref[:, c * V7X_LANES:(c + 1) * V7X_LANES] = z
    o_ref[...] = _layer_norm(o_ref[...], g2_ref[...], b2_ref[...])


def _combine(dest_flat, ys, gates, h1t, g2, b2):
    tokens = h1t.shape[0] // TOK_ROWS
    tm = COMBINE_TILE
    row = lambda i, *_: (i, 0)
    return pl.pallas_call(
        _combine_kernel,
        grid_spec=pltpu.PrefetchScalarGridSpec(
            num_scalar_prefetch=1,
            grid=(tokens // tm,),
            in_specs=[
                pl.BlockSpec(memory_space=pl.ANY),
                pl.BlockSpec((tm, V7X_LANES), row),
                pl.BlockSpec((tm * TOK_ROWS, V7X_LANES), row),
                pl.BlockSpec((1, D_MODEL), lambda i, *_: (0, 0)),
                pl.BlockSpec((1, D_MODEL), lambda i, *_: (0, 0)),
            ],
            out_specs=pl.BlockSpec((tm, D_MODEL), row),
            scratch_shapes=[pltpu.VMEM((2, TOP_K, tm * TOK_ROWS, V7X_LANES), F32),
                            pltpu.SemaphoreType.DMA((2,))],
        ),
        out_shape=jax.ShapeDtypeStruct((tokens, D_MODEL), F32),
        compiler_params=pltpu.CompilerParams(
            dimension_semantics=("arbitrary",), vmem_limit_bytes=V7X_VMEM_LIMIT),
        name="combine",
    )(dest_flat, ys, gates, h1t, g2, b2)


def _rotate_half_cols(w):
    half = QK_ROPE // 2
    return jnp.concatenate([-w[..., half:], w[..., :half]], axis=-1)


def _rope_table(length):
    inv_freq = 1.0 / (ROPE_THETA ** (jnp.arange(0, QK_ROPE, 2, dtype=F32) / QK_ROPE))
    freqs = jnp.arange(length, dtype=F32)[:, None] * inv_freq[None, :]
    emb = jnp.concatenate([freqs, freqs], axis=-1)
    return jnp.concatenate([jnp.cos(emb), jnp.sin(emb)], axis=-1)


def _routing_plan(idx, rank, counts, n_super):
    experts = jnp.arange(N_EXPERTS, dtype=jnp.int32)

    def lookup(table, i):
        return jnp.sum(jnp.where(i[..., None] == experts, table, 0), axis=-1)

    def bucket(cum, i):
        return jnp.minimum(jnp.sum((cum <= i[..., None]).astype(jnp.int32), axis=-1), N_EXPERTS - 1)

    counts = counts.astype(jnp.int32)
    padded = (counts + SEG_ALIGN - 1) // SEG_ALIGN * SEG_ALIGN
    pad_end = jnp.cumsum(padded)
    pad_start = pad_end - padded
    dest = (lookup(pad_start, idx) + rank).reshape(-1).astype(jnp.int32)

    n_padmax = N_EXPERTS * SEG_ALIGN
    padcnt = padded - counts
    padcum = jnp.cumsum(padcnt)
    p = jnp.arange(n_padmax, dtype=jnp.int32)
    pe = bucket(padcum, p)
    pad_rows = lookup(pad_start + counts - (padcum - padcnt), pe) + p
    n_pad = jnp.stack([padcum[-1], pad_end[-1]]).astype(jnp.int32)
    pad_rows = jnp.where(p < n_pad[0], pad_rows, 0).astype(jnp.int32)

    n_st = (padded + SUPER_ROWS - 1) // SUPER_ROWS
    st_cum = jnp.cumsum(n_st)
    n_used = jnp.stack([st_cum[-1], pad_end[-1]]).astype(jnp.int32)
    s = jnp.arange(n_super + 1, dtype=jnp.int32)
    s_eff = jnp.minimum(s, n_used[0] - 1)
    se = bucket(st_cum, s_eff).astype(jnp.int32)
    local = s_eff - lookup(st_cum - n_st, se)
    st_start = (lookup(pad_start, se) + local * SUPER_ROWS).astype(jnp.int32)
    st_rows = jnp.clip(lookup(padded, se) - local * SUPER_ROWS, 0, SUPER_ROWS)
    st_rows = jnp.where(s < n_used[0], st_rows, 0).astype(jnp.int32)
    return dest, pad_rows, n_pad, se, st_start, st_rows, n_used


def kernel(x, meta_tokens, ln_in_g, ln_in_b, w_in, q_norm_g, w_uq, kv_norm_g, w_uk, w_uv, conv_dw_w,
           conv_dw_b, conv_ln_g, conv_ln_b, w_out, ln1_g, ln1_b, w_router, b_router, w_mlp1, b_mlp1,
           w_mlp2, b_mlp2, ln2_g, ln2_b):
    batch, seq, _ = x.shape
    tokens = batch * seq
    row2 = lambda a: a.reshape(1, -1)

    wi = w_in[0]
    s_kpe = Q_LORA + KV_LORA
    kpe_w = wi[:, s_kpe:s_kpe + QK_ROPE]
    w1 = jnp.concatenate(
        [wi[:, :s_kpe], kpe_w, _rotate_half_cols(kpe_w), wi[:, s_kpe + QK_ROPE:]], axis=1).astype(BF16)
    wq3 = w_uq[0].reshape(Q_LORA, N_HEADS, QK_DIM)
    wq_nope = wq3[:, :, :QK_NOPE].reshape(Q_LORA, N_HEADS * QK_NOPE)
    wq_pe = wq3[:, :, QK_NOPE:]
    wq_pr = jnp.concatenate([wq_pe, _rotate_half_cols(wq_pe)], axis=-1).reshape(Q_LORA, N_HEADS * 2 * QK_ROPE)
    wq = jnp.concatenate([wq_nope, wq_pr], axis=1).astype(BF16)
    wuk = w_uk[0].astype(BF16)
    wuv = w_uv[0].astype(BF16)
    wo = w_out[0].astype(BF16)
    wr = w_router[0]
    wr_hi = wr.astype(BF16)
    wr_lo = (wr - wr_hi.astype(F32)).astype(BF16)
    cs = _rope_table(N_META + seq)
    conv_w = jnp.repeat(conv_dw_w[0], V7X_SUBLANES, axis=0)

    x2d = x.reshape(tokens, D_MODEL)
    proj_args = (row2(ln_in_g), row2(ln_in_b), w1, row2(q_norm_g[0]), row2(kv_norm_g[0]), wq, wuk, wuv)

    _, k_meta, v_meta, glu_meta = _in_proj(meta_tokens, *proj_args, cs[:N_META], N_META)
    q, k, v, glu = _in_proj(x2d, *proj_args, cs[N_META:], ROW_TILE)
    k_meta = jnp.pad(k_meta, ((0, 0), (0, 0), (0, V7X_LANES - N_META)))
    v_meta = jnp.pad(v_meta, ((0, 0), (0, V7X_LANES - N_META), (0, 0)))
    attn = _attention(q, k, v, k_meta, v_meta, batch, seq)
    conv = _conv(glu, glu_meta, conv_w, row2(conv_dw_b[0]), row2(conv_ln_g[0]), row2(conv_ln_b[0]), batch, seq)

    h1t, idx, rank, gates, counts = _out_proj(
        attn, conv, x2d, row2(ln_in_g), row2(ln_in_b), wo, row2(ln1_g[0]), row2(ln1_b[0]),
        wr_hi, wr_lo, row2(b_router[0]))

    n_assign = tokens * TOP_K
    n_rows = n_assign + N_EXPERTS * SEG_ALIGN
    n_super = N_EXPERTS + -(-n_assign // SUPER_ROWS)
    dest, pad_rows, n_pad, st_e, st_start, st_rows, n_used = _routing_plan(
        idx[:, :TOP_K], rank[:, :TOP_K], counts[0], n_super)

    xs = _dispatch(dest, pad_rows, n_pad, h1t, n_rows)
    ys = _experts(st_e, st_start, st_rows, n_used, xs, w_mlp1[0], b_mlp1[0].reshape(N_EXPERTS, 1, 2 * D_FF),
                  w_mlp2[0], b_mlp2[0].reshape(N_EXPERTS, 1, D_MODEL), n_super)
    out = _combine(dest, ys, gates, h1t, row2(ln2_g[0]), row2(ln2_b[0]))
    return out.reshape(batch, seq, D_MODEL)
```

```python
import functools
import math

import jax
import jax.numpy as jnp
from jax import lax
from jax.experimental import pallas as pl
from jax.experimental.pallas import tpu as pltpu

D_MODEL = 2048
N_META = 16
N_HEADS = 8
QK_NOPE = 128
QK_ROPE = 64
QK_DIM = QK_NOPE + QK_ROPE
V_DIM = 128
Q_LORA = 768
KV_LORA = 512
ROPE_THETA = 10000.0
MLA_WIDTH = N_HEADS * V_DIM
CONV_CH = 1024
CONV_W = 31
N_EXPERTS = 32
TOP_K = 4
D_FF = 2048
SWIGLU_LIMIT = 7.0
SWIGLU_ALPHA = 1.702
DEEPNORM_ALPHA = 2.0 ** 0.25
LN_EPS = 1e-5
RMS_EPS = 1e-6

V7X_LANES = 128
V7X_SUBLANES = 8
V7X_VMEM_LIMIT = 56 * 1024 * 1024

ROW_TILE = 256
OUT_TILE = 256
OUT_CHAINS = 1
ATT_TILE = 512
ATT_CHAINS = 2
CONV_TILE = 256
CONV_HALO = 32
CONV_ROWS = 32
CONV_LANES = 256
DISPATCH_TILE = 256
COMBINE_TILE = 128
SEG_ALIGN = 128
SUPER_ROWS = 1536
FF_TILE = 256
MM_ROWS = 512

F32 = jnp.float32
BF16 = jnp.bfloat16


def _dot(a, b):
    return jnp.dot(a, b, preferred_element_type=F32)


def _dot_nt(a, b):
    return lax.dot_general(a, b, (((1,), (1,)), ((), ())), preferred_element_type=F32)


def _layer_norm(x, g, b):
    mu = jnp.mean(x, axis=-1, keepdims=True)
    xc = x - mu
    var = jnp.mean(xc * xc, axis=-1, keepdims=True)
    return xc * lax.rsqrt(var + LN_EPS) * g + b


def _rms_norm(x, g):
    ms = jnp.mean(x * x, axis=-1, keepdims=True)
    return x * lax.rsqrt(ms + RMS_EPS) * g


def _const_spec(shape):
    zeros = (0,) * len(shape)
    return pl.BlockSpec(shape, lambda *_: zeros)


TOK_ROWS = D_MODEL // V7X_LANES


def _load_token_major(ref, row0, n_tok, j):
    return ref[pl.ds(row0 + j, n_tok, stride=TOK_ROWS), :]


def _store_token_major(ref, row0, x):
    n_tok = x.shape[0]
    for j in range(TOK_ROWS):
        ref[pl.ds(row0 + j, n_tok, stride=TOK_ROWS), :] = x[:, j * V7X_LANES:(j + 1) * V7X_LANES]


def _in_proj_kernel(x_ref, lng_ref, lnb_ref, wc_ref, wkpe_ref, wa_ref, wg_ref, qg_ref, kvg_ref,
                    wq_ref, wuk_ref, wuv_ref, cs_ref,
                    q_ref, k_ref, v_ref, glu_ref):
    h0 = _layer_norm(x_ref[...], lng_ref[...], lnb_ref[...])
    hb = h0.astype(BF16)
    cs = cs_ref[...]

    def rope(t128):
        t = t128 * cs
        return t + pltpu.roll(t, QK_ROPE, axis=1)

    cq = _dot(hb, wc_ref[:, 0:Q_LORA])
    cqn = _rms_norm(cq, qg_ref[...]).astype(BF16)
    ckv = _dot(hb, wc_ref[:, Q_LORA:Q_LORA + KV_LORA])
    ckvn = _rms_norm(ckv, kvg_ref[...]).astype(BF16)
    kpe = rope(_dot(hb, wkpe_ref[...]))

    a = _dot(hb, wa_ref[...])
    g = _dot(hb, wg_ref[...])
    glu_ref[...] = a * jax.nn.sigmoid(g)

    knope = _dot(ckvn, wuk_ref[...])
    v = _dot(ckvn, wuv_ref[...])
    kpe_t = kpe.T[0:QK_ROPE, :].astype(BF16)
    for h in range(N_HEADS):
        k_ref[h, 0:QK_NOPE, :] = knope[:, h * QK_NOPE:(h + 1) * QK_NOPE].T.astype(BF16)
        k_ref[h, QK_NOPE:QK_DIM, :] = kpe_t
        v_ref[h] = v[:, h * V_DIM:(h + 1) * V_DIM].astype(BF16)

    qn = _dot(cqn, wq_ref[:, 0:N_HEADS * QK_NOPE])
    qp = _dot(cqn, wq_ref[:, N_HEADS * QK_NOPE:])
    for h in range(N_HEADS):
        q_ref[h, :, 0:QK_NOPE] = qn[:, h * QK_NOPE:(h + 1) * QK_NOPE].astype(BF16)
        q_ref[h, :, QK_NOPE:QK_DIM] = rope(qp[:, h * V7X_LANES:(h + 1) * V7X_LANES])[:, :QK_ROPE].astype(BF16)


def _in_proj(x2d, lng, lnb, wc, wkpe, wa, wg, qg, kvg, wq, wuk, wuv, cs, tm):
    rows = x2d.shape[0]
    n_cs = cs.shape[0] // tm
    row = lambda i: (i, 0)
    head_row = lambda i: (0, i, 0)
    return pl.pallas_call(
        _in_proj_kernel,
        grid=(rows // tm,),
        in_specs=[
            pl.BlockSpec((tm, D_MODEL), row),
            _const_spec((1, D_MODEL)), _const_spec((1, D_MODEL)),
            _const_spec((D_MODEL, Q_LORA + KV_LORA)), _const_spec((D_MODEL, 2 * QK_ROPE)),
            _const_spec((D_MODEL, CONV_CH)), _const_spec((D_MODEL, CONV_CH)),
            _const_spec((1, Q_LORA)), _const_spec((1, KV_LORA)),
            _const_spec((Q_LORA, 2 * N_HEADS * QK_NOPE)),
            _const_spec((KV_LORA, N_HEADS * QK_NOPE)), _const_spec((KV_LORA, MLA_WIDTH)),
            pl.BlockSpec((tm, V7X_LANES), lambda i: (i % n_cs, 0)),
        ],
        out_specs=[
            pl.BlockSpec((N_HEADS, tm, QK_DIM), head_row),
            pl.BlockSpec((N_HEADS, QK_DIM, tm), lambda i: (0, 0, i)),
            pl.BlockSpec((N_HEADS, tm, V_DIM), head_row),
            pl.BlockSpec((tm, CONV_CH), row),
        ],
        out_shape=[
            jax.ShapeDtypeStruct((N_HEADS, rows, QK_DIM), BF16),
            jax.ShapeDtypeStruct((N_HEADS, QK_DIM, rows), BF16),
            jax.ShapeDtypeStruct((N_HEADS, rows, V_DIM), BF16),
            jax.ShapeDtypeStruct((rows, CONV_CH), F32),
        ],
        compiler_params=pltpu.CompilerParams(
            dimension_semantics=("arbitrary",), vmem_limit_bytes=V7X_VMEM_LIMIT),
        name="in_proj",
    )(x2d, lng, lnb, wc, wkpe, wa, wg, qg, kvg, wq, wuk, wuv, cs)


def _attention_kernel(q_ref, k_ref, v_ref, km_ref, vm_ref, o_ref, *state):
    i = pl.program_id(2)
    c_exp = (1.0 / math.sqrt(QK_DIM)) * math.log2(math.e)
    chain_rows = ATT_TILE // ATT_CHAINS
    chains = range(ATT_CHAINS)
    m_refs, l_refs, acc_refs = (state[n * ATT_CHAINS:(n + 1) * ATT_CHAINS] for n in range(3))

    def lane_tiles(x):
        return [x[:, t * V7X_LANES:(t + 1) * V7X_LANES] for t in range(x.shape[1] // V7X_LANES)]

    def row_max(x):
        if x.shape[1] % V7X_LANES:
            return jnp.max(x, axis=1, keepdims=True)
        return jnp.max(functools.reduce(jnp.maximum, lane_tiles(x)), axis=1, keepdims=True)

    def lane_partial_sum(x):
        if x.shape[1] % V7X_LANES:
            lane = lax.broadcasted_iota(jnp.int32, (x.shape[0], V7X_LANES), 1)
            return jnp.where(lane == 0, jnp.sum(x, axis=1, keepdims=True), 0.0)
        return functools.reduce(jnp.add, lane_tiles(x))

    def update(h, s, vb, first):
        s_max = jnp.broadcast_to(row_max(s), (s.shape[0], V7X_LANES))
        if first:
            m_new = s_max
        else:
            m_old = m_refs[h][...]
            m_new = jnp.maximum(m_old, s_max)
            alpha = jnp.exp2(c_exp * (m_old - m_new))
        if s.shape[1] % V7X_LANES:
            p = jnp.exp2(c_exp * (s - m_new[:, :s.shape[1]]))
        else:
            p = jnp.concatenate([jnp.exp2(c_exp * (t - m_new)) for t in lane_tiles(s)], axis=1)
        p_sum = lane_partial_sum(p)
        pv = _dot(p.astype(BF16), vb)
        m_refs[h][...] = m_new
        if first:
            l_refs[h][...] = p_sum
            acc_refs[h][...] = pv
        else:
            l_refs[h][...] = alpha * l_refs[h][...] + p_sum
            acc_refs[h][...] = alpha * acc_refs[h][...] + pv

    def chain(x, h):
        return x[h * chain_rows:(h + 1) * chain_rows, :]

    q = q_ref[0]

    start = pl.multiple_of(i * ATT_TILE, ATT_TILE)
    s = _dot(q, jnp.concatenate([k_ref[0, :, pl.ds(start, ATT_TILE)], km_ref[0]], axis=1))
    r = lax.broadcasted_iota(jnp.int32, s.shape, 0)
    c = lax.broadcasted_iota(jnp.int32, s.shape, 1)
    last_visible = jnp.where(c >= ATT_TILE, ATT_TILE + N_META - 1, r)
    s = jnp.where(c <= last_visible, s, -1e30)
    for h in chains:
        cols = (h + 1) * chain_rows
        s_h = jnp.concatenate([chain(s, h)[:, :cols], chain(s, h)[:, ATT_TILE:]], axis=1)
        v_h = jnp.concatenate([v_ref[0, pl.ds(start, cols), :], vm_ref[0]], axis=0)
        update(h, s_h, v_h, True)

    def block(j, _):
        start = pl.multiple_of(j * ATT_TILE, ATT_TILE)
        s = _dot(q, k_ref[0, :, pl.ds(start, ATT_TILE)])
        vb = v_ref[0, pl.ds(start, ATT_TILE), :]
        for h in chains:
            update(h, chain(s, h), vb, False)
        return 0

    lax.fori_loop(0, i, block, 0)

    for h in chains:
        l = jnp.sum(l_refs[h][...], axis=1, keepdims=True)
        o_ref[pl.ds(h * chain_rows, chain_rows), :] = (acc_refs[h][...] / l).astype(BF16)


def _attention(q, k, v, km, vm, batch, seq):
    nq = seq // ATT_TILE
    return pl.pallas_call(
        _attention_kernel,
        grid=(batch, N_HEADS, nq),
        in_specs=[
            pl.BlockSpec((1, ATT_TILE, QK_DIM), lambda b, h, i: (h, b * nq + i, 0)),
            pl.BlockSpec((1, QK_DIM, seq), lambda b, h, i: (h, 0, b)),
            pl.BlockSpec((1, seq, V_DIM), lambda b, h, i: (h, b, 0)),
            pl.BlockSpec((1, QK_DIM, V7X_LANES), lambda b, h, i: (h, 0, 0)),
            pl.BlockSpec((1, V7X_LANES, V_DIM), lambda b, h, i: (h, 0, 0)),
        ],
        out_specs=pl.BlockSpec((ATT_TILE, V_DIM), lambda b, h, i: (b * nq + i, h)),
        out_shape=jax.ShapeDtypeStruct((batch * seq, MLA_WIDTH), BF16),
        scratch_shapes=[pltpu.VMEM((ATT_TILE // ATT_CHAINS, V7X_LANES), F32)] * (3 * ATT_CHAINS),
        compiler_params=pltpu.CompilerParams(
            dimension_semantics=("arbitrary", "arbitrary", "arbitrary"), vmem_limit_bytes=V7X_VMEM_LIMIT),
        name="attention",
    )(q, k, v, km, vm)


def _conv_kernel(cur_ref, prev_ref, meta_ref, w_ref, cb_ref, lng_ref, lnb_ref, o_ref,
                 win_ref, shift_ref, acc_ref):
    i = pl.program_id(1)

    @pl.when(i == 0)
    def _():
        win_ref[0:CONV_HALO - N_META, :] = jnp.zeros((CONV_HALO - N_META, CONV_CH), F32)
        win_ref[CONV_HALO - N_META:CONV_HALO, :] = meta_ref[...]

    @pl.when(i > 0)
    def _():
        win_ref[0:CONV_HALO, :] = prev_ref[...]

    win_ref[CONV_HALO:CONV_HALO + CONV_TILE, :] = cur_ref[...]
    win_ref[CONV_HALO + CONV_TILE:, :] = jnp.zeros((V7X_SUBLANES, CONV_CH), F32)

    base = CONV_HALO - (CONV_W - 1)
    win_rows = CONV_HALO + CONV_TILE
    for shift in range(1, V7X_SUBLANES):
        for r in range(0, win_rows, CONV_ROWS):
            shift_ref[shift - 1, pl.ds(r, CONV_ROWS), :] = win_ref[pl.ds(r + shift, CONV_ROWS), :]

    def window(shift, row, lanes):
        if shift == 0:
            return win_ref[pl.ds(row, CONV_ROWS), lanes]
        return shift_ref[shift - 1, pl.ds(row, CONV_ROWS), lanes]

    for rc in range(CONV_TILE // CONV_ROWS):
        r0 = rc * CONV_ROWS
        for c in range(CONV_CH // CONV_LANES):
            lanes = pl.ds(c * CONV_LANES, CONV_LANES)
            acc = jnp.zeros((CONV_ROWS, CONV_LANES), F32)
            for k in range(CONV_W):
                shift = (base + k) % V7X_SUBLANES
                w_k = jnp.concatenate([w_ref[k * V7X_SUBLANES:(k + 1) * V7X_SUBLANES, lanes]]
                                      * (CONV_ROWS // V7X_SUBLANES), axis=0)
                acc = acc + window(shift, r0 + base + k - shift, lanes) * w_k
            acc_ref[pl.ds(r0, CONV_ROWS), lanes] = acc

    y = _layer_norm(acc_ref[...] + cb_ref[...], lng_ref[...], lnb_ref[...])
    o_ref[...] = (y * jax.nn.sigmoid(y)).astype(BF16)


def _conv(glu, glu_meta, w, cb, lng, lnb, batch, seq):
    nt = seq // CONV_TILE
    per = CONV_TILE // CONV_HALO
    return pl.pallas_call(
        _conv_kernel,
        grid=(batch, nt),
        in_specs=[
            pl.BlockSpec((CONV_TILE, CONV_CH), lambda b, i: (b * nt + i, 0)),
            pl.BlockSpec((CONV_HALO, CONV_CH), lambda b, i: (jnp.maximum((b * nt + i) * per - 1, 0), 0)),
            _const_spec((N_META, CONV_CH)),
            _const_spec((CONV_W * V7X_SUBLANES, CONV_CH)),
            _const_spec((1, CONV_CH)), _const_spec((1, CONV_CH)), _const_spec((1, CONV_CH)),
        ],
        out_specs=pl.BlockSpec((CONV_TILE, CONV_CH), lambda b, i: (b * nt + i, 0)),
        out_shape=jax.ShapeDtypeStruct((batch * seq, CONV_CH), BF16),
        scratch_shapes=[pltpu.VMEM((CONV_HALO + CONV_TILE + V7X_SUBLANES, CONV_CH), F32),
                        pltpu.VMEM((V7X_SUBLANES - 1, CONV_HALO + CONV_TILE, CONV_CH), F32),
                        pltpu.VMEM((CONV_TILE, CONV_CH), F32)],
        compiler_params=pltpu.CompilerParams(
            dimension_semantics=("arbitrary", "arbitrary"), vmem_limit_bytes=V7X_VMEM_LIMIT),
        name="conv",
    )(glu, glu, glu_meta, w, cb, lng, lnb)


def _out_proj_kernel(attn_ref, conv_ref, x_ref, lng_ref, lnb_ref, wo_ref, g1_ref, b1_ref,
                     wrh_ref, wrl_ref, br_ref,
                     h1_ref, idx_ref, rank_ref, gate_ref, cnt_ref, carry_ref):
    step = pl.program_id(0)
    tm = x_ref.shape[0] // OUT_CHAINS

    @pl.when(step == 0)
    def _():
        carry_ref[...] = jnp.zeros_like(carry_ref)

    counts = carry_ref[...]
    for ch in range(OUT_CHAINS):
        counts = _route_chain(pl.ds(ch * tm, tm), ch * tm * TOK_ROWS, counts,
                              attn_ref, conv_ref, x_ref, lng_ref, lnb_ref, wo_ref, g1_ref, b1_ref,
                              wrh_ref, wrl_ref, br_ref, h1_ref, idx_ref, rank_ref, gate_ref)
    carry_ref[...] = counts
    cnt_ref[...] = counts.astype(jnp.int32)


def _route_chain(rows, h1_row0, counts, attn_ref, conv_ref, x_ref, lng_ref, lnb_ref, wo_ref, g1_ref, b1_ref,
                 wrh_ref, wrl_ref, br_ref, h1_ref, idx_ref, rank_ref, gate_ref):
    tm = rows.size
    h0 = _layer_norm(x_ref[rows, :], lng_ref[...], lnb_ref[...])
    mix = _dot(attn_ref[rows, :], wo_ref[0:MLA_WIDTH, :]) + _dot(conv_ref[rows, :], wo_ref[MLA_WIDTH:, :])
    h1 = _layer_norm(DEEPNORM_ALPHA * h0 + mix, g1_ref[...], b1_ref[...])
    _store_token_major(h1_ref, h1_row0, h1)

    hi = h1.astype(BF16)
    lo = (h1 - hi.astype(F32)).astype(BF16)
    logits = (_dot(hi, wrh_ref[...]) + (_dot(hi, wrl_ref[...]) + _dot(lo, wrh_ref[...]))) + br_ref[...]

    lane = lax.broadcasted_iota(jnp.int32, (tm, N_EXPERTS), 1)
    work = logits
    vals, idxs = [], []
    for _ in range(TOP_K):
        mx = jnp.max(work, axis=1, keepdims=True)
        ix = jnp.min(jnp.where(work == mx, lane, N_EXPERTS), axis=1, keepdims=True)
        vals.append(mx)
        idxs.append(ix)
        work = jnp.where(lane == ix, -jnp.inf, work)
    exps = [jnp.exp(v - vals[0]) for v in vals]
    denom = exps[0] + exps[1] + exps[2] + exps[3]

    onehots = [(lane == ix) for ix in idxs]
    chosen = (onehots[0] | onehots[1] | onehots[2] | onehots[3])
    chosen_f = jnp.where(chosen, 1.0, 0.0)
    r = lax.broadcasted_iota(jnp.int32, (tm, tm), 0)
    c = lax.broadcasted_iota(jnp.int32, (tm, tm), 1)
    lower = jnp.where(c < r, 1.0, 0.0).astype(BF16)
    before = _dot(lower, chosen_f.astype(BF16)) + counts

    out_lane = lax.broadcasted_iota(jnp.int32, (tm, V7X_LANES), 1)
    idx_out = jnp.zeros((tm, V7X_LANES), jnp.int32)
    rank_out = jnp.zeros((tm, V7X_LANES), jnp.int32)
    gate_out = jnp.zeros((tm, V7X_LANES), F32)
    for k in range(TOP_K):
        rank_k = jnp.sum(jnp.where(onehots[k], before, 0.0), axis=1, keepdims=True).astype(jnp.int32)
        idx_out = jnp.where(out_lane == k, idxs[k], idx_out)
        rank_out = jnp.where(out_lane == k, rank_k, rank_out)
        gate_out = jnp.where(out_lane == k, exps[k] / denom, gate_out)
    idx_ref[rows, :] = idx_out
    rank_ref[rows, :] = rank_out
    gate_ref[rows, :] = gate_out
    return counts + jnp.sum(chosen_f, axis=0, keepdims=True)


def _out_proj(attn, conv, x2d, lng, lnb, wo, g1, b1, wrh, wrl, br):
    rows = x2d.shape[0]
    tm = OUT_TILE
    row = lambda i: (i, 0)
    return pl.pallas_call(
        _out_proj_kernel,
        grid=(rows // tm,),
        in_specs=[
            pl.BlockSpec((tm, MLA_WIDTH), row), pl.BlockSpec((tm, CONV_CH), row),
            pl.BlockSpec((tm, D_MODEL), row),
            _const_spec((1, D_MODEL)), _const_spec((1, D_MODEL)),
            _const_spec((D_MODEL, D_MODEL)),
            _const_spec((1, D_MODEL)), _const_spec((1, D_MODEL)),
            _const_spec((D_MODEL, N_EXPERTS)), _const_spec((D_MODEL, N_EXPERTS)),
            _const_spec((1, N_EXPERTS)),
        ],
        out_specs=[
            pl.BlockSpec((tm * TOK_ROWS, V7X_LANES), row),
            pl.BlockSpec((tm, V7X_LANES), row), pl.BlockSpec((tm, V7X_LANES), row),
            pl.BlockSpec((tm, V7X_LANES), row),
            _const_spec((1, N_EXPERTS)),
        ],
        out_shape=[
            jax.ShapeDtypeStruct((rows * TOK_ROWS, V7X_LANES), F32),
            jax.ShapeDtypeStruct((rows, V7X_LANES), jnp.int32),
            jax.ShapeDtypeStruct((rows, V7X_LANES), jnp.int32),
            jax.ShapeDtypeStruct((rows, V7X_LANES), F32),
            jax.ShapeDtypeStruct((1, N_EXPERTS), jnp.int32),
        ],
        scratch_shapes=[pltpu.VMEM((1, N_EXPERTS), F32)],
        compiler_params=pltpu.CompilerParams(
            dimension_semantics=("arbitrary",), vmem_limit_bytes=V7X_VMEM_LIMIT),
        name="out_proj_router",
    )(attn, conv, x2d, lng, lnb, wo, g1, b1, wrh, wrl, br)


def _token_copy(src_ref, src_tok, dst_ref, dst_tok, sem, n_tok=1):
    rows = n_tok * TOK_ROWS
    src = src_ref.at[pl.ds(pl.multiple_of(src_tok * TOK_ROWS, TOK_ROWS), rows), :]
    dst = dst_ref.at[pl.ds(pl.multiple_of(dst_tok * TOK_ROWS, TOK_ROWS), rows), :]
    return pltpu.make_async_copy(src, dst, sem)


def _dispatch_kernel(dest_ref, padrow_ref, npad_ref, h1_ref, xs_ref, zero_ref, sem, zsem):
    step = pl.program_id(0)
    base = step * (DISPATCH_TILE * TOP_K)

    def copies(t):
        return [_token_copy(h1_ref, t, xs_ref, dest_ref[base + t * TOP_K + k], sem) for k in range(TOP_K)]

    def start(t, _):
        for c in copies(t):
            c.start()
        return 0

    def wait(t, _):
        for c in copies(t):
            c.wait()
        return 0

    lax.fori_loop(0, DISPATCH_TILE, start, 0)

    @pl.when(step == 0)
    def _():
        zero_ref[...] = jnp.zeros_like(zero_ref)
        n = npad_ref[0]
        tail_start = npad_ref[1]
        n_tail = (xs_ref.shape[0] // TOK_ROWS - tail_start) // SEG_ALIGN

        def zstart(p, _):
            _token_copy(zero_ref, 0, xs_ref, padrow_ref[p], zsem).start()
            return 0

        def zwait(p, _):
            _token_copy(zero_ref, 0, xs_ref, padrow_ref[p], zsem).wait()
            return 0

        def tstart(b, _):
            _token_copy(zero_ref, 0, xs_ref, tail_start + b * SEG_ALIGN, zsem, SEG_ALIGN).start()
            return 0

        def twait(b, _):
            _token_copy(zero_ref, 0, xs_ref, tail_start + b * SEG_ALIGN, zsem, SEG_ALIGN).wait()
            return 0

        lax.fori_loop(0, n, zstart, 0)
        lax.fori_loop(0, n, zwait, 0)
        lax.fori_loop(0, n_tail, tstart, 0)
        lax.fori_loop(0, n_tail, twait, 0)

    lax.fori_loop(0, DISPATCH_TILE, wait, 0)


def _dispatch(dest_flat, pad_rows, n_pad, h1t, n_rows):
    tokens = h1t.shape[0] // TOK_ROWS
    return pl.pallas_call(
        _dispatch_kernel,
        grid_spec=pltpu.PrefetchScalarGridSpec(
            num_scalar_prefetch=3,
            grid=(tokens // DISPATCH_TILE,),
            in_specs=[pl.BlockSpec((DISPATCH_TILE * TOK_ROWS, V7X_LANES), lambda i, *_: (i, 0))],
            out_specs=pl.BlockSpec(memory_space=pl.ANY),
            scratch_shapes=[pltpu.VMEM((SEG_ALIGN * TOK_ROWS, V7X_LANES), F32),
                            pltpu.SemaphoreType.DMA, pltpu.SemaphoreType.DMA],
        ),
        out_shape=jax.ShapeDtypeStruct((n_rows * TOK_ROWS, V7X_LANES), F32),
        compiler_params=pltpu.CompilerParams(
            dimension_semantics=("arbitrary",), has_side_effects=True),
        name="dispatch",
    )(dest_flat, pad_rows, n_pad, h1t)


N_FF = D_FF // FF_TILE
SUB = SEG_ALIGN
N_SUB = SUPER_ROWS // SUB
PREFETCH_SLOTS = -(-N_SUB // N_FF)
OUT_SLOTS = MM_ROWS // SUB


def _experts_kernel(st_e_ref, st_start_ref, st_rows_ref, n_used_ref,
                    xs_ref, w1g_ref, w1u_ref, b1g_ref, b1u_ref, w2_ref, b2_ref,
                    ys_ref,
                    xb_ref, acc_ref, wg_ref, wu_ref, wd_ref, xstage_ref, ostage_ref, xsem, osem):
    s = pl.program_id(0)
    j = pl.program_id(1)
    parity = s % 2
    rows = st_rows_ref[s]
    start = st_start_ref[s]
    n_blk = rows // SUB
    next_start = st_start_ref[s + 1]
    next_blk = st_rows_ref[s + 1] // SUB

    def x_copy(tok0, slot):
        return _token_copy(xs_ref, tok0, xstage_ref.at[slot], 0, xsem.at[slot], SUB)

    def y_copy(slot, tok0):
        return _token_copy(ostage_ref.at[slot], 0, ys_ref, tok0, osem.at[slot], SUB)

    def convert(slot, par, blk):
        off = pl.multiple_of(blk * SUB, SUB)
        for c in range(TOK_ROWS):
            xb_ref[par, pl.ds(off, SUB), c * V7X_LANES:(c + 1) * V7X_LANES] = (
                _load_token_major(xstage_ref.at[slot], 0, SUB, c).astype(BF16))

    @pl.when((s == 0) & (j == 0))
    def _():
        ostage_ref[0] = jnp.zeros(ostage_ref.shape[1:], F32)
        tail_start = n_used_ref[1]
        n_tail = (ys_ref.shape[0] // TOK_ROWS - tail_start) // SUB

        def tstart(b, _):
            y_copy(0, tail_start + b * SUB).start()
            return 0

        def twait(b, _):
            y_copy(0, tail_start + b * SUB).wait()
            return 0

        lax.fori_loop(0, n_tail, tstart, 0)
        lax.fori_loop(0, n_tail, twait, 0)

        def first(b, _):
            x_copy(start + b * SUB, 0).start()
            x_copy(start + b * SUB, 0).wait()
            convert(0, 0, b)
            return 0

        lax.fori_loop(0, n_blk, first, 0)

    for p in range(PREFETCH_SLOTS):
        @pl.when(j * PREFETCH_SLOTS + p < next_blk)
        def _(p=p):
            x_copy(next_start + (j * PREFETCH_SLOTS + p) * SUB, p).start()

    def compute(last):
        wg_ref[...] = w1g_ref[0].astype(BF16)
        wu_ref[...] = w1u_ref[0].astype(BF16)
        wd_ref[...] = w2_ref[0].astype(BF16)
        bg = b1g_ref[0]
        bu = b1u_ref[0]
        b2 = b2_ref[0]

        def chunk(row0, m):
            off = pl.multiple_of(row0, SUB)
            xb = xb_ref[parity, pl.ds(off, m), :]
            g = _dot(xb, wg_ref[...]) + bg
            u = _dot(xb, wu_ref[...]) + bu
            g = jnp.minimum(g, SWIGLU_LIMIT)
            u = jnp.clip(u, -SWIGLU_LIMIT, SWIGLU_LIMIT)
            act = g * jax.nn.sigmoid(SWIGLU_ALPHA * g) * (u + 1.0)
            y = _dot(act.astype(BF16), wd_ref[...])
            if not last:
                acc_ref[pl.ds(off, m), :] += y
                return
            y = acc_ref[pl.ds(off, m), :] + y + b2
            for i in range(m // SUB):
                blk = row0 // SUB + i
                slot = blk % OUT_SLOTS

                @pl.when(blk >= OUT_SLOTS)
                def _(blk=blk, slot=slot):
                    y_copy(slot, start + (blk - OUT_SLOTS) * SUB).wait()

                _store_token_major(ostage_ref.at[slot], 0, y[i * SUB:(i + 1) * SUB, :])
                y_copy(slot, start + blk * SUB).start()

        n_big = rows // MM_ROWS

        def big(r, _):
            chunk(r * MM_ROWS, MM_ROWS)
            return 0

        lax.fori_loop(0, n_big, big, 0)
        done = n_big * MM_ROWS
        m = MM_ROWS // 2
        while m >= SUB:
            take = ((rows - done) & m) != 0

            @pl.when(take)
            def _(done=done, m=m):
                chunk(done, m)

            done = done + jnp.where(take, m, 0)
            m //= 2

    @pl.when((j == 0) & (rows > 0))
    def _():
        def zero(b, _):
            acc_ref[pl.ds(pl.multiple_of(b * SUB, SUB), SUB), :] = jnp.zeros((SUB, D_MODEL), F32)
            return 0

        lax.fori_loop(0, n_blk, zero, 0)

    @pl.when((j < N_FF - 1) & (rows > 0))
    def _():
        compute(False)

    @pl.when((j == N_FF - 1) & (rows > 0))
    def _():
        compute(True)

    for p in range(PREFETCH_SLOTS):
        @pl.when(j * PREFETCH_SLOTS + p < next_blk)
        def _(p=p):
            blk = j * PREFETCH_SLOTS + p
            x_copy(next_start + blk * SUB, p).wait()
            convert(p, 1 - parity, blk)

    def drain(tok0, blocks):
        for slot in range(OUT_SLOTS):
            @pl.when(slot < blocks)
            def _(slot=slot):
                y_copy(slot, tok0 + _last_block_on_slot(blocks, slot) * SUB).wait()

    @pl.when((j == 0) & (s > 0))
    def _():
        prev = jnp.maximum(s - 1, 0)
        drain(st_start_ref[prev], st_rows_ref[prev] // SUB)

    @pl.when((j == N_FF - 1) & (s == pl.num_programs(0) - 1))
    def _():
        drain(start, n_blk)


def _last_block_on_slot(n_blk, slot):
    return slot + OUT_SLOTS * ((n_blk - 1 - slot) // OUT_SLOTS)


def _experts(st_e, st_start, st_rows, n_used, xs, w1, b1, w2, b2, n_super):
    n_rows = xs.shape[0] // TOK_ROWS

    def ff(s, j, n_used_ref):
        return jnp.where(s < n_used_ref[0], j, N_FF - 1)

    w1g_map = lambda s, j, e, st, rw, nu: (e[s], 0, ff(s, j, nu))
    w1u_map = lambda s, j, e, st, rw, nu: (e[s], 0, N_FF + ff(s, j, nu))
    w2_map = lambda s, j, e, st, rw, nu: (e[s], ff(s, j, nu), 0)
    b2_map = lambda s, j, e, st, rw, nu: (e[s], 0, 0)
    return pl.pallas_call(
        _experts_kernel,
        grid_spec=pltpu.PrefetchScalarGridSpec(
            num_scalar_prefetch=4,
            grid=(n_super, N_FF),
            in_specs=[
                pl.BlockSpec(memory_space=pl.ANY),
                pl.BlockSpec((1, D_MODEL, FF_TILE), w1g_map),
                pl.BlockSpec((1, D_MODEL, FF_TILE), w1u_map),
                pl.BlockSpec((1, 1, FF_TILE), w1g_map),
                pl.BlockSpec((1, 1, FF_TILE), w1u_map),
                pl.BlockSpec((1, FF_TILE, D_MODEL), w2_map),
                pl.BlockSpec((1, 1, D_MODEL), b2_map),
            ],
            out_specs=pl.BlockSpec(memory_space=pl.ANY),
            scratch_shapes=[
                pltpu.VMEM((2, SUPER_ROWS, D_MODEL), BF16),
                pltpu.VMEM((SUPER_ROWS, D_MODEL), F32),
                pltpu.VMEM((D_MODEL, FF_TILE), BF16),
                pltpu.VMEM((D_MODEL, FF_TILE), BF16),
                pltpu.VMEM((FF_TILE, D_MODEL), BF16),
                pltpu.VMEM((PREFETCH_SLOTS, SUB * TOK_ROWS, V7X_LANES), F32),
                pltpu.VMEM((OUT_SLOTS, SUB * TOK_ROWS, V7X_LANES), F32),
                pltpu.SemaphoreType.DMA((PREFETCH_SLOTS,)),
                pltpu.SemaphoreType.DMA((OUT_SLOTS,)),
            ],
        ),
        out_shape=jax.ShapeDtypeStruct((n_rows * TOK_ROWS, V7X_LANES), F32),
        compiler_params=pltpu.CompilerParams(
            dimension_semantics=("arbitrary", "arbitrary"), vmem_limit_bytes=V7X_VMEM_LIMIT,
            has_side_effects=True),
        name="experts",
    )(st_e, st_start, st_rows, n_used, xs, w1, w1, b1, b1, w2, b2)


def _combine_kernel(dest_ref, ys_ref, gate_ref, h1_ref, g2_ref, b2_ref, o_ref, buf_ref, sem):
    step = pl.program_id(0)
    n_steps = pl.num_programs(0)
    tm = COMBINE_TILE

    def copies(at_step, slot, t):
        base = at_step * (tm * TOP_K)
        return [_token_copy(ys_ref, dest_ref[base + t * TOP_K + k], buf_ref.at[slot, k], t, sem.at[slot])
                for k in range(TOP_K)]

    def gather(at_step, slot):
        def start(t, _):
            for c in copies(at_step, slot, t):
                c.start()
            return 0

        lax.fori_loop(0, tm, start, 0)

    @pl.when(step == 0)
    def _():
        gather(0, 0)

    @pl.when(step + 1 < n_steps)
    def _():
        gather(step + 1, (step + 1) % 2)

    slot = step % 2

    def wait(t, _):
        for c in copies(step, slot, t):
            c.wait()
        return 0

    lax.fori_loop(0, tm, wait, 0)

    gates = gate_ref[...]
    for c in range(TOK_ROWS):
        z = DEEPNORM_ALPHA * _load_token_major(h1_ref, 0, tm, c)
        for k in range(TOP_K):
            z = z + _load_token_major(buf_ref.at[slot, k], 0, tm, c) * gates[:, k:k + 1]
        o_ref[:, c * V7X_LANES:(c + 1) * V7X_LANES] = z
    o_ref[...] = _layer_norm(o_ref[...], g2_ref[...], b2_ref[...])


def _combine(dest_flat, ys, gates, h1t, g2, b2):
    tokens = h1t.shape[0] // TOK_ROWS
    tm = COMBINE_TILE
    row = lambda i, *_: (i, 0)
    return pl.pallas_call(
        _combine_kernel,
        grid_spec=pltpu.PrefetchScalarGridSpec(
            num_scalar_prefetch=1,
            grid=(tokens // tm,),
            in_specs=[
                pl.BlockSpec(memory_space=pl.ANY),
                pl.BlockSpec((tm, V7X_LANES), row),
                pl.BlockSpec((tm * TOK_ROWS, V7X_LANES), row),
                pl.BlockSpec((1, D_MODEL), lambda i, *_: (0, 0)),
                pl.BlockSpec((1, D_MODEL), lambda i, *_: (0, 0)),
            ],
            out_specs=pl.BlockSpec((tm, D_MODEL), row),
            scratch_shapes=[pltpu.VMEM((2, TOP_K, tm * TOK_ROWS, V7X_LANES), F32),
                            pltpu.SemaphoreType.DMA((2,))],
        ),
        out_shape=jax.ShapeDtypeStruct((tokens, D_MODEL), F32),
        compiler_params=pltpu.CompilerParams(
            dimension_semantics=("arbitrary",), vmem_limit_bytes=V7X_VMEM_LIMIT),
        name="combine",
    )(dest_flat, ys, gates, h1t, g2, b2)


def _rotate_half_cols(w):
    half = QK_ROPE // 2
    return jnp.concatenate([-w[..., half:], w[..., :half]], axis=-1)


def _rope_table(length):
    inv_freq = 1.0 / (ROPE_THETA ** (jnp.arange(0, QK_ROPE, 2, dtype=F32) / QK_ROPE))
    freqs = jnp.arange(length, dtype=F32)[:, None] * inv_freq[None, :]
    emb = jnp.concatenate([freqs, freqs], axis=-1)
    return jnp.concatenate([jnp.cos(emb), jnp.sin(emb)], axis=-1)


def _routing_plan(idx, rank, counts, n_super):
    experts = jnp.arange(N_EXPERTS, dtype=jnp.int32)

    def lookup(table, i):
        return jnp.sum(jnp.where(i[..., None] == experts, table, 0), axis=-1)

    def bucket(cum, i):
        return jnp.minimum(jnp.sum((cum <= i[..., None]).astype(jnp.int32), axis=-1), N_EXPERTS - 1)

    counts = counts.astype(jnp.int32)
    padded = (counts + SEG_ALIGN - 1) // SEG_ALIGN * SEG_ALIGN
    pad_end = jnp.cumsum(padded)
    pad_start = pad_end - padded
    dest = (lookup(pad_start, idx) + rank).reshape(-1).astype(jnp.int32)

    n_padmax = N_EXPERTS * SEG_ALIGN
    padcnt = padded - counts
    padcum = jnp.cumsum(padcnt)
    p = jnp.arange(n_padmax, dtype=jnp.int32)
    pe = bucket(padcum, p)
    pad_rows = lookup(pad_start + counts - (padcum - padcnt), pe) + p
    n_pad = jnp.stack([padcum[-1], pad_end[-1]]).astype(jnp.int32)
    pad_rows = jnp.where(p < n_pad[0], pad_rows, 0).astype(jnp.int32)

    n_st = (padded + SUPER_ROWS - 1) // SUPER_ROWS
    st_cum = jnp.cumsum(n_st)
    n_used = jnp.stack([st_cum[-1], pad_end[-1]]).astype(jnp.int32)
    s = jnp.arange(n_super + 1, dtype=jnp.int32)
    s_eff = jnp.minimum(s, n_used[0] - 1)
    se = bucket(st_cum, s_eff).astype(jnp.int32)
    local = s_eff - lookup(st_cum - n_st, se)
    st_start = (lookup(pad_start, se) + local * SUPER_ROWS).astype(jnp.int32)
    st_rows = jnp.clip(lookup(padded, se) - local * SUPER_ROWS, 0, SUPER_ROWS)
    st_rows = jnp.where(s < n_used[0], st_rows, 0).astype(jnp.int32)
    return dest, pad_rows, n_pad, se, st_start, st_rows, n_used


def kernel(x, meta_tokens, ln_in_g, ln_in_b, w_in, q_norm_g, w_uq, kv_norm_g, w_uk, w_uv, conv_dw_w,
           conv_dw_b, conv_ln_g, conv_ln_b, w_out, ln1_g, ln1_b, w_router, b_router, w_mlp1, b_mlp1,
           w_mlp2, b_mlp2, ln2_g, ln2_b):
    batch, seq, _ = x.shape
    tokens = batch * seq
    row2 = lambda a: a.reshape(1, -1)

    wi = w_in[0]
    s_kpe = Q_LORA + KV_LORA
    s_conv = s_kpe + QK_ROPE
    kpe_w = wi[:, s_kpe:s_conv]
    w_proj = (wi[:, :s_kpe].astype(BF16),
              jnp.concatenate([kpe_w, _rotate_half_cols(kpe_w)], axis=1).astype(BF16),
              wi[:, s_conv:s_conv + CONV_CH].astype(BF16),
              wi[:, s_conv + CONV_CH:].astype(BF16))
    wq3 = w_uq[0].reshape(Q_LORA, N_HEADS, QK_DIM)
    wq_nope = wq3[:, :, :QK_NOPE].reshape(Q_LORA, N_HEADS * QK_NOPE)
    wq_pe = wq3[:, :, QK_NOPE:]
    wq_pr = jnp.concatenate([wq_pe, _rotate_half_cols(wq_pe)], axis=-1).reshape(Q_LORA, N_HEADS * 2 * QK_ROPE)
    wq = jnp.concatenate([wq_nope, wq_pr], axis=1).astype(BF16)
    wuk = w_uk[0].astype(BF16)
    wuv = w_uv[0].astype(BF16)
    wo = w_out[0].astype(BF16)
    wr = w_router[0]
    wr_hi = wr.astype(BF16)
    wr_lo = (wr - wr_hi.astype(F32)).astype(BF16)
    cs = _rope_table(N_META + seq)
    conv_w = jnp.repeat(conv_dw_w[0], V7X_SUBLANES, axis=0)

    x2d = x.reshape(tokens, D_MODEL)
    proj_args = (row2(ln_in_g), row2(ln_in_b), *w_proj, row2(q_norm_g[0]), row2(kv_norm_g[0]), wq, wuk, wuv)

    _, k_meta, v_meta, glu_meta = _in_proj(meta_tokens, *proj_args, cs[:N_META], N_META)
    q, k, v, glu = _in_proj(x2d, *proj_args, cs[N_META:], ROW_TILE)
    k_meta = jnp.pad(k_meta, ((0, 0), (0, 0), (0, V7X_LANES - N_META)))
    v_meta = jnp.pad(v_meta, ((0, 0), (0, V7X_LANES - N_META), (0, 0)))
    attn = _attention(q, k, v, k_meta, v_meta, batch, seq)
    conv = _conv(glu, glu_meta, conv_w, row2(conv_dw_b[0]), row2(conv_ln_g[0]), row2(conv_ln_b[0]), batch, seq)

    h1t, idx, rank, gates, counts = _out_proj(
        attn, conv, x2d, row2(ln_in_g), row2(ln_in_b), wo, row2(ln1_g[0]), row2(ln1_b[0]),
        wr_hi, wr_lo, row2(b_router[0]))

    n_assign = tokens * TOP_K
    n_rows = n_assign + N_EXPERTS * SEG_ALIGN
    n_super = N_EXPERTS + -(-n_assign // SUPER_ROWS)
    dest, pad_rows, n_pad, st_e, st_start, st_rows, n_used = _routing_plan(
        idx[:, :TOP_K], rank[:, :TOP_K], counts[0], n_super)

    xs = _dispatch(dest, pad_rows, n_pad, h1t, n_rows)
    ys = _experts(st_e, st_start, st_rows, n_used, xs, w_mlp1[0], b_mlp1[0].reshape(N_EXPERTS, 1, 2 * D_FF),
                  w_mlp2[0], b_mlp2[0].reshape(N_EXPERTS, 1, D_MODEL), n_super)
    out = _combine(dest, ys, gates, h1t, row2(ln2_g[0]), row2(ln2_b[0]))
    return out.reshape(batch, seq, D_MODEL)
```

```python
import functools
import math

import jax
import jax.numpy as jnp
from jax import lax
from jax.experimental import pallas as pl
from jax.experimental.pallas import tpu as pltpu

D_MODEL = 2048
N_META = 16
N_HEADS = 8
QK_NOPE = 128
QK_ROPE = 64
QK_DIM = QK_NOPE + QK_ROPE
V_DIM = 128
Q_LORA = 768
KV_LORA = 512
ROPE_THETA = 10000.0
MLA_WIDTH = N_HEADS * V_DIM
CONV_CH = 1024
CONV_W = 31
N_EXPERTS = 32
TOP_K = 4
D_FF = 2048
SWIGLU_LIMIT = 7.0
SWIGLU_ALPHA = 1.702
DEEPNORM_ALPHA = 2.0 ** 0.25
LN_EPS = 1e-5
RMS_EPS = 1e-6

V7X_LANES = 128
V7X_SUBLANES = 8
V7X_VMEM_LIMIT = 56 * 1024 * 1024

ROW_TILE = 256
OUT_TILE = 256
OUT_CHAINS = 1
ATT_TILE = 512
ATT_CHAINS = 2
CONV_TILE = 256
CONV_HALO = 32
CONV_ROWS = 32
CONV_LANES = 256
DISPATCH_TILE = 256
COMBINE_TILE = 128
SEG_ALIGN = 128
SUPER_ROWS = 1536
FF_TILE = 256
MM_ROWS = 512

F32 = jnp.float32
BF16 = jnp.bfloat16


def _dot(a, b):
    return jnp.dot(a, b, preferred_element_type=F32)


def _dot_nt(a, b):
    return lax.dot_general(a, b, (((1,), (1,)), ((), ())), preferred_element_type=F32)


def _layer_norm(x, g, b):
    mu = jnp.mean(x, axis=-1, keepdims=True)
    xc = x - mu
    var = jnp.mean(xc * xc, axis=-1, keepdims=True)
    return xc * lax.rsqrt(var + LN_EPS) * g + b


def _rms_norm(x, g):
    ms = jnp.mean(x * x, axis=-1, keepdims=True)
    return x * lax.rsqrt(ms + RMS_EPS) * g


def _const_spec(shape):
    zeros = (0,) * len(shape)
    return pl.BlockSpec(shape, lambda *_: zeros)


TOK_ROWS = D_MODEL // V7X_LANES


def _load_token_major(ref, row0, n_tok, j):
    return ref[pl.ds(row0 + j, n_tok, stride=TOK_ROWS), :]


def _store_token_major(ref, row0, x):
    n_tok = x.shape[0]
    for j in range(TOK_ROWS):
        ref[pl.ds(row0 + j, n_tok, stride=TOK_ROWS), :] = x[:, j * V7X_LANES:(j + 1) * V7X_LANES]


def _in_proj_kernel(x_ref, lng_ref, lnb_ref, wc_ref, wkpe_ref, wa_ref, wg_ref, qg_ref, kvg_ref,
                    wq_ref, wuk_ref, wuv_ref, cs_ref,
                    q_ref, k_ref, v_ref, glu_ref):
    h0 = _layer_norm(x_ref[...], lng_ref[...], lnb_ref[...])
    hb = h0.astype(BF16)
    cs = cs_ref[...]

    def rope(t128):
        t = t128 * cs
        return t + pltpu.roll(t, QK_ROPE, axis=1)

    cq = _dot(hb, wc_ref[:, 0:Q_LORA])
    cqn = _rms_norm(cq, qg_ref[...]).astype(BF16)
    ckv = _dot(hb, wc_ref[:, Q_LORA:Q_LORA + KV_LORA])
    ckvn = _rms_norm(ckv, kvg_ref[...]).astype(BF16)
    kpe = rope(_dot(hb, wkpe_ref[...]))

    a = _dot(hb, wa_ref[...])
    g = _dot(hb, wg_ref[...])
    glu_ref[...] = a * jax.nn.sigmoid(g)

    knope = _dot(ckvn, wuk_ref[...])
    v = _dot(ckvn, wuv_ref[...])
    kpe_t = kpe.T[0:QK_ROPE, :].astype(BF16)
    for h in range(N_HEADS):
        k_ref[h, 0:QK_NOPE, :] = knope[:, h * QK_NOPE:(h + 1) * QK_NOPE].T.astype(BF16)
        k_ref[h, QK_NOPE:QK_DIM, :] = kpe_t
        v_ref[h] = v[:, h * V_DIM:(h + 1) * V_DIM].astype(BF16)

    qn = _dot(cqn, wq_ref[:, 0:N_HEADS * QK_NOPE])
    qp = _dot(cqn, wq_ref[:, N_HEADS * QK_NOPE:])
    for h in range(N_HEADS):
        q_ref[h, :, 0:QK_NOPE] = qn[:, h * QK_NOPE:(h + 1) * QK_NOPE].astype(BF16)
        q_ref[h, :, QK_NOPE:QK_DIM] = rope(qp[:, h * V7X_LANES:(h + 1) * V7X_LANES])[:, :QK_ROPE].astype(BF16)


def _in_proj(x2d, lng, lnb, wc, wkpe, wa, wg, qg, kvg, wq, wuk, wuv, cs, tm):
    rows = x2d.shape[0]
    n_cs = cs.shape[0] // tm
    row = lambda i: (i, 0)
    head_row = lambda i: (0, i, 0)
    return pl.pallas_call(
        _in_proj_kernel,
        grid=(rows // tm,),
        in_specs=[
            pl.BlockSpec((tm, D_MODEL), row),
            _const_spec((1, D_MODEL)), _const_spec((1, D_MODEL)),
            _const_spec((D_MODEL, Q_LORA + KV_LORA)), _const_spec((D_MODEL, 2 * QK_ROPE)),
            _const_spec((D_MODEL, CONV_CH)), _const_spec((D_MODEL, CONV_CH)),
            _const_spec((1, Q_LORA)), _const_spec((1, KV_LORA)),
            _const_spec((Q_LORA, 2 * N_HEADS * QK_NOPE)),
            _const_spec((KV_LORA, N_HEADS * QK_NOPE)), _const_spec((KV_LORA, MLA_WIDTH)),
            pl.BlockSpec((tm, V7X_LANES), lambda i: (i % n_cs, 0)),
        ],
        out_specs=[
            pl.BlockSpec((N_HEADS, tm, QK_DIM), head_row),
            pl.BlockSpec((N_HEADS, QK_DIM, tm), lambda i: (0, 0, i)),
            pl.BlockSpec((N_HEADS, tm, V_DIM), head_row),
            pl.BlockSpec((tm, CONV_CH), row),
        ],
        out_shape=[
            jax.ShapeDtypeStruct((N_HEADS, rows, QK_DIM), BF16),
            jax.ShapeDtypeStruct((N_HEADS, QK_DIM, rows), BF16),
            jax.ShapeDtypeStruct((N_HEADS, rows, V_DIM), BF16),
            jax.ShapeDtypeStruct((rows, CONV_CH), F32),
        ],
        compiler_params=pltpu.CompilerParams(
            dimension_semantics=("arbitrary",), vmem_limit_bytes=V7X_VMEM_LIMIT),
        name="in_proj",
    )(x2d, lng, lnb, wc, wkpe, wa, wg, qg, kvg, wq, wuk, wuv, cs)


def _attention_kernel(q_ref, k_ref, v_ref, km_ref, vm_ref, o_ref, *state):
    i = pl.program_id(2)
    c_exp = (1.0 / math.sqrt(QK_DIM)) * math.log2(math.e)
    chain_rows = ATT_TILE // ATT_CHAINS
    chains = range(ATT_CHAINS)
    m_refs, l_refs, acc_refs = (state[n * ATT_CHAINS:(n + 1) * ATT_CHAINS] for n in range(3))

    def lane_tiles(x):
        return [x[:, t * V7X_LANES:(t + 1) * V7X_LANES] for t in range(x.shape[1] // V7X_LANES)]

    def row_max(x):
        if x.shape[1] % V7X_LANES:
            return jnp.max(x, axis=1, keepdims=True)
        return jnp.max(functools.reduce(jnp.maximum, lane_tiles(x)), axis=1, keepdims=True)

    def lane_partial_sum(x):
        if x.shape[1] % V7X_LANES:
            lane = lax.broadcasted_iota(jnp.int32, (x.shape[0], V7X_LANES), 1)
            return jnp.where(lane == 0, jnp.sum(x, axis=1, keepdims=True), 0.0)
        return functools.reduce(jnp.add, lane_tiles(x))

    def update(h, s, vb, first):
        s_max = jnp.broadcast_to(row_max(s), (s.shape[0], V7X_LANES))
        if first:
            m_new = s_max
        else:
            m_old = m_refs[h][...]
            m_new = jnp.maximum(m_old, s_max)
            alpha = jnp.exp2(c_exp * (m_old - m_new))
        if s.shape[1] % V7X_LANES:
            p = jnp.exp2(c_exp * (s - m_new[:, :s.shape[1]]))
        else:
            p = jnp.concatenate([jnp.exp2(c_exp * (t - m_new)) for t in lane_tiles(s)], axis=1)
        p_sum = lane_partial_sum(p)
        pv = _dot(p.astype(BF16), vb)
        m_refs[h][...] = m_new
        if first:
            l_refs[h][...] = p_sum
            acc_refs[h][...] = pv
        else:
            l_refs[h][...] = alpha * l_refs[h][...] + p_sum
            acc_refs[h][...] = alpha * acc_refs[h][...] + pv

    def chain(x, h):
        return x[h * chain_rows:(h + 1) * chain_rows, :]

    q = q_ref[0]

    start = pl.multiple_of(i * ATT_TILE, ATT_TILE)
    s = _dot(q, jnp.concatenate([k_ref[0, :, pl.ds(start, ATT_TILE)], km_ref[0]], axis=1))
    r = lax.broadcasted_iota(jnp.int32, s.shape, 0)
    c = lax.broadcasted_iota(jnp.int32, s.shape, 1)
    last_visible = jnp.where(c >= ATT_TILE, ATT_TILE + N_META - 1, r)
    s = jnp.where(c <= last_visible, s, -1e30)
    for h in chains:
        cols = (h + 1) * chain_rows
        s_h = jnp.concatenate([chain(s, h)[:, :cols], chain(s, h)[:, ATT_TILE:]], axis=1)
        v_h = jnp.concatenate([v_ref[0, pl.ds(start, cols), :], vm_ref[0]], axis=0)
        update(h, s_h, v_h, True)

    def block(j, _):
        start = pl.multiple_of(j * ATT_TILE, ATT_TILE)
        s = _dot(q, k_ref[0, :, pl.ds(start, ATT_TILE)])
        vb = v_ref[0, pl.ds(start, ATT_TILE), :]
        for h in chains:
            update(h, chain(s, h), vb, False)
        return 0

    lax.fori_loop(0, i, block, 0)

    for h in chains:
        l = jnp.sum(l_refs[h][...], axis=1, keepdims=True)
        o_ref[pl.ds(h * chain_rows, chain_rows), :] = (acc_refs[h][...] / l).astype(BF16)


def _attention(q, k, v, km, vm, batch, seq):
    nq = seq // ATT_TILE
    return pl.pallas_call(
        _attention_kernel,
        grid=(batch, N_HEADS, nq),
        in_specs=[
            pl.BlockSpec((1, ATT_TILE, QK_DIM), lambda b, h, i: (h, b * nq + i, 0)),
            pl.BlockSpec((1, QK_DIM, seq), lambda b, h, i: (h, 0, b)),
            pl.BlockSpec((1, seq, V_DIM), lambda b, h, i: (h, b, 0)),
            pl.BlockSpec((1, QK_DIM, V7X_LANES), lambda b, h, i: (h, 0, 0)),
            pl.BlockSpec((1, V7X_LANES, V_DIM), lambda b, h, i: (h, 0, 0)),
        ],
        out_specs=pl.BlockSpec((ATT_TILE, V_DIM), lambda b, h, i: (b * nq + i, h)),
        out_shape=jax.ShapeDtypeStruct((batch * seq, MLA_WIDTH), BF16),
        scratch_shapes=[pltpu.VMEM((ATT_TILE // ATT_CHAINS, V7X_LANES), F32)] * (3 * ATT_CHAINS),
        compiler_params=pltpu.CompilerParams(
            dimension_semantics=("arbitrary", "arbitrary", "arbitrary"), vmem_limit_bytes=V7X_VMEM_LIMIT),
        name="attention",
    )(q, k, v, km, vm)


def _conv_kernel(cur_ref, prev_ref, meta_ref, w_ref, cb_ref, lng_ref, lnb_ref, o_ref,
                 win_ref, shift_ref, acc_ref):
    i = pl.program_id(1)

    @pl.when(i == 0)
    def _():
        win_ref[0:CONV_HALO - N_META, :] = jnp.zeros((CONV_HALO - N_META, CONV_CH), F32)
        win_ref[CONV_HALO - N_META:CONV_HALO, :] = meta_ref[...]

    @pl.when(i > 0)
    def _():
        win_ref[0:CONV_HALO, :] = prev_ref[...]

    win_ref[CONV_HALO:CONV_HALO + CONV_TILE, :] = cur_ref[...]
    win_ref[CONV_HALO + CONV_TILE:, :] = jnp.zeros((V7X_SUBLANES, CONV_CH), F32)

    base = CONV_HALO - (CONV_W - 1)
    win_rows = CONV_HALO + CONV_TILE
    for shift in range(1, V7X_SUBLANES):
        for r in range(0, win_rows, CONV_ROWS):
            shift_ref[shift - 1, pl.ds(r, CONV_ROWS), :] = win_ref[pl.ds(r + shift, CONV_ROWS), :]

    def window(shift, row, lanes):
        if shift == 0:
            return win_ref[pl.ds(row, CONV_ROWS), lanes]
        return shift_ref[shift - 1, pl.ds(row, CONV_ROWS), lanes]

    for rc in range(CONV_TILE // CONV_ROWS):
        r0 = rc * CONV_ROWS
        for c in range(CONV_CH // CONV_LANES):
            lanes = pl.ds(c * CONV_LANES, CONV_LANES)
            acc = jnp.zeros((CONV_ROWS, CONV_LANES), F32)
            for k in range(CONV_W):
                shift = (base + k) % V7X_SUBLANES
                w_k = jnp.concatenate([w_ref[k * V7X_SUBLANES:(k + 1) * V7X_SUBLANES, lanes]]
                                      * (CONV_ROWS // V7X_SUBLANES), axis=0)
                acc = acc + window(shift, r0 + base + k - shift, lanes) * w_k
            acc_ref[pl.ds(r0, CONV_ROWS), lanes] = acc

    y = _layer_norm(acc_ref[...] + cb_ref[...], lng_ref[...], lnb_ref[...])
    o_ref[...] = (y * jax.nn.sigmoid(y)).astype(BF16)


def _conv(glu, glu_meta, w, cb, lng, lnb, batch, seq):
    nt = seq // CONV_TILE
    per = CONV_TILE // CONV_HALO
    return pl.pallas_call(
        _conv_kernel,
        grid=(batch, nt),
        in_specs=[
            pl.BlockSpec((CONV_TILE, CONV_CH), lambda b, i: (b * nt + i, 0)),
            pl.BlockSpec((CONV_HALO, CONV_CH), lambda b, i: (jnp.maximum((b * nt + i) * per - 1, 0), 0)),
            _const_spec((N_META, CONV_CH)),
            _const_spec((CONV_W * V7X_SUBLANES, CONV_CH)),
            _const_spec((1, CONV_CH)), _const_spec((1, CONV_CH)), _const_spec((1, CONV_CH)),
        ],
        out_specs=pl.BlockSpec((CONV_TILE, CONV_CH), lambda b, i: (b * nt + i, 0)),
        out_shape=jax.ShapeDtypeStruct((batch * seq, CONV_CH), BF16),
        scratch_shapes=[pltpu.VMEM((CONV_HALO + CONV_TILE + V7X_SUBLANES, CONV_CH), F32),
                        pltpu.VMEM((V7X_SUBLANES - 1, CONV_HALO + CONV_TILE, CONV_CH), F32),
                        pltpu.VMEM((CONV_TILE, CONV_CH), F32)],
        compiler_params=pltpu.CompilerParams(
            dimension_semantics=("arbitrary", "arbitrary"), vmem_limit_bytes=V7X_VMEM_LIMIT),
        name="conv",
    )(glu, glu, glu_meta, w, cb, lng, lnb)


def _out_proj_kernel(attn_ref, conv_ref, x_ref, lng_ref, lnb_ref, wo_ref, g1_ref, b1_ref,
                     wrh_ref, wrl_ref, br_ref,
                     h1_ref, idx_ref, rank_ref, gate_ref, cnt_ref, carry_ref):
    step = pl.program_id(0)
    tm = x_ref.shape[0] // OUT_CHAINS

    @pl.when(step == 0)
    def _():
        carry_ref[...] = jnp.zeros_like(carry_ref)

    counts = carry_ref[...]
    for ch in range(OUT_CHAINS):
        counts = _route_chain(pl.ds(ch * tm, tm), ch * tm * TOK_ROWS, counts,
                              attn_ref, conv_ref, x_ref, lng_ref, lnb_ref, wo_ref, g1_ref, b1_ref,
                              wrh_ref, wrl_ref, br_ref, h1_ref, idx_ref, rank_ref, gate_ref)
    carry_ref[...] = counts
    cnt_ref[...] = counts.astype(jnp.int32)


def _route_chain(rows, h1_row0, counts, attn_ref, conv_ref, x_ref, lng_ref, lnb_ref, wo_ref, g1_ref, b1_ref,
                 wrh_ref, wrl_ref, br_ref, h1_ref, idx_ref, rank_ref, gate_ref):
    tm = rows.size
    h0 = _layer_norm(x_ref[rows, :], lng_ref[...], lnb_ref[...])
    mix = _dot(attn_ref[rows, :], wo_ref[0:MLA_WIDTH, :]) + _dot(conv_ref[rows, :], wo_ref[MLA_WIDTH:, :])
    h1 = _layer_norm(DEEPNORM_ALPHA * h0 + mix, g1_ref[...], b1_ref[...])
    _store_token_major(h1_ref, h1_row0, h1)

    hi = h1.astype(BF16)
    lo = (h1 - hi.astype(F32)).astype(BF16)
    logits = (_dot(hi, wrh_ref[...]) + (_dot(hi, wrl_ref[...]) + _dot(lo, wrh_ref[...]))) + br_ref[...]

    lane = lax.broadcasted_iota(jnp.int32, (tm, N_EXPERTS), 1)
    work = logits
    vals, idxs = [], []
    for _ in range(TOP_K):
        mx = jnp.max(work, axis=1, keepdims=True)
        ix = jnp.min(jnp.where(work == mx, lane, N_EXPERTS), axis=1, keepdims=True)
        vals.append(mx)
        idxs.append(ix)
        work = jnp.where(lane == ix, -jnp.inf, work)
    exps = [jnp.exp(v - vals[0]) for v in vals]
    denom = exps[0] + exps[1] + exps[2] + exps[3]

    onehots = [(lane == ix) for ix in idxs]
    chosen = (onehots[0] | onehots[1] | onehots[2] | onehots[3])
    chosen_f = jnp.where(chosen, 1.0, 0.0)
    r = lax.broadcasted_iota(jnp.int32, (tm, tm), 0)
    c = lax.broadcasted_iota(jnp.int32, (tm, tm), 1)
    lower = jnp.where(c < r, 1.0, 0.0).astype(BF16)
    before = _dot(lower, chosen_f.astype(BF16)) + counts

    out_lane = lax.broadcasted_iota(jnp.int32, (tm, V7X_LANES), 1)
    idx_out = jnp.zeros((tm, V7X_LANES), jnp.int32)
    rank_out = jnp.zeros((tm, V7X_LANES), jnp.int32)
    gate_out = jnp.zeros((tm, V7X_LANES), F32)
    for k in range(TOP_K):
        rank_k = jnp.sum(jnp.where(onehots[k], before, 0.0), axis=1, keepdims=True).astype(jnp.int32)
        idx_out = jnp.where(out_lane == k, idxs[k], idx_out)
        rank_out = jnp.where(out_lane == k, rank_k, rank_out)
        gate_out = jnp.where(out_lane == k, exps[k] / denom, gate_out)
    idx_ref[rows, :] = idx_out
    rank_ref[rows, :] = rank_out
    gate_ref[rows, :] = gate_out
    return counts + jnp.sum(chosen_f, axis=0, keepdims=True)


def _out_proj(attn, conv, x2d, lng, lnb, wo, g1, b1, wrh, wrl, br):
    rows = x2d.shape[0]
    tm = OUT_TILE
    row = lambda i: (i, 0)
    return pl.pallas_call(
        _out_proj_kernel,
        grid=(rows // tm,),
        in_specs=[
            pl.BlockSpec((tm, MLA_WIDTH), row), pl.BlockSpec((tm, CONV_CH), row),
            pl.BlockSpec((tm, D_MODEL), row),
            _const_spec((1, D_MODEL)), _const_spec((1, D_MODEL)),
            _const_spec((D_MODEL, D_MODEL)),
            _const_spec((1, D_MODEL)), _const_spec((1, D_MODEL)),
            _const_spec((D_MODEL, N_EXPERTS)), _const_spec((D_MODEL, N_EXPERTS)),
            _const_spec((1, N_EXPERTS)),
        ],
        out_specs=[
            pl.BlockSpec((tm * TOK_ROWS, V7X_LANES), row),
            pl.BlockSpec((tm, V7X_LANES), row), pl.BlockSpec((tm, V7X_LANES), row),
            pl.BlockSpec((tm, V7X_LANES), row),
            _const_spec((1, N_EXPERTS)),
        ],
        out_shape=[
            jax.ShapeDtypeStruct((rows * TOK_ROWS, V7X_LANES), F32),
            jax.ShapeDtypeStruct((rows, V7X_LANES), jnp.int32),
            jax.ShapeDtypeStruct((rows, V7X_LANES), jnp.int32),
            jax.ShapeDtypeStruct((rows, V7X_LANES), F32),
            jax.ShapeDtypeStruct((1, N_EXPERTS), jnp.int32),
        ],
        scratch_shapes=[pltpu.VMEM((1, N_EXPERTS), F32)],
        compiler_params=pltpu.CompilerParams(
            dimension_semantics=("arbitrary",), vmem_limit_bytes=V7X_VMEM_LIMIT),
        name="out_proj_router",
    )(attn, conv, x2d, lng, lnb, wo, g1, b1, wrh, wrl, br)


def _token_copy(src_ref, src_tok, dst_ref, dst_tok, sem, n_tok=1):
    rows = n_tok * TOK_ROWS
    src = src_ref.at[pl.ds(pl.multiple_of(src_tok * TOK_ROWS, TOK_ROWS), rows), :]
    dst = dst_ref.at[pl.ds(pl.multiple_of(dst_tok * TOK_ROWS, TOK_ROWS), rows), :]
    return pltpu.make_async_copy(src, dst, sem)


def _dispatch_kernel(dest_ref, padrow_ref, npad_ref, h1_ref, xs_ref, zero_ref, sem, zsem):
    step = pl.program_id(0)
    base = step * (DISPATCH_TILE * TOP_K)

    def copies(t):
        return [_token_copy(h1_ref, t, xs_ref, dest_ref[base + t * TOP_K + k], sem) for k in range(TOP_K)]

    def start(t, _):
        for c in copies(t):
            c.start()
        return 0

    def wait(t, _):
        for c in copies(t):
            c.wait()
        return 0

    lax.fori_loop(0, DISPATCH_TILE, start, 0)

    @pl.when(step == 0)
    def _():
        zero_ref[...] = jnp.zeros_like(zero_ref)
        n = npad_ref[0]
        tail_start = npad_ref[1]
        n_tail = (xs_ref.shape[0] // TOK_ROWS - tail_start) // SEG_ALIGN

        def zstart(p, _):
            _token_copy(zero_ref, 0, xs_ref, padrow_ref[p], zsem).start()
            return 0

        def zwait(p, _):
            _token_copy(zero_ref, 0, xs_ref, padrow_ref[p], zsem).wait()
            return 0

        def tstart(b, _):
            _token_copy(zero_ref, 0, xs_ref, tail_start + b * SEG_ALIGN, zsem, SEG_ALIGN).start()
            return 0

        def twait(b, _):
            _token_copy(zero_ref, 0, xs_ref, tail_start + b * SEG_ALIGN, zsem, SEG_ALIGN).wait()
            return 0

        lax.fori_loop(0, n, zstart, 0)
        lax.fori_loop(0, n, zwait, 0)
        lax.fori_loop(0, n_tail, tstart, 0)
        lax.fori_loop(0, n_tail, twait, 0)

    lax.fori_loop(0, DISPATCH_TILE, wait, 0)


def _dispatch(dest_flat, pad_rows, n_pad, h1t, n_rows):
    tokens = h1t.shape[0] // TOK_ROWS
    return pl.pallas_call(
        _dispatch_kernel,
        grid_spec=pltpu.PrefetchScalarGridSpec(
            num_scalar_prefetch=3,
            grid=(tokens // DISPATCH_TILE,),
            in_specs=[pl.BlockSpec((DISPATCH_TILE * TOK_ROWS, V7X_LANES), lambda i, *_: (i, 0))],
            out_specs=pl.BlockSpec(memory_space=pl.ANY),
            scratch_shapes=[pltpu.VMEM((SEG_ALIGN * TOK_ROWS, V7X_LANES), F32),
                            pltpu.SemaphoreType.DMA, pltpu.SemaphoreType.DMA],
        ),
        out_shape=jax.ShapeDtypeStruct((n_rows * TOK_ROWS, V7X_LANES), F32),
        compiler_params=pltpu.CompilerParams(
            dimension_semantics=("arbitrary",), has_side_effects=True),
        name="dispatch",
    )(dest_flat, pad_rows, n_pad, h1t)


N_FF = D_FF // FF_TILE
SUB = SEG_ALIGN
N_SUB = SUPER_ROWS // SUB
PREFETCH_SLOTS = -(-N_SUB // N_FF)
OUT_SLOTS = MM_ROWS // SUB


def _experts_kernel(st_e_ref, st_start_ref, st_rows_ref, n_used_ref,
                    xs_ref, w1g_ref, w1u_ref, b1g_ref, b1u_ref, w2_ref, b2_ref,
                    ys_ref,
                    xb_ref, acc_ref, wg_ref, wu_ref, wd_ref, xstage_ref, ostage_ref, xsem, osem):
    s = pl.program_id(0)
    j = pl.program_id(1)
    parity = s % 2
    rows = st_rows_ref[s]
    start = st_start_ref[s]
    n_blk = rows // SUB
    next_start = st_start_ref[s + 1]
    next_blk = st_rows_ref[s + 1] // SUB

    def x_copy(tok0, slot):
        return _token_copy(xs_ref, tok0, xstage_ref.at[slot], 0, xsem.at[slot], SUB)

    def y_copy(slot, tok0):
        return _token_copy(ostage_ref.at[slot], 0, ys_ref, tok0, osem.at[slot], SUB)

    def convert(slot, par, blk):
        off = pl.multiple_of(blk * SUB, SUB)
        for c in range(TOK_ROWS):
            xb_ref[par, pl.ds(off, SUB), c * V7X_LANES:(c + 1) * V7X_LANES] = (
                _load_token_major(xstage_ref.at[slot], 0, SUB, c).astype(BF16))

    @pl.when((s == 0) & (j == 0))
    def _():
        ostage_ref[0] = jnp.zeros(ostage_ref.shape[1:], F32)
        tail_start = n_used_ref[1]
        n_tail = (ys_ref.shape[0] // TOK_ROWS - tail_start) // SUB

        def tstart(b, _):
            y_copy(0, tail_start + b * SUB).start()
            return 0

        def twait(b, _):
            y_copy(0, tail_start + b * SUB).wait()
            return 0

        lax.fori_loop(0, n_tail, tstart, 0)
        lax.fori_loop(0, n_tail, twait, 0)

        def first(b, _):
            x_copy(start + b * SUB, 0).start()
            x_copy(start + b * SUB, 0).wait()
            convert(0, 0, b)
            return 0

        lax.fori_loop(0, n_blk, first, 0)

    for p in range(PREFETCH_SLOTS):
        @pl.when(j * PREFETCH_SLOTS + p < next_blk)
        def _(p=p):
            x_copy(next_start + (j * PREFETCH_SLOTS + p) * SUB, p).start()

    def compute(last):
        def cast_weights():
            wg_ref[...] = w1g_ref[0].astype(BF16)
            wu_ref[...] = w1u_ref[0].astype(BF16)
            wd_ref[...] = w2_ref[0].astype(BF16)

        bg = b1g_ref[0]
        bu = b1u_ref[0]
        b2 = b2_ref[0]

        def chunk(row0, m):
            off = row0 if isinstance(row0, int) else pl.multiple_of(row0, SUB)
            xb = xb_ref[parity, pl.ds(off, m), :]
            g = _dot(xb, wg_ref[...]) + bg
            u = _dot(xb, wu_ref[...]) + bu
            g = jnp.minimum(g, SWIGLU_LIMIT)
            u = jnp.clip(u, -SWIGLU_LIMIT, SWIGLU_LIMIT)
            act = g * jax.nn.sigmoid(SWIGLU_ALPHA * g) * (u + 1.0)
            y = _dot(act.astype(BF16), wd_ref[...])
            if not last:
                acc_ref[pl.ds(off, m), :] += y
                return
            y = acc_ref[pl.ds(off, m), :] + y + b2
            for i in range(m // SUB):
                blk = row0 // SUB + i
                slot = blk % OUT_SLOTS

                @pl.when(blk >= OUT_SLOTS)
                def _(blk=blk, slot=slot):
                    y_copy(slot, start + (blk - OUT_SLOTS) * SUB).wait()

                _store_token_major(ostage_ref.at[slot], 0, y[i * SUB:(i + 1) * SUB, :])
                y_copy(slot, start + blk * SUB).start()

        n_big = rows // MM_ROWS

        def big(r, _):
            chunk(r * MM_ROWS, MM_ROWS)
            return 0

        @pl.when(n_big > 0)
        def _():
            cast_weights()
            chunk(0, MM_ROWS)

        @pl.when(n_big == 0)
        def _():
            cast_weights()

        lax.fori_loop(1, n_big, big, 0)
        done = n_big * MM_ROWS
        m = MM_ROWS // 2
        while m >= SUB:
            take = ((rows - done) & m) != 0

            @pl.when(take)
            def _(done=done, m=m):
                chunk(done, m)

            done = done + jnp.where(take, m, 0)
            m //= 2

    @pl.when((j == 0) & (rows > 0))
    def _():
        def zero(b, _):
            acc_ref[pl.ds(pl.multiple_of(b * SUB, SUB), SUB), :] = jnp.zeros((SUB, D_MODEL), F32)
            return 0

        lax.fori_loop(0, n_blk, zero, 0)

    @pl.when((j < N_FF - 1) & (rows > 0))
    def _():
        compute(False)

    @pl.when((j == N_FF - 1) & (rows > 0))
    def _():
        compute(True)

    for p in range(PREFETCH_SLOTS):
        @pl.when(j * PREFETCH_SLOTS + p < next_blk)
        def _(p=p):
            blk = j * PREFETCH_SLOTS + p
            x_copy(next_start + blk * SUB, p).wait()
            convert(p, 1 - parity, blk)

    def drain(tok0, blocks):
        for slot in range(OUT_SLOTS):
            @pl.when(slot < blocks)
            def _(slot=slot):
                y_copy(slot, tok0 + _last_block_on_slot(blocks, slot) * SUB).wait()

    @pl.when((j == 0) & (s > 0))
    def _():
        prev = jnp.maximum(s - 1, 0)
        drain(st_start_ref[prev], st_rows_ref[prev] // SUB)

    @pl.when((j == N_FF - 1) & (s == pl.num_programs(0) - 1))
    def _():
        drain(start, n_blk)


def _last_block_on_slot(n_blk, slot):
    return slot + OUT_SLOTS * ((n_blk - 1 - slot) // OUT_SLOTS)


def _experts(st_e, st_start, st_rows, n_used, xs, w1, b1, w2, b2, n_super):
    n_rows = xs.shape[0] // TOK_ROWS

    def ff(s, j, n_used_ref):
        return jnp.where(s < n_used_ref[0], j, N_FF - 1)

    w1g_map = lambda s, j, e, st, rw, nu: (e[s], 0, ff(s, j, nu))
    w1u_map = lambda s, j, e, st, rw, nu: (e[s], 0, N_FF + ff(s, j, nu))
    w2_map = lambda s, j, e, st, rw, nu: (e[s], ff(s, j, nu), 0)
    b2_map = lambda s, j, e, st, rw, nu: (e[s], 0, 0)
    return pl.pallas_call(
        _experts_kernel,
        grid_spec=pltpu.PrefetchScalarGridSpec(
            num_scalar_prefetch=4,
            grid=(n_used[0], N_FF),
            in_specs=[
                pl.BlockSpec(memory_space=pl.ANY),
                pl.BlockSpec((1, D_MODEL, FF_TILE), w1g_map),
                pl.BlockSpec((1, D_MODEL, FF_TILE), w1u_map),
                pl.BlockSpec((1, 1, FF_TILE), w1g_map),
                pl.BlockSpec((1, 1, FF_TILE), w1u_map),
                pl.BlockSpec((1, FF_TILE, D_MODEL), w2_map),
                pl.BlockSpec((1, 1, D_MODEL), b2_map),
            ],
            out_specs=pl.BlockSpec(memory_space=pl.ANY),
            scratch_shapes=[
                pltpu.VMEM((2, SUPER_ROWS, D_MODEL), BF16),
                pltpu.VMEM((SUPER_ROWS, D_MODEL), F32),
                pltpu.VMEM((D_MODEL, FF_TILE), BF16),
                pltpu.VMEM((D_MODEL, FF_TILE), BF16),
                pltpu.VMEM((FF_TILE, D_MODEL), BF16),
                pltpu.VMEM((PREFETCH_SLOTS, SUB * TOK_ROWS, V7X_LANES), F32),
                pltpu.VMEM((OUT_SLOTS, SUB * TOK_ROWS, V7X_LANES), F32),
                pltpu.SemaphoreType.DMA((PREFETCH_SLOTS,)),
                pltpu.SemaphoreType.DMA((OUT_SLOTS,)),
            ],
        ),
        out_shape=jax.ShapeDtypeStruct((n_rows * TOK_ROWS, V7X_LANES), F32),
        compiler_params=pltpu.CompilerParams(
            dimension_semantics=("arbitrary", "arbitrary"), vmem_limit_bytes=V7X_VMEM_LIMIT,
            has_side_effects=True),
        name="experts",
    )(st_e, st_start, st_rows, n_used, xs, w1, w1, b1, b1, w2, b2)


def _combine_kernel(dest_ref, ys_ref, gate_ref, h1_ref, g2_ref, b2_ref, o_ref, buf_ref, sem):
    step = pl.program_id(0)
    n_steps = pl.num_programs(0)
    tm = COMBINE_TILE

    def copies(at_step, slot, t):
        base = at_step * (tm * TOP_K)
        return [_token_copy(ys_ref, dest_ref[base + t * TOP_K + k], buf_ref.at[slot, k], t, sem.at[slot])
                for k in range(TOP_K)]

    def gather(at_step, slot):
        def start(t, _):
            for c in copies(at_step, slot, t):
                c.start()
            return 0

        lax.fori_loop(0, tm, start, 0)

    @pl.when(step == 0)
    def _():
        gather(0, 0)

    @pl.when(step + 1 < n_steps)
    def _():
        gather(step + 1, (step + 1) % 2)

    slot = step % 2

    def wait(t, _):
        for c in copies(step, slot, t):
            c.wait()
        return 0

    lax.fori_loop(0, tm, wait, 0)

    gates = gate_ref[...]
    for c in range(TOK_ROWS):
        z = DEEPNORM_ALPHA * _load_token_major(h1_ref, 0, tm, c)
        for k in range(TOP_K):
            z = z + _load_token_major(buf_ref.at[slot, k], 0, tm, c) * gates[:, k:k + 1]
        o_ref[:, c * V7X_LANES:(c + 1) * V7X_LANES] = z
    o_ref[...] = _layer_norm(o_ref[...], g2_ref[...], b2_ref[...])


def _combine(dest_flat, ys, gates, h1t, g2, b2):
    tokens = h1t.shape[0] // TOK_ROWS
    tm = COMBINE_TILE
    row = lambda i, *_: (i, 0)
    return pl.pallas_call(
        _combine_kernel,
        grid_spec=pltpu.PrefetchScalarGridSpec(
            num_scalar_prefetch=1,
            grid=(tokens // tm,),
            in_specs=[
                pl.BlockSpec(memory_space=pl.ANY),
                pl.BlockSpec((tm, V7X_LANES), row),
                pl.BlockSpec((tm * TOK_ROWS, V7X_LANES), row),
                pl.BlockSpec((1, D_MODEL), lambda i, *_: (0, 0)),
                pl.BlockSpec((1, D_MODEL), lambda i, *_: (0, 0)),
            ],
            out_specs=pl.BlockSpec((tm, D_MODEL), row),
            scratch_shapes=[pltpu.VMEM((2, TOP_K, tm * TOK_ROWS, V7X_LANES), F32),
                            pltpu.SemaphoreType.DMA((2,))],
        ),
        out_shape=jax.ShapeDtypeStruct((tokens, D_MODEL), F32),
        compiler_params=pltpu.CompilerParams(
            dimension_semantics=("arbitrary",), vmem_limit_bytes=V7X_VMEM_LIMIT),
        name="combine",
    )(dest_flat, ys, gates, h1t, g2, b2)


def _rotate_half_cols(w):
    half = QK_ROPE // 2
    return jnp.concatenate([-w[..., half:], w[..., :half]], axis=-1)


def _rope_table(length):
    inv_freq = 1.0 / (ROPE_THETA ** (jnp.arange(0, QK_ROPE, 2, dtype=F32) / QK_ROPE))
    freqs = jnp.arange(length, dtype=F32)[:, None] * inv_freq[None, :]
    emb = jnp.concatenate([freqs, freqs], axis=-1)
    return jnp.concatenate([jnp.cos(emb), jnp.sin(emb)], axis=-1)


def _routing_plan(idx, rank, counts, n_super):
    experts = jnp.arange(N_EXPERTS, dtype=jnp.int32)

    def lookup(table, i):
        return jnp.sum(jnp.where(i[..., None] == experts, table, 0), axis=-1)

    def bucket(cum, i):
        return jnp.minimum(jnp.sum((cum <= i[..., None]).astype(jnp.int32), axis=-1), N_EXPERTS - 1)

    counts = counts.astype(jnp.int32)
    padded = (counts + SEG_ALIGN - 1) // SEG_ALIGN * SEG_ALIGN
    pad_end = jnp.cumsum(padded)
    pad_start = pad_end - padded
    dest = (lookup(pad_start, idx) + rank).reshape(-1).astype(jnp.int32)

    n_padmax = N_EXPERTS * SEG_ALIGN
    padcnt = padded - counts
    padcum = jnp.cumsum(padcnt)
    p = jnp.arange(n_padmax, dtype=jnp.int32)
    pe = bucket(padcum, p)
    pad_rows = lookup(pad_start + counts - (padcum - padcnt), pe) + p
    n_pad = jnp.stack([padcum[-1], pad_end[-1]]).astype(jnp.int32)
    pad_rows = jnp.where(p < n_pad[0], pad_rows, 0).astype(jnp.int32)

    n_st = (padded + SUPER_ROWS - 1) // SUPER_ROWS
    st_cum = jnp.cumsum(n_st)
    n_used = jnp.stack([st_cum[-1], pad_end[-1]]).astype(jnp.int32)
    s = jnp.arange(n_super + 1, dtype=jnp.int32)
    s_eff = jnp.minimum(s, n_used[0] - 1)
    se = bucket(st_cum, s_eff).astype(jnp.int32)
    local = s_eff - lookup(st_cum - n_st, se)
    st_start = (lookup(pad_start, se) + local * SUPER_ROWS).astype(jnp.int32)
    st_rows = jnp.clip(lookup(padded, se) - local * SUPER_ROWS, 0, SUPER_ROWS)
    st_rows = jnp.where(s < n_used[0], st_rows, 0).astype(jnp.int32)
    return dest, pad_rows, n_pad, se, st_start, st_rows, n_used


def kernel(x, meta_tokens, ln_in_g, ln_in_b, w_in, q_norm_g, w_uq, kv_norm_g, w_uk, w_uv, conv_dw_w,
           conv_dw_b, conv_ln_g, conv_ln_b, w_out, ln1_g, ln1_b, w_router, b_router, w_mlp1, b_mlp1,
           w_mlp2, b_mlp2, ln2_g, ln2_b):
    batch, seq, _ = x.shape
    tokens = batch * seq
    row2 = lambda a: a.reshape(1, -1)

    wi = w_in[0]
    s_kpe = Q_LORA + KV_LORA
    s_conv = s_kpe + QK_ROPE
    kpe_w = wi[:, s_kpe:s_conv]
    w_proj = (wi[:, :s_kpe].astype(BF16),
              jnp.concatenate([kpe_w, _rotate_half_cols(kpe_w)], axis=1).astype(BF16),
              wi[:, s_conv:s_conv + CONV_CH].astype(BF16),
              wi[:, s_conv + CONV_CH:].astype(BF16))
    wq3 = w_uq[0].reshape(Q_LORA, N_HEADS, QK_DIM)
    wq_nope = wq3[:, :, :QK_NOPE].reshape(Q_LORA, N_HEADS * QK_NOPE)
    wq_pe = wq3[:, :, QK_NOPE:]
    wq_pr = jnp.concatenate([wq_pe, _rotate_half_cols(wq_pe)], axis=-1).reshape(Q_LORA, N_HEADS * 2 * QK_ROPE)
    wq = jnp.concatenate([wq_nope, wq_pr], axis=1).astype(BF16)
    wuk = w_uk[0].astype(BF16)
    wuv = w_uv[0].astype(BF16)
    wo = w_out[0].astype(BF16)
    wr = w_router[0]
    wr_hi = wr.astype(BF16)
    wr_lo = (wr - wr_hi.astype(F32)).astype(BF16)
    cs = _rope_table(N_META + seq)
    conv_w = jnp.repeat(conv_dw_w[0], V7X_SUBLANES, axis=0)

    x2d = x.reshape(tokens, D_MODEL)
    proj_args = (row2(ln_in_g), row2(ln_in_b), *w_proj, row2(q_norm_g[0]), row2(kv_norm_g[0]), wq, wuk, wuv)

    _, k_meta, v_meta, glu_meta = _in_proj(meta_tokens, *proj_args, cs[:N_META], N_META)
    q, k, v, glu = _in_proj(x2d, *proj_args, cs[N_META:], ROW_TILE)
    k_meta = jnp.pad(k_meta, ((0, 0), (0, 0), (0, V7X_LANES - N_META)))
    v_meta = jnp.pad(v_meta, ((0, 0), (0, V7X_LANES - N_META), (0, 0)))
    attn = _attention(q, k, v, k_meta, v_meta, batch, seq)
    conv = _conv(glu, glu_meta, conv_w, row2(conv_dw_b[0]), row2(conv_ln_g[0]), row2(conv_ln_b[0]), batch, seq)

    h1t, idx, rank, gates, counts = _out_proj(
        attn, conv, x2d, row2(ln_in_g), row2(ln_in_b), wo, row2(ln1_g[0]), row2(ln1_b[0]),
        wr_hi, wr_lo, row2(b_router[0]))

    n_assign = tokens * TOP_K
    n_rows = n_assign + N_EXPERTS * SEG_ALIGN
    n_super = N_EXPERTS + -(-n_assign // SUPER_ROWS)
    dest, pad_rows, n_pad, st_e, st_start, st_rows, n_used = _routing_plan(
        idx[:, :TOP_K], rank[:, :TOP_K], counts[0], n_super)

    xs = _dispatch(dest, pad_rows, n_pad, h1t, n_rows)
    ys = _experts(st_e, st_start, st_rows, n_used, xs, w_mlp1[0], b_mlp1[0].reshape(N_EXPERTS, 1, 2 * D_FF),
                  w_mlp2[0], b_mlp2[0].reshape(N_EXPERTS, 1, D_MODEL), n_super)
    out = _combine(dest, ys, gates, h1t, row2(ln2_g[0]), row2(ln2_b[0]))
    return out.reshape(batch, seq, D_MODEL)
```

```python
import functools
import math

import jax
import jax.numpy as jnp
from jax import lax
from jax.experimental import pallas as pl
from jax.experimental.pallas import tpu as pltpu

D_MODEL = 2048
N_META = 16
N_HEADS = 8
QK_NOPE = 128
QK_ROPE = 64
QK_DIM = QK_NOPE + QK_ROPE
V_DIM = 128
Q_LORA = 768
KV_LORA = 512
ROPE_THETA = 10000.0
MLA_WIDTH = N_HEADS * V_DIM
CONV_CH = 1024
CONV_W = 31
N_EXPERTS = 32
TOP_K = 4
D_FF = 2048
SWIGLU_LIMIT = 7.0
SWIGLU_ALPHA = 1.702
DEEPNORM_ALPHA = 2.0 ** 0.25
LN_EPS = 1e-5
RMS_EPS = 1e-6

V7X_LANES = 128
V7X_SUBLANES = 8
V7X_VMEM_LIMIT = 56 * 1024 * 1024

ROW_TILE = 256
OUT_TILE = 256
OUT_CHAINS = 1
ATT_TILE = 512
ATT_CHAINS = 2
CONV_TILE = 256
CONV_HALO = 32
CONV_ROWS = 32
CONV_LANES = 256
DISPATCH_TILE = 256
COMBINE_TILE = 128
SEG_ALIGN = 128
SUPER_ROWS = 1536
FF_TILE = 256
MM_ROWS = 512

F32 = jnp.float32
BF16 = jnp.bfloat16


def _dot(a, b):
    return jnp.dot(a, b, preferred_element_type=F32)


def _dot_nt(a, b):
    return lax.dot_general(a, b, (((1,), (1,)), ((), ())), preferred_element_type=F32)


def _layer_norm(x, g, b):
    mu = jnp.mean(x, axis=-1, keepdims=True)
    xc = x - mu
    var = jnp.mean(xc * xc, axis=-1, keepdims=True)
    return xc * lax.rsqrt(var + LN_EPS) * g + b


def _rms_norm(x, g):
    ms = jnp.mean(x * x, axis=-1, keepdims=True)
    return x * lax.rsqrt(ms + RMS_EPS) * g


def _const_spec(shape):
    zeros = (0,) * len(shape)
    return pl.BlockSpec(shape, lambda *_: zeros)


TOK_ROWS = D_MODEL // V7X_LANES


def _load_token_major(ref, row0, n_tok, j, pitch=TOK_ROWS):
    return ref[pl.ds(row0 + j, n_tok, stride=pitch), :]


PADDED_PITCH = TOK_ROWS + V7X_SUBLANES


def _store_token_major(ref, row0, x):
    n_tok = x.shape[0]
    for j in range(TOK_ROWS):
        ref[pl.ds(row0 + j, n_tok, stride=TOK_ROWS), :] = x[:, j * V7X_LANES:(j + 1) * V7X_LANES]


def _in_proj_kernel(x_ref, lng_ref, lnb_ref, wc_ref, wkpe_ref, wa_ref, wg_ref, qg_ref, kvg_ref,
                    wq_ref, wuk_ref, wuv_ref, cs_ref,
                    q_ref, k_ref, v_ref, glu_ref):
    h0 = _layer_norm(x_ref[...], lng_ref[...], lnb_ref[...])
    hb = h0.astype(BF16)
    cs = cs_ref[...]

    def rope(t128):
        t = t128 * cs
        return t + pltpu.roll(t, QK_ROPE, axis=1)

    cq = _dot(hb, wc_ref[:, 0:Q_LORA])
    cqn = _rms_norm(cq, qg_ref[...]).astype(BF16)
    ckv = _dot(hb, wc_ref[:, Q_LORA:Q_LORA + KV_LORA])
    ckvn = _rms_norm(ckv, kvg_ref[...]).astype(BF16)
    kpe = rope(_dot(hb, wkpe_ref[...]))

    a = _dot(hb, wa_ref[...])
    g = _dot(hb, wg_ref[...])
    glu_ref[...] = a * jax.nn.sigmoid(g)

    knope = _dot(ckvn, wuk_ref[...])
    v = _dot(ckvn, wuv_ref[...])
    kpe_t = kpe.T[0:QK_ROPE, :].astype(BF16)
    for h in range(N_HEADS):
        k_ref[h, 0:QK_NOPE, :] = knope[:, h * QK_NOPE:(h + 1) * QK_NOPE].T.astype(BF16)
        k_ref[h, QK_NOPE:QK_DIM, :] = kpe_t
        v_ref[h] = v[:, h * V_DIM:(h + 1) * V_DIM].astype(BF16)

    qn = _dot(cqn, wq_ref[:, 0:N_HEADS * QK_NOPE])
    qp = _dot(cqn, wq_ref[:, N_HEADS * QK_NOPE:])
    for h in range(N_HEADS):
        q_ref[h, :, 0:QK_NOPE] = qn[:, h * QK_NOPE:(h + 1) * QK_NOPE].astype(BF16)
        q_ref[h, :, QK_NOPE:QK_DIM] = rope(qp[:, h * V7X_LANES:(h + 1) * V7X_LANES])[:, :QK_ROPE].astype(BF16)


def _in_proj(x2d, lng, lnb, wc, wkpe, wa, wg, qg, kvg, wq, wuk, wuv, cs, tm):
    rows = x2d.shape[0]
    n_cs = cs.shape[0] // tm
    row = lambda i: (i, 0)
    head_row = lambda i: (0, i, 0)
    return pl.pallas_call(
        _in_proj_kernel,
        grid=(rows // tm,),
        in_specs=[
            pl.BlockSpec((tm, D_MODEL), row),
            _const_spec((1, D_MODEL)), _const_spec((1, D_MODEL)),
            _const_spec((D_MODEL, Q_LORA + KV_LORA)), _const_spec((D_MODEL, 2 * QK_ROPE)),
            _const_spec((D_MODEL, CONV_CH)), _const_spec((D_MODEL, CONV_CH)),
            _const_spec((1, Q_LORA)), _const_spec((1, KV_LORA)),
            _const_spec((Q_LORA, 2 * N_HEADS * QK_NOPE)),
            _const_spec((KV_LORA, N_HEADS * QK_NOPE)), _const_spec((KV_LORA, MLA_WIDTH)),
            pl.BlockSpec((tm, V7X_LANES), lambda i: (i % n_cs, 0)),
        ],
        out_specs=[
            pl.BlockSpec((N_HEADS, tm, QK_DIM), head_row),
            pl.BlockSpec((N_HEADS, QK_DIM, tm), lambda i: (0, 0, i)),
            pl.BlockSpec((N_HEADS, tm, V_DIM), head_row),
            pl.BlockSpec((tm, CONV_CH), row),
        ],
        out_shape=[
            jax.ShapeDtypeStruct((N_HEADS, rows, QK_DIM), BF16),
            jax.ShapeDtypeStruct((N_HEADS, QK_DIM, rows), BF16),
            jax.ShapeDtypeStruct((N_HEADS, rows, V_DIM), BF16),
            jax.ShapeDtypeStruct((rows, CONV_CH), F32),
        ],
        compiler_params=pltpu.CompilerParams(
            dimension_semantics=("arbitrary",), vmem_limit_bytes=V7X_VMEM_LIMIT),
        name="in_proj",
    )(x2d, lng, lnb, wc, wkpe, wa, wg, qg, kvg, wq, wuk, wuv, cs)


def _attention_kernel(q_ref, k_ref, v_ref, km_ref, vm_ref, o_ref, *state):
    i = pl.program_id(2)
    c_exp = (1.0 / math.sqrt(QK_DIM)) * math.log2(math.e)
    chain_rows = ATT_TILE // ATT_CHAINS
    chains = range(ATT_CHAINS)
    m_refs, l_refs, acc_refs = (state[n * ATT_CHAINS:(n + 1) * ATT_CHAINS] for n in range(3))

    def lane_tiles(x):
        return [x[:, t * V7X_LANES:(t + 1) * V7X_LANES] for t in range(x.shape[1] // V7X_LANES)]

    def row_max(x):
        if x.shape[1] % V7X_LANES:
            return jnp.max(x, axis=1, keepdims=True)
        return jnp.max(functools.reduce(jnp.maximum, lane_tiles(x)), axis=1, keepdims=True)

    def lane_partial_sum(x):
        if x.shape[1] % V7X_LANES:
            lane = lax.broadcasted_iota(jnp.int32, (x.shape[0], V7X_LANES), 1)
            return jnp.where(lane == 0, jnp.sum(x, axis=1, keepdims=True), 0.0)
        return functools.reduce(jnp.add, lane_tiles(x))

    def update(h, s, vb, first):
        s_max = jnp.broadcast_to(row_max(s), (s.shape[0], V7X_LANES))
        if first:
            m_new = s_max
        else:
            m_old = m_refs[h][...]
            m_new = jnp.maximum(m_old, s_max)
            alpha = jnp.exp2(c_exp * (m_old - m_new))
        if s.shape[1] % V7X_LANES:
            p = jnp.exp2(c_exp * (s - m_new[:, :s.shape[1]]))
        else:
            p = jnp.concatenate([jnp.exp2(c_exp * (t - m_new)) for t in lane_tiles(s)], axis=1)
        p_sum = lane_partial_sum(p)
        pv = _dot(p.astype(BF16), vb)
        m_refs[h][...] = m_new
        if first:
            l_refs[h][...] = p_sum
            acc_refs[h][...] = pv
        else:
            l_refs[h][...] = alpha * l_refs[h][...] + p_sum
            acc_refs[h][...] = alpha * acc_refs[h][...] + pv

    def chain(x, h):
        return x[h * chain_rows:(h + 1) * chain_rows, :]

    q = q_ref[0]

    start = pl.multiple_of(i * ATT_TILE, ATT_TILE)
    s = _dot(q, jnp.concatenate([k_ref[0, :, pl.ds(start, ATT_TILE)], km_ref[0]], axis=1))
    r = lax.broadcasted_iota(jnp.int32, s.shape, 0)
    c = lax.broadcasted_iota(jnp.int32, s.shape, 1)
    last_visible = jnp.where(c >= ATT_TILE, ATT_TILE + N_META - 1, r)
    s = jnp.where(c <= last_visible, s, -1e30)
    for h in chains:
        cols = (h + 1) * chain_rows
        s_h = jnp.concatenate([chain(s, h)[:, :cols], chain(s, h)[:, ATT_TILE:]], axis=1)
        v_h = jnp.concatenate([v_ref[0, pl.ds(start, cols), :], vm_ref[0]], axis=0)
        update(h, s_h, v_h, True)

    def block(j, _):
        start = pl.multiple_of(j * ATT_TILE, ATT_TILE)
        s = _dot(q, k_ref[0, :, pl.ds(start, ATT_TILE)])
        vb = v_ref[0, pl.ds(start, ATT_TILE), :]
        for h in chains:
            update(h, chain(s, h), vb, False)
        return 0

    lax.fori_loop(0, i, block, 0)

    for h in chains:
        l = jnp.sum(l_refs[h][...], axis=1, keepdims=True)
        o_ref[pl.ds(h * chain_rows, chain_rows), :] = (acc_refs[h][...] / l).astype(BF16)


def _attention(q, k, v, km, vm, batch, seq):
    nq = seq // ATT_TILE
    return pl.pallas_call(
        _attention_kernel,
        grid=(batch, N_HEADS, nq),
        in_specs=[
            pl.BlockSpec((1, ATT_TILE, QK_DIM), lambda b, h, i: (h, b * nq + i, 0)),
            pl.BlockSpec((1, QK_DIM, seq), lambda b, h, i: (h, 0, b)),
            pl.BlockSpec((1, seq, V_DIM), lambda b, h, i: (h, b, 0)),
            pl.BlockSpec((1, QK_DIM, V7X_LANES), lambda b, h, i: (h, 0, 0)),
            pl.BlockSpec((1, V7X_LANES, V_DIM), lambda b, h, i: (h, 0, 0)),
        ],
        out_specs=pl.BlockSpec((ATT_TILE, V_DIM), lambda b, h, i: (b * nq + i, h)),
        out_shape=jax.ShapeDtypeStruct((batch * seq, MLA_WIDTH), BF16),
        scratch_shapes=[pltpu.VMEM((ATT_TILE // ATT_CHAINS, V7X_LANES), F32)] * (3 * ATT_CHAINS),
        compiler_params=pltpu.CompilerParams(
            dimension_semantics=("arbitrary", "arbitrary", "arbitrary"), vmem_limit_bytes=V7X_VMEM_LIMIT),
        name="attention",
    )(q, k, v, km, vm)


def _conv_kernel(cur_ref, prev_ref, meta_ref, w_ref, cb_ref, lng_ref, lnb_ref, o_ref,
                 win_ref, shift_ref, acc_ref):
    i = pl.program_id(1)

    @pl.when(i == 0)
    def _():
        win_ref[0:CONV_HALO - N_META, :] = jnp.zeros((CONV_HALO - N_META, CONV_CH), F32)
        win_ref[CONV_HALO - N_META:CONV_HALO, :] = meta_ref[...]

    @pl.when(i > 0)
    def _():
        win_ref[0:CONV_HALO, :] = prev_ref[...]

    win_ref[CONV_HALO:CONV_HALO + CONV_TILE, :] = cur_ref[...]
    win_ref[CONV_HALO + CONV_TILE:, :] = jnp.zeros((V7X_SUBLANES, CONV_CH), F32)

    base = CONV_HALO - (CONV_W - 1)
    win_rows = CONV_HALO + CONV_TILE
    for shift in range(1, V7X_SUBLANES):
        for r in range(0, win_rows, CONV_ROWS):
            shift_ref[shift - 1, pl.ds(r, CONV_ROWS), :] = win_ref[pl.ds(r + shift, CONV_ROWS), :]

    def window(shift, row, lanes):
        if shift == 0:
            return win_ref[pl.ds(row, CONV_ROWS), lanes]
        return shift_ref[shift - 1, pl.ds(row, CONV_ROWS), lanes]

    for rc in range(CONV_TILE // CONV_ROWS):
        r0 = rc * CONV_ROWS
        for c in range(CONV_CH // CONV_LANES):
            lanes = pl.ds(c * CONV_LANES, CONV_LANES)
            acc = jnp.zeros((CONV_ROWS, CONV_LANES), F32)
            for k in range(CONV_W):
                shift = (base + k) % V7X_SUBLANES
                w_k = jnp.concatenate([w_ref[k * V7X_SUBLANES:(k + 1) * V7X_SUBLANES, lanes]]
                                      * (CONV_ROWS // V7X_SUBLANES), axis=0)
                acc = acc + window(shift, r0 + base + k - shift, lanes) * w_k
            acc_ref[pl.ds(r0, CONV_ROWS), lanes] = acc

    y = _layer_norm(acc_ref[...] + cb_ref[...], lng_ref[...], lnb_ref[...])
    o_ref[...] = (y * jax.nn.sigmoid(y)).astype(BF16)


def _conv(glu, glu_meta, w, cb, lng, lnb, batch, seq):
    nt = seq // CONV_TILE
    per = CONV_TILE // CONV_HALO
    return pl.pallas_call(
        _conv_kernel,
        grid=(batch, nt),
        in_specs=[
            pl.BlockSpec((CONV_TILE, CONV_CH), lambda b, i: (b * nt + i, 0)),
            pl.BlockSpec((CONV_HALO, CONV_CH), lambda b, i: (jnp.maximum((b * nt + i) * per - 1, 0), 0)),
            _const_spec((N_META, CONV_CH)),
            _const_spec((CONV_W * V7X_SUBLANES, CONV_CH)),
            _const_spec((1, CONV_CH)), _const_spec((1, CONV_CH)), _const_spec((1, CONV_CH)),
        ],
        out_specs=pl.BlockSpec((CONV_TILE, CONV_CH), lambda b, i: (b * nt + i, 0)),
        out_shape=jax.ShapeDtypeStruct((batch * seq, CONV_CH), BF16),
        scratch_shapes=[pltpu.VMEM((CONV_HALO + CONV_TILE + V7X_SUBLANES, CONV_CH), F32),
                        pltpu.VMEM((V7X_SUBLANES - 1, CONV_HALO + CONV_TILE, CONV_CH), F32),
                        pltpu.VMEM((CONV_TILE, CONV_CH), F32)],
        compiler_params=pltpu.CompilerParams(
            dimension_semantics=("arbitrary", "arbitrary"), vmem_limit_bytes=V7X_VMEM_LIMIT),
        name="conv",
    )(glu, glu, glu_meta, w, cb, lng, lnb)


def _out_proj_kernel(attn_ref, conv_ref, x_ref, lng_ref, lnb_ref, wo_ref, g1_ref, b1_ref,
                     wrh_ref, wrl_ref, br_ref,
                     h1_ref, idx_ref, rank_ref, gate_ref, cnt_ref, carry_ref):
    step = pl.program_id(0)
    tm = x_ref.shape[0] // OUT_CHAINS

    @pl.when(step == 0)
    def _():
        carry_ref[...] = jnp.zeros_like(carry_ref)

    counts = carry_ref[...]
    for ch in range(OUT_CHAINS):
        counts = _route_chain(pl.ds(ch * tm, tm), ch * tm * TOK_ROWS, counts,
                              attn_ref, conv_ref, x_ref, lng_ref, lnb_ref, wo_ref, g1_ref, b1_ref,
                              wrh_ref, wrl_ref, br_ref, h1_ref, idx_ref, rank_ref, gate_ref)
    carry_ref[...] = counts
    cnt_ref[...] = counts.astype(jnp.int32)


def _route_chain(rows, h1_row0, counts, attn_ref, conv_ref, x_ref, lng_ref, lnb_ref, wo_ref, g1_ref, b1_ref,
                 wrh_ref, wrl_ref, br_ref, h1_ref, idx_ref, rank_ref, gate_ref):
    tm = rows.size
    h0 = _layer_norm(x_ref[rows, :], lng_ref[...], lnb_ref[...])
    mix = _dot(attn_ref[rows, :], wo_ref[0:MLA_WIDTH, :]) + _dot(conv_ref[rows, :], wo_ref[MLA_WIDTH:, :])
    h1 = _layer_norm(DEEPNORM_ALPHA * h0 + mix, g1_ref[...], b1_ref[...])
    _store_token_major(h1_ref, h1_row0, h1)

    hi = h1.astype(BF16)
    lo = (h1 - hi.astype(F32)).astype(BF16)
    logits = (_dot(hi, wrh_ref[...]) + (_dot(hi, wrl_ref[...]) + _dot(lo, wrh_ref[...]))) + br_ref[...]

    lane = lax.broadcasted_iota(jnp.int32, (tm, N_EXPERTS), 1)
    work = logits
    vals, idxs = [], []
    for _ in range(TOP_K):
        mx = jnp.max(work, axis=1, keepdims=True)
        ix = jnp.min(jnp.where(work == mx, lane, N_EXPERTS), axis=1, keepdims=True)
        vals.append(mx)
        idxs.append(ix)
        work = jnp.where(lane == ix, -jnp.inf, work)
    exps = [jnp.exp(v - vals[0]) for v in vals]
    denom = exps[0] + exps[1] + exps[2] + exps[3]

    onehots = [(lane == ix) for ix in idxs]
    chosen = (onehots[0] | onehots[1] | onehots[2] | onehots[3])
    chosen_f = jnp.where(chosen, 1.0, 0.0)
    r = lax.broadcasted_iota(jnp.int32, (tm, tm), 0)
    c = lax.broadcasted_iota(jnp.int32, (tm, tm), 1)
    lower = jnp.where(c < r, 1.0, 0.0).astype(BF16)
    before = _dot(lower, chosen_f.astype(BF16)) + counts

    out_lane = lax.broadcasted_iota(jnp.int32, (tm, V7X_LANES), 1)
    idx_out = jnp.zeros((tm, V7X_LANES), jnp.int32)
    rank_out = jnp.zeros((tm, V7X_LANES), jnp.int32)
    gate_out = jnp.zeros((tm, V7X_LANES), F32)
    for k in range(TOP_K):
        rank_k = jnp.sum(jnp.where(onehots[k], before, 0.0), axis=1, keepdims=True).astype(jnp.int32)
        idx_out = jnp.where(out_lane == k, idxs[k], idx_out)
        rank_out = jnp.where(out_lane == k, rank_k, rank_out)
        gate_out = jnp.where(out_lane == k, exps[k] / denom, gate_out)
    idx_ref[rows, :] = idx_out
    rank_ref[rows, :] = rank_out
    gate_ref[rows, :] = gate_out
    return counts + jnp.sum(chosen_f, axis=0, keepdims=True)


def _out_proj(attn, conv, x2d, lng, lnb, wo, g1, b1, wrh, wrl, br):
    rows = x2d.shape[0]
    tm = OUT_TILE
    row = lambda i: (i, 0)
    return pl.pallas_call(
        _out_proj_kernel,
        grid=(rows // tm,),
        in_specs=[
            pl.BlockSpec((tm, MLA_WIDTH), row), pl.BlockSpec((tm, CONV_CH), row),
            pl.BlockSpec((tm, D_MODEL), row),
            _const_spec((1, D_MODEL)), _const_spec((1, D_MODEL)),
            _const_spec((D_MODEL, D_MODEL)),
            _const_spec((1, D_MODEL)), _const_spec((1, D_MODEL)),
            _const_spec((D_MODEL, N_EXPERTS)), _const_spec((D_MODEL, N_EXPERTS)),
            _const_spec((1, N_EXPERTS)),
        ],
        out_specs=[
            pl.BlockSpec((tm * TOK_ROWS, V7X_LANES), row),
            pl.BlockSpec((tm, V7X_LANES), row), pl.BlockSpec((tm, V7X_LANES), row),
            pl.BlockSpec((tm, V7X_LANES), row),
            _const_spec((1, N_EXPERTS)),
        ],
        out_shape=[
            jax.ShapeDtypeStruct((rows * TOK_ROWS, V7X_LANES), F32),
            jax.ShapeDtypeStruct((rows, V7X_LANES), jnp.int32),
            jax.ShapeDtypeStruct((rows, V7X_LANES), jnp.int32),
            jax.ShapeDtypeStruct((rows, V7X_LANES), F32),
            jax.ShapeDtypeStruct((1, N_EXPERTS), jnp.int32),
        ],
        scratch_shapes=[pltpu.VMEM((1, N_EXPERTS), F32)],
        compiler_params=pltpu.CompilerParams(
            dimension_semantics=("arbitrary",), vmem_limit_bytes=V7X_VMEM_LIMIT),
        name="out_proj_router",
    )(attn, conv, x2d, lng, lnb, wo, g1, b1, wrh, wrl, br)


def _token_copy(src_ref, src_tok, dst_ref, dst_tok, sem, n_tok=1, dst_pitch=TOK_ROWS):
    assert n_tok == 1 or dst_pitch == TOK_ROWS
    rows = n_tok * TOK_ROWS
    src = src_ref.at[pl.ds(pl.multiple_of(src_tok * TOK_ROWS, V7X_SUBLANES), rows), :]
    dst = dst_ref.at[pl.ds(pl.multiple_of(dst_tok * dst_pitch, V7X_SUBLANES), rows), :]
    return pltpu.make_async_copy(src, dst, sem)


def _dispatch_kernel(dest_ref, padrow_ref, npad_ref, h1_ref, xs_ref, zero_ref, sem, zsem):
    step = pl.program_id(0)
    base = step * (DISPATCH_TILE * TOP_K)

    def copies(t):
        return [_token_copy(h1_ref, t, xs_ref, dest_ref[base + t * TOP_K + k], sem) for k in range(TOP_K)]

    def start(t, _):
        for c in copies(t):
            c.start()
        return 0

    def wait(t, _):
        for c in copies(t):
            c.wait()
        return 0

    lax.fori_loop(0, DISPATCH_TILE, start, 0)

    @pl.when(step == 0)
    def _():
        zero_ref[...] = jnp.zeros_like(zero_ref)
        n = npad_ref[0]
        tail_start = npad_ref[1]
        n_tail = (xs_ref.shape[0] // TOK_ROWS - tail_start) // SEG_ALIGN

        def zstart(p, _):
            _token_copy(zero_ref, 0, xs_ref, padrow_ref[p], zsem).start()
            return 0

        def zwait(p, _):
            _token_copy(zero_ref, 0, xs_ref, padrow_ref[p], zsem).wait()
            return 0

        def tstart(b, _):
            _token_copy(zero_ref, 0, xs_ref, tail_start + b * SEG_ALIGN, zsem, SEG_ALIGN).start()
            return 0

        def twait(b, _):
            _token_copy(zero_ref, 0, xs_ref, tail_start + b * SEG_ALIGN, zsem, SEG_ALIGN).wait()
            return 0

        lax.fori_loop(0, n, zstart, 0)
        lax.fori_loop(0, n, zwait, 0)
        lax.fori_loop(0, n_tail, tstart, 0)
        lax.fori_loop(0, n_tail, twait, 0)

    lax.fori_loop(0, DISPATCH_TILE, wait, 0)


def _dispatch(dest_flat, pad_rows, n_pad, h1t, n_rows):
    tokens = h1t.shape[0] // TOK_ROWS
    return pl.pallas_call(
        _dispatch_kernel,
        grid_spec=pltpu.PrefetchScalarGridSpec(
            num_scalar_prefetch=3,
            grid=(tokens // DISPATCH_TILE,),
            in_specs=[pl.BlockSpec((DISPATCH_TILE * TOK_ROWS, V7X_LANES), lambda i, *_: (i, 0))],
            out_specs=pl.BlockSpec(memory_space=pl.ANY),
            scratch_shapes=[pltpu.VMEM((SEG_ALIGN * TOK_ROWS, V7X_LANES), F32),
                            pltpu.SemaphoreType.DMA, pltpu.SemaphoreType.DMA],
        ),
        out_shape=jax.ShapeDtypeStruct((n_rows * TOK_ROWS, V7X_LANES), F32),
        compiler_params=pltpu.CompilerParams(
            dimension_semantics=("arbitrary",), has_side_effects=True),
        name="dispatch",
    )(dest_flat, pad_rows, n_pad, h1t)


N_FF = D_FF // FF_TILE
SUB = SEG_ALIGN
N_SUB = SUPER_ROWS // SUB
PREFETCH_SLOTS = -(-N_SUB // N_FF)
OUT_SLOTS = MM_ROWS // SUB


def _experts_kernel(st_e_ref, st_start_ref, st_rows_ref, n_used_ref,
                    xs_ref, w1g_ref, w1u_ref, b1g_ref, b1u_ref, w2_ref, b2_ref,
                    ys_ref,
                    xb_ref, acc_ref, wg_ref, wu_ref, wd_ref, xstage_ref, ostage_ref, xsem, osem):
    s = pl.program_id(0)
    j = pl.program_id(1)
    parity = s % 2
    rows = st_rows_ref[s]
    start = st_start_ref[s]
    n_blk = rows // SUB
    next_start = st_start_ref[s + 1]
    next_blk = st_rows_ref[s + 1] // SUB

    def x_copy(tok0, slot):
        return _token_copy(xs_ref, tok0, xstage_ref.at[slot], 0, xsem.at[slot], SUB)

    def y_copy(slot, tok0):
        return _token_copy(ostage_ref.at[slot], 0, ys_ref, tok0, osem.at[slot], SUB)

    def convert(slot, par, blk):
        off = pl.multiple_of(blk * SUB, SUB)
        for c in range(TOK_ROWS):
            xb_ref[par, pl.ds(off, SUB), c * V7X_LANES:(c + 1) * V7X_LANES] = (
                _load_token_major(xstage_ref.at[slot], 0, SUB, c).astype(BF16))

    @pl.when((s == 0) & (j == 0))
    def _():
        ostage_ref[0] = jnp.zeros(ostage_ref.shape[1:], F32)
        tail_start = n_used_ref[1]
        n_tail = (ys_ref.shape[0] // TOK_ROWS - tail_start) // SUB

        def tstart(b, _):
            y_copy(0, tail_start + b * SUB).start()
            return 0

        def twait(b, _):
            y_copy(0, tail_start + b * SUB).wait()
            return 0

        lax.fori_loop(0, n_tail, tstart, 0)
        lax.fori_loop(0, n_tail, twait, 0)

        def first(b, _):
            x_copy(start + b * SUB, 0).start()
            x_copy(start + b * SUB, 0).wait()
            convert(0, 0, b)
            return 0

        lax.fori_loop(0, n_blk, first, 0)

    for p in range(PREFETCH_SLOTS):
        @pl.when(j * PREFETCH_SLOTS + p < next_blk)
        def _(p=p):
            x_copy(next_start + (j * PREFETCH_SLOTS + p) * SUB, p).start()

    def compute(last):
        def cast_weights():
            wg_ref[...] = w1g_ref[0].astype(BF16)
            wu_ref[...] = w1u_ref[0].astype(BF16)
            wd_ref[...] = w2_ref[0].astype(BF16)

        bg = b1g_ref[0]
        bu = b1u_ref[0]
        b2 = b2_ref[0]

        def chunk(row0, m):
            off = row0 if isinstance(row0, int) else pl.multiple_of(row0, SUB)
            xb = xb_ref[parity, pl.ds(off, m), :]
            g = _dot(xb, wg_ref[...]) + bg
            u = _dot(xb, wu_ref[...]) + bu
            g = jnp.minimum(g, SWIGLU_LIMIT)
            u = jnp.clip(u, -SWIGLU_LIMIT, SWIGLU_LIMIT)
            act = g * jax.nn.sigmoid(SWIGLU_ALPHA * g) * (u + 1.0)
            y = _dot(act.astype(BF16), wd_ref[...])
            if not last:
                acc_ref[pl.ds(off, m), :] += y
                return
            y = acc_ref[pl.ds(off, m), :] + y + b2
            for i in range(m // SUB):
                blk = row0 // SUB + i
                slot = blk % OUT_SLOTS

                @pl.when(blk >= OUT_SLOTS)
                def _(blk=blk, slot=slot):
                    y_copy(slot, start + (blk - OUT_SLOTS) * SUB).wait()

                _store_token_major(ostage_ref.at[slot], 0, y[i * SUB:(i + 1) * SUB, :])
                y_copy(slot, start + blk * SUB).start()

        n_big = rows // MM_ROWS

        def big(r, _):
            chunk(r * MM_ROWS, MM_ROWS)
            return 0

        @pl.when(n_big > 0)
        def _():
            cast_weights()
            chunk(0, MM_ROWS)

        @pl.when(n_big == 0)
        def _():
            cast_weights()

        lax.fori_loop(1, n_big, big, 0)
        done = n_big * MM_ROWS
        m = MM_ROWS // 2
        while m >= SUB:
            take = ((rows - done) & m) != 0

            @pl.when(take)
            def _(done=done, m=m):
                chunk(done, m)

            done = done + jnp.where(take, m, 0)
            m //= 2

    @pl.when((j == 0) & (rows > 0))
    def _():
        def zero(b, _):
            acc_ref[pl.ds(pl.multiple_of(b * SUB, SUB), SUB), :] = jnp.zeros((SUB, D_MODEL), F32)
            return 0

        lax.fori_loop(0, n_blk, zero, 0)

    @pl.when((j < N_FF - 1) & (rows > 0))
    def _():
        compute(False)

    @pl.when((j == N_FF - 1) & (rows > 0))
    def _():
        compute(True)

    for p in range(PREFETCH_SLOTS):
        @pl.when(j * PREFETCH_SLOTS + p < next_blk)
        def _(p=p):
            blk = j * PREFETCH_SLOTS + p
            x_copy(next_start + blk * SUB, p).wait()
            convert(p, 1 - parity, blk)

    def drain(tok0, blocks):
        for slot in range(OUT_SLOTS):
            @pl.when(slot < blocks)
            def _(slot=slot):
                y_copy(slot, tok0 + _last_block_on_slot(blocks, slot) * SUB).wait()

    @pl.when((j == 0) & (s > 0))
    def _():
        prev = jnp.maximum(s - 1, 0)
        drain(st_start_ref[prev], st_rows_ref[prev] // SUB)

    @pl.when((j == N_FF - 1) & (s == pl.num_programs(0) - 1))
    def _():
        drain(start, n_blk)


def _last_block_on_slot(n_blk, slot):
    return slot + OUT_SLOTS * ((n_blk - 1 - slot) // OUT_SLOTS)


def _experts(st_e, st_start, st_rows, n_used, xs, w1, b1, w2, b2, n_super):
    n_rows = xs.shape[0] // TOK_ROWS

    def ff(s, j, n_used_ref):
        return jnp.where(s < n_used_ref[0], j, N_FF - 1)

    w1g_map = lambda s, j, e, st, rw, nu: (e[s], 0, ff(s, j, nu))
    w1u_map = lambda s, j, e, st, rw, nu: (e[s], 0, N_FF + ff(s, j, nu))
    w2_map = lambda s, j, e, st, rw, nu: (e[s], ff(s, j, nu), 0)
    b2_map = lambda s, j, e, st, rw, nu: (e[s], 0, 0)
    return pl.pallas_call(
        _experts_kernel,
        grid_spec=pltpu.PrefetchScalarGridSpec(
            num_scalar_prefetch=4,
            grid=(n_used[0], N_FF),
            in_specs=[
                pl.BlockSpec(memory_space=pl.ANY),
                pl.BlockSpec((1, D_MODEL, FF_TILE), w1g_map),
                pl.BlockSpec((1, D_MODEL, FF_TILE), w1u_map),
                pl.BlockSpec((1, 1, FF_TILE), w1g_map),
                pl.BlockSpec((1, 1, FF_TILE), w1u_map),
                pl.BlockSpec((1, FF_TILE, D_MODEL), w2_map),
                pl.BlockSpec((1, 1, D_MODEL), b2_map),
            ],
            out_specs=pl.BlockSpec(memory_space=pl.ANY),
            scratch_shapes=[
                pltpu.VMEM((2, SUPER_ROWS, D_MODEL), BF16),
                pltpu.VMEM((SUPER_ROWS, D_MODEL), F32),
                pltpu.VMEM((D_MODEL, FF_TILE), BF16),
                pltpu.VMEM((D_MODEL, FF_TILE), BF16),
                pltpu.VMEM((FF_TILE, D_MODEL), BF16),
                pltpu.VMEM((PREFETCH_SLOTS, SUB * TOK_ROWS, V7X_LANES), F32),
                pltpu.VMEM((OUT_SLOTS, SUB * TOK_ROWS, V7X_LANES), F32),
                pltpu.SemaphoreType.DMA((PREFETCH_SLOTS,)),
                pltpu.SemaphoreType.DMA((OUT_SLOTS,)),
            ],
        ),
        out_shape=jax.ShapeDtypeStruct((n_rows * TOK_ROWS, V7X_LANES), F32),
        compiler_params=pltpu.CompilerParams(
            dimension_semantics=("arbitrary", "arbitrary"), vmem_limit_bytes=V7X_VMEM_LIMIT,
            has_side_effects=True),
        name="experts",
    )(st_e, st_start, st_rows, n_used, xs, w1, w1, b1, b1, w2, b2)


def _combine_kernel(dest_ref, ys_ref, gate_ref, h1_ref, g2_ref, b2_ref, o_ref, buf_ref, sem):
    step = pl.program_id(0)
    n_steps = pl.num_programs(0)
    tm = COMBINE_TILE

    def copies(at_step, slot, t):
        base = at_step * (tm * TOP_K)
        return [_token_copy(ys_ref, dest_ref[base + t * TOP_K + k], buf_ref.at[slot, k], t, sem.at[slot],
                            dst_pitch=PADDED_PITCH)
                for k in range(TOP_K)]

    def gather(at_step, slot):
        def start(t, _):
            for c in copies(at_step, slot, t):
                c.start()
            return 0

        lax.fori_loop(0, tm, start, 0)

    @pl.when(step == 0)
    def _():
        gather(0, 0)

    @pl.when(step + 1 < n_steps)
    def _():
        gather(step + 1, (step + 1) % 2)

    slot = step % 2

    def wait(t, _):
        for c in copies(step, slot, t):
            c.wait()
        return 0

    lax.fori_loop(0, tm, wait, 0)

    gates = gate_ref[...]
    for c in range(TOK_ROWS):
        z = DEEPNORM_ALPHA * _load_token_major(h1_ref, 0, tm, c)
        for k in range(TOP_K):
            z = z + _load_token_major(buf_ref.at[slot, k], 0, tm, c, PADDED_PITCH) * gates[:, k:k + 1]
        o_ref[:, c * V7X_LANES:(c + 1) * V7X_LANES] = z
    o_ref[...] = _layer_norm(o_ref[...], g2_ref[...], b2_ref[...])


def _combine(dest_flat, ys, gates, h1t, g2, b2):
    tokens = h1t.shape[0] // TOK_ROWS
    tm = COMBINE_TILE
    row = lambda i, *_: (i, 0)
    return pl.pallas_call(
        _combine_kernel,
        grid_spec=pltpu.PrefetchScalarGridSpec(
            num_scalar_prefetch=1,
            grid=(tokens // tm,),
            in_specs=[
                pl.BlockSpec(memory_space=pl.ANY),
                pl.BlockSpec((tm, V7X_LANES), row),
                pl.BlockSpec((tm * TOK_ROWS, V7X_LANES), row),
                pl.BlockSpec((1, D_MODEL), lambda i, *_: (0, 0)),
                pl.BlockSpec((1, D_MODEL), lambda i, *_: (0, 0)),
            ],
            out_specs=pl.BlockSpec((tm, D_MODEL), row),
            scratch_shapes=[pltpu.VMEM((2, TOP_K, tm * PADDED_PITCH, V7X_LANES), F32),
                            pltpu.SemaphoreType.DMA((2,))],
        ),
        out_shape=jax.ShapeDtypeStruct((tokens, D_MODEL), F32),
        compiler_params=pltpu.CompilerParams(
            dimension_semantics=("arbitrary",), vmem_limit_bytes=V7X_VMEM_LIMIT),
        name="combine",
    )(dest_flat, ys, gates, h1t, g2, b2)


def _rotate_half_cols(w):
    half = QK_ROPE // 2
    return jnp.concatenate([-w[..., half:], w[..., :half]], axis=-1)


def _rope_table(length):
    inv_freq = 1.0 / (ROPE_THETA ** (jnp.arange(0, QK_ROPE, 2, dtype=F32) / QK_ROPE))
    freqs = jnp.arange(length, dtype=F32)[:, None] * inv_freq[None, :]
    emb = jnp.concatenate([freqs, freqs], axis=-1)
    return jnp.concatenate([jnp.cos(emb), jnp.sin(emb)], axis=-1)


def _routing_plan(idx, rank, counts, n_super):
    experts = jnp.arange(N_EXPERTS, dtype=jnp.int32)

    def lookup(table, i):
        return jnp.sum(jnp.where(i[..., None] == experts, table, 0), axis=-1)

    def bucket(cum, i):
        return jnp.minimum(jnp.sum((cum <= i[..., None]).astype(jnp.int32), axis=-1), N_EXPERTS - 1)

    counts = counts.astype(jnp.int32)
    padded = (counts + SEG_ALIGN - 1) // SEG_ALIGN * SEG_ALIGN
    pad_end = jnp.cumsum(padded)
    pad_start = pad_end - padded
    dest = (lookup(pad_start, idx) + rank).reshape(-1).astype(jnp.int32)

    n_padmax = N_EXPERTS * SEG_ALIGN
    padcnt = padded - counts
    padcum = jnp.cumsum(padcnt)
    p = jnp.arange(n_padmax, dtype=jnp.int32)
    pe = bucket(padcum, p)
    pad_rows = lookup(pad_start + counts - (padcum - padcnt), pe) + p
    n_pad = jnp.stack([padcum[-1], pad_end[-1]]).astype(jnp.int32)
    pad_rows = jnp.where(p < n_pad[0], pad_rows, 0).astype(jnp.int32)

    n_st = (padded + SUPER_ROWS - 1) // SUPER_ROWS
    st_cum = jnp.cumsum(n_st)
    n_used = jnp.stack([st_cum[-1], pad_end[-1]]).astype(jnp.int32)
    s = jnp.arange(n_super + 1, dtype=jnp.int32)
    s_eff = jnp.minimum(s, n_used[0] - 1)
    se = bucket(st_cum, s_eff).astype(jnp.int32)
    local = s_eff - lookup(st_cum - n_st, se)
    st_start = (lookup(pad_start, se) + local * SUPER_ROWS).astype(jnp.int32)
    st_rows = jnp.clip(lookup(padded, se) - local * SUPER_ROWS, 0, SUPER_ROWS)
    st_rows = jnp.where(s < n_used[0], st_rows, 0).astype(jnp.int32)
    return dest, pad_rows, n_pad, se, st_start, st_rows, n_used


def kernel(x, meta_tokens, ln_in_g, ln_in_b, w_in, q_norm_g, w_uq, kv_norm_g, w_uk, w_uv, conv_dw_w,
           conv_dw_b, conv_ln_g, conv_ln_b, w_out, ln1_g, ln1_b, w_router, b_router, w_mlp1, b_mlp1,
           w_mlp2, b_mlp2, ln2_g, ln2_b):
    batch, seq, _ = x.shape
    tokens = batch * seq
    row2 = lambda a: a.reshape(1, -1)

    wi = w_in[0]
    s_kpe = Q_LORA + KV_LORA
    s_conv = s_kpe + QK_ROPE
    kpe_w = wi[:, s_kpe:s_conv]
    w_proj = (wi[:, :s_kpe].astype(BF16),
              jnp.concatenate([kpe_w, _rotate_half_cols(kpe_w)], axis=1).astype(BF16),
              wi[:, s_conv:s_conv + CONV_CH].astype(BF16),
              wi[:, s_conv + CONV_CH:].astype(BF16))
    wq3 = w_uq[0].reshape(Q_LORA, N_HEADS, QK_DIM)
    wq_nope = wq3[:, :, :QK_NOPE].reshape(Q_LORA, N_HEADS * QK_NOPE)
    wq_pe = wq3[:, :, QK_NOPE:]
    wq_pr = jnp.concatenate([wq_pe, _rotate_half_cols(wq_pe)], axis=-1).reshape(Q_LORA, N_HEADS * 2 * QK_ROPE)
    wq = jnp.concatenate([wq_nope, wq_pr], axis=1).astype(BF16)
    wuk = w_uk[0].astype(BF16)
    wuv = w_uv[0].astype(BF16)
    wo = w_out[0].astype(BF16)
    wr = w_router[0]
    wr_hi = wr.astype(BF16)
    wr_lo = (wr - wr_hi.astype(F32)).astype(BF16)
    cs = _rope_table(N_META + seq)
    conv_w = jnp.repeat(conv_dw_w[0], V7X_SUBLANES, axis=0)

    x2d = x.reshape(tokens, D_MODEL)
    proj_args = (row2(ln_in_g), row2(ln_in_b), *w_proj, row2(q_norm_g[0]), row2(kv_norm_g[0]), wq, wuk, wuv)

    _, k_meta, v_meta, glu_meta = _in_proj(meta_tokens, *proj_args, cs[:N_META], N_META)
    q, k, v, glu = _in_proj(x2d, *proj_args, cs[N_META:], ROW_TILE)
    k_meta = jnp.pad(k_meta, ((0, 0), (0, 0), (0, V7X_LANES - N_META)))
    v_meta = jnp.pad(v_meta, ((0, 0), (0, V7X_LANES - N_META), (0, 0)))
    attn = _attention(q, k, v, k_meta, v_meta, batch, seq)
    conv = _conv(glu, glu_meta, conv_w, row2(conv_dw_b[0]), row2(conv_ln_g[0]), row2(conv_ln_b[0]), batch, seq)

    h1t, idx, rank, gates, counts = _out_proj(
        attn, conv, x2d, row2(ln_in_g), row2(ln_in_b), wo, row2(ln1_g[0]), row2(ln1_b[0]),
        wr_hi, wr_lo, row2(b_router[0]))

    n_assign = tokens * TOP_K
    n_rows = n_assign + N_EXPERTS * SEG_ALIGN
    n_super = N_EXPERTS + -(-n_assign // SUPER_ROWS)
    dest, pad_rows, n_pad, st_e, st_start, st_rows, n_used = _routing_plan(
        idx[:, :TOP_K], rank[:, :TOP_K], counts[0], n_super)

    xs = _dispatch(dest, pad_rows, n_pad, h1t, n_rows)
    ys = _experts(st_e, st_start, st_rows, n_used, xs, w_mlp1[0], b_mlp1[0].reshape(N_EXPERTS, 1, 2 * D_FF),
                  w_mlp2[0], b_mlp2[0].reshape(N_EXPERTS, 1, D_MODEL), n_super)
    out = _combine(dest, ys, gates, h1t, row2(ln2_g[0]), row2(ln2_b[0]))
    return out.reshape(batch, seq, D_MODEL)
```

```python
import functools
import math

import jax
import jax.numpy as jnp
from jax import lax
from jax.experimental import pallas as pl
from jax.experimental.pallas import tpu as pltpu

D_MODEL = 2048
N_META = 16
N_HEADS = 8
QK_NOPE = 128
QK_ROPE = 64
QK_DIM = QK_NOPE + QK_ROPE
V_DIM = 128
Q_LORA = 768
KV_LORA = 512
ROPE_THETA = 10000.0
MLA_WIDTH = N_HEADS * V_DIM
CONV_CH = 1024
CONV_W = 31
N_EXPERTS = 32
TOP_K = 4
D_FF = 2048
SWIGLU_LIMIT = 7.0
SWIGLU_ALPHA = 1.702
DEEPNORM_ALPHA = 2.0 ** 0.25
LN_EPS = 1e-5
RMS_EPS = 1e-6

V7X_LANES = 128
V7X_SUBLANES = 8
DMA_QUEUES = 2
V7X_VMEM_LIMIT = 56 * 1024 * 1024

ROW_TILE = 256
OUT_TILE = 256
OUT_CHAINS = 1
ATT_TILE = 512
ATT_CHAINS = 2
CONV_TILE = 256
CONV_HALO = 32
CONV_ROWS = 32
CONV_LANES = 256
DISPATCH_TILE = 256
COMBINE_TILE = 128
SEG_ALIGN = 128
SUPER_ROWS = 1536
FF_TILE = 256
MM_ROWS = 512

F32 = jnp.float32
BF16 = jnp.bfloat16


def _dot(a, b):
    return jnp.dot(a, b, preferred_element_type=F32)


def _dot_nt(a, b):
    return lax.dot_general(a, b, (((1,), (1,)), ((), ())), preferred_element_type=F32)


def _layer_norm(x, g, b):
    mu = jnp.mean(x, axis=-1, keepdims=True)
    xc = x - mu
    var = jnp.mean(xc * xc, axis=-1, keepdims=True)
    return xc * lax.rsqrt(var + LN_EPS) * g + b


def _rms_norm(x, g):
    ms = jnp.mean(x * x, axis=-1, keepdims=True)
    return x * lax.rsqrt(ms + RMS_EPS) * g


def _const_spec(shape):
    zeros = (0,) * len(shape)
    return pl.BlockSpec(shape, lambda *_: zeros)


TOK_ROWS = D_MODEL // V7X_LANES


def _load_token_major(ref, row0, n_tok, j, pitch=TOK_ROWS):
    return ref[pl.ds(row0 + j, n_tok, stride=pitch), :]


PADDED_PITCH = TOK_ROWS + V7X_SUBLANES


def _store_token_major(ref, row0, x):
    n_tok = x.shape[0]
    for j in range(TOK_ROWS):
        ref[pl.ds(row0 + j, n_tok, stride=TOK_ROWS), :] = x[:, j * V7X_LANES:(j + 1) * V7X_LANES]


def _in_proj_kernel(x_ref, lng_ref, lnb_ref, wc_ref, wkpe_ref, wa_ref, wg_ref, qg_ref, kvg_ref,
                    wq_ref, wuk_ref, wuv_ref, cs_ref,
                    q_ref, k_ref, v_ref, glu_ref):
    h0 = _layer_norm(x_ref[...], lng_ref[...], lnb_ref[...])
    hb = h0.astype(BF16)
    cs = cs_ref[...]

    def rope(t128):
        t = t128 * cs
        return t + pltpu.roll(t, QK_ROPE, axis=1)

    cq = _dot(hb, wc_ref[:, 0:Q_LORA])
    cqn = _rms_norm(cq, qg_ref[...]).astype(BF16)
    ckv = _dot(hb, wc_ref[:, Q_LORA:Q_LORA + KV_LORA])
    ckvn = _rms_norm(ckv, kvg_ref[...]).astype(BF16)
    kpe = rope(_dot(hb, wkpe_ref[...]))

    a = _dot(hb, wa_ref[...])
    g = _dot(hb, wg_ref[...])
    glu_ref[...] = a * jax.nn.sigmoid(g)

    knope = _dot(ckvn, wuk_ref[...])
    v = _dot(ckvn, wuv_ref[...])
    kpe_t = kpe.T[0:QK_ROPE, :].astype(BF16)
    for h in range(N_HEADS):
        k_ref[h, 0:QK_NOPE, :] = knope[:, h * QK_NOPE:(h + 1) * QK_NOPE].T.astype(BF16)
        k_ref[h, QK_NOPE:QK_DIM, :] = kpe_t
        v_ref[h] = v[:, h * V_DIM:(h + 1) * V_DIM].astype(BF16)

    qn = _dot(cqn, wq_ref[:, 0:N_HEADS * QK_NOPE])
    qp = _dot(cqn, wq_ref[:, N_HEADS * QK_NOPE:])
    for h in range(N_HEADS):
        q_ref[h, :, 0:QK_NOPE] = qn[:, h * QK_NOPE:(h + 1) * QK_NOPE].astype(BF16)
        q_ref[h, :, QK_NOPE:QK_DIM] = rope(qp[:, h * V7X_LANES:(h + 1) * V7X_LANES])[:, :QK_ROPE].astype(BF16)


def _in_proj(x2d, lng, lnb, wc, wkpe, wa, wg, qg, kvg, wq, wuk, wuv, cs, tm):
    rows = x2d.shape[0]
    n_cs = cs.shape[0] // tm
    row = lambda i: (i, 0)
    head_row = lambda i: (0, i, 0)
    return pl.pallas_call(
        _in_proj_kernel,
        grid=(rows // tm,),
        in_specs=[
            pl.BlockSpec((tm, D_MODEL), row),
            _const_spec((1, D_MODEL)), _const_spec((1, D_MODEL)),
            _const_spec((D_MODEL, Q_LORA + KV_LORA)), _const_spec((D_MODEL, 2 * QK_ROPE)),
            _const_spec((D_MODEL, CONV_CH)), _const_spec((D_MODEL, CONV_CH)),
            _const_spec((1, Q_LORA)), _const_spec((1, KV_LORA)),
            _const_spec((Q_LORA, 2 * N_HEADS * QK_NOPE)),
            _const_spec((KV_LORA, N_HEADS * QK_NOPE)), _const_spec((KV_LORA, MLA_WIDTH)),
            pl.BlockSpec((tm, V7X_LANES), lambda i: (i % n_cs, 0)),
        ],
        out_specs=[
            pl.BlockSpec((N_HEADS, tm, QK_DIM), head_row),
            pl.BlockSpec((N_HEADS, QK_DIM, tm), lambda i: (0, 0, i)),
            pl.BlockSpec((N_HEADS, tm, V_DIM), head_row),
            pl.BlockSpec((tm, CONV_CH), row),
        ],
        out_shape=[
            jax.ShapeDtypeStruct((N_HEADS, rows, QK_DIM), BF16),
            jax.ShapeDtypeStruct((N_HEADS, QK_DIM, rows), BF16),
            jax.ShapeDtypeStruct((N_HEADS, rows, V_DIM), BF16),
            jax.ShapeDtypeStruct((rows, CONV_CH), F32),
        ],
        compiler_params=pltpu.CompilerParams(
            dimension_semantics=("arbitrary",), vmem_limit_bytes=V7X_VMEM_LIMIT),
        name="in_proj",
    )(x2d, lng, lnb, wc, wkpe, wa, wg, qg, kvg, wq, wuk, wuv, cs)


def _attention_kernel(q_ref, k_ref, v_ref, km_ref, vm_ref, o_ref, *state):
    i = pl.program_id(2)
    c_exp = (1.0 / math.sqrt(QK_DIM)) * math.log2(math.e)
    chain_rows = ATT_TILE // ATT_CHAINS
    chains = range(ATT_CHAINS)
    m_refs, l_refs, acc_refs = (state[n * ATT_CHAINS:(n + 1) * ATT_CHAINS] for n in range(3))

    def lane_tiles(x):
        return [x[:, t * V7X_LANES:(t + 1) * V7X_LANES] for t in range(x.shape[1] // V7X_LANES)]

    def row_max(x):
        if x.shape[1] % V7X_LANES:
            return jnp.max(x, axis=1, keepdims=True)
        return jnp.max(functools.reduce(jnp.maximum, lane_tiles(x)), axis=1, keepdims=True)

    def lane_partial_sum(x):
        if x.shape[1] % V7X_LANES:
            lane = lax.broadcasted_iota(jnp.int32, (x.shape[0], V7X_LANES), 1)
            return jnp.where(lane == 0, jnp.sum(x, axis=1, keepdims=True), 0.0)
        return functools.reduce(jnp.add, lane_tiles(x))

    def update(h, s, vb, first):
        s_max = jnp.broadcast_to(row_max(s), (s.shape[0], V7X_LANES))
        if first:
            m_new = s_max
        else:
            m_old = m_refs[h][...]
            m_new = jnp.maximum(m_old, s_max)
            alpha = jnp.exp2(c_exp * (m_old - m_new))
        if s.shape[1] % V7X_LANES:
            p = jnp.exp2(c_exp * (s - m_new[:, :s.shape[1]]))
        else:
            p = jnp.concatenate([jnp.exp2(c_exp * (t - m_new)) for t in lane_tiles(s)], axis=1)
        p_sum = lane_partial_sum(p)
        pv = _dot(p.astype(BF16), vb)
        m_refs[h][...] = m_new
        if first:
            l_refs[h][...] = p_sum
            acc_refs[h][...] = pv
        else:
            l_refs[h][...] = alpha * l_refs[h][...] + p_sum
            acc_refs[h][...] = alpha * acc_refs[h][...] + pv

    def chain(x, h):
        return x[h * chain_rows:(h + 1) * chain_rows, :]

    q = q_ref[0]

    start = pl.multiple_of(i * ATT_TILE, ATT_TILE)
    s = _dot(q, jnp.concatenate([k_ref[0, :, pl.ds(start, ATT_TILE)], km_ref[0]], axis=1))
    r = lax.broadcasted_iota(jnp.int32, s.shape, 0)
    c = lax.broadcasted_iota(jnp.int32, s.shape, 1)
    last_visible = jnp.where(c >= ATT_TILE, ATT_TILE + N_META - 1, r)
    s = jnp.where(c <= last_visible, s, -1e30)
    for h in chains:
        cols = (h + 1) * chain_rows
        s_h = jnp.concatenate([chain(s, h)[:, :cols], chain(s, h)[:, ATT_TILE:]], axis=1)
        v_h = jnp.concatenate([v_ref[0, pl.ds(start, cols), :], vm_ref[0]], axis=0)
        update(h, s_h, v_h, True)

    def block(j, _):
        start = pl.multiple_of(j * ATT_TILE, ATT_TILE)
        s = _dot(q, k_ref[0, :, pl.ds(start, ATT_TILE)])
        vb = v_ref[0, pl.ds(start, ATT_TILE), :]
        for h in chains:
            update(h, chain(s, h), vb, False)
        return 0

    lax.fori_loop(0, i, block, 0)

    for h in chains:
        l = jnp.sum(l_refs[h][...], axis=1, keepdims=True)
        o_ref[pl.ds(h * chain_rows, chain_rows), :] = (acc_refs[h][...] / l).astype(BF16)


def _attention(q, k, v, km, vm, batch, seq):
    nq = seq // ATT_TILE
    return pl.pallas_call(
        _attention_kernel,
        grid=(batch, N_HEADS, nq),
        in_specs=[
            pl.BlockSpec((1, ATT_TILE, QK_DIM), lambda b, h, i: (h, b * nq + i, 0)),
            pl.BlockSpec((1, QK_DIM, seq), lambda b, h, i: (h, 0, b)),
            pl.BlockSpec((1, seq, V_DIM), lambda b, h, i: (h, b, 0)),
            pl.BlockSpec((1, QK_DIM, V7X_LANES), lambda b, h, i: (h, 0, 0)),
            pl.BlockSpec((1, V7X_LANES, V_DIM), lambda b, h, i: (h, 0, 0)),
        ],
        out_specs=pl.BlockSpec((ATT_TILE, V_DIM), lambda b, h, i: (b * nq + i, h)),
        out_shape=jax.ShapeDtypeStruct((batch * seq, MLA_WIDTH), BF16),
        scratch_shapes=[pltpu.VMEM((ATT_TILE // ATT_CHAINS, V7X_LANES), F32)] * (3 * ATT_CHAINS),
        compiler_params=pltpu.CompilerParams(
            dimension_semantics=("arbitrary", "arbitrary", "arbitrary"), vmem_limit_bytes=V7X_VMEM_LIMIT),
        name="attention",
    )(q, k, v, km, vm)


def _conv_kernel(cur_ref, prev_ref, meta_ref, w_ref, cb_ref, lng_ref, lnb_ref, o_ref,
                 win_ref, shift_ref, acc_ref):
    i = pl.program_id(1)

    @pl.when(i == 0)
    def _():
        win_ref[0:CONV_HALO - N_META, :] = jnp.zeros((CONV_HALO - N_META, CONV_CH), F32)
        win_ref[CONV_HALO - N_META:CONV_HALO, :] = meta_ref[...]

    @pl.when(i > 0)
    def _():
        win_ref[0:CONV_HALO, :] = prev_ref[...]

    win_ref[CONV_HALO:CONV_HALO + CONV_TILE, :] = cur_ref[...]
    win_ref[CONV_HALO + CONV_TILE:, :] = jnp.zeros((V7X_SUBLANES, CONV_CH), F32)

    base = CONV_HALO - (CONV_W - 1)
    win_rows = CONV_HALO + CONV_TILE
    for shift in range(1, V7X_SUBLANES):
        for r in range(0, win_rows, CONV_ROWS):
            shift_ref[shift - 1, pl.ds(r, CONV_ROWS), :] = win_ref[pl.ds(r + shift, CONV_ROWS), :]

    def window(shift, row, lanes):
        if shift == 0:
            return win_ref[pl.ds(row, CONV_ROWS), lanes]
        return shift_ref[shift - 1, pl.ds(row, CONV_ROWS), lanes]

    for rc in range(CONV_TILE // CONV_ROWS):
        r0 = rc * CONV_ROWS
        for c in range(CONV_CH // CONV_LANES):
            lanes = pl.ds(c * CONV_LANES, CONV_LANES)
            acc = jnp.zeros((CONV_ROWS, CONV_LANES), F32)
            for k in range(CONV_W):
                shift = (base + k) % V7X_SUBLANES
                w_k = jnp.concatenate([w_ref[k * V7X_SUBLANES:(k + 1) * V7X_SUBLANES, lanes]]
                                      * (CONV_ROWS // V7X_SUBLANES), axis=0)
                acc = acc + window(shift, r0 + base + k - shift, lanes) * w_k
            acc_ref[pl.ds(r0, CONV_ROWS), lanes] = acc

    y = _layer_norm(acc_ref[...] + cb_ref[...], lng_ref[...], lnb_ref[...])
    o_ref[...] = (y * jax.nn.sigmoid(y)).astype(BF16)


def _conv(glu, glu_meta, w, cb, lng, lnb, batch, seq):
    nt = seq // CONV_TILE
    per = CONV_TILE // CONV_HALO
    return pl.pallas_call(
        _conv_kernel,
        grid=(batch, nt),
        in_specs=[
            pl.BlockSpec((CONV_TILE, CONV_CH), lambda b, i: (b * nt + i, 0)),
            pl.BlockSpec((CONV_HALO, CONV_CH), lambda b, i: (jnp.maximum((b * nt + i) * per - 1, 0), 0)),
            _const_spec((N_META, CONV_CH)),
            _const_spec((CONV_W * V7X_SUBLANES, CONV_CH)),
            _const_spec((1, CONV_CH)), _const_spec((1, CONV_CH)), _const_spec((1, CONV_CH)),
        ],
        out_specs=pl.BlockSpec((CONV_TILE, CONV_CH), lambda b, i: (b * nt + i, 0)),
        out_shape=jax.ShapeDtypeStruct((batch * seq, CONV_CH), BF16),
        scratch_shapes=[pltpu.VMEM((CONV_HALO + CONV_TILE + V7X_SUBLANES, CONV_CH), F32),
                        pltpu.VMEM((V7X_SUBLANES - 1, CONV_HALO + CONV_TILE, CONV_CH), F32),
                        pltpu.VMEM((CONV_TILE, CONV_CH), F32)],
        compiler_params=pltpu.CompilerParams(
            dimension_semantics=("arbitrary", "arbitrary"), vmem_limit_bytes=V7X_VMEM_LIMIT),
        name="conv",
    )(glu, glu, glu_meta, w, cb, lng, lnb)


def _out_proj_kernel(attn_ref, conv_ref, x_ref, lng_ref, lnb_ref, wo_ref, g1_ref, b1_ref,
                     wrh_ref, wrl_ref, br_ref,
                     h1_ref, idx_ref, rank_ref, gate_ref, cnt_ref, carry_ref):
    step = pl.program_id(0)
    tm = x_ref.shape[0] // OUT_CHAINS

    @pl.when(step == 0)
    def _():
        carry_ref[...] = jnp.zeros_like(carry_ref)

    counts = carry_ref[...]
    for ch in range(OUT_CHAINS):
        counts = _route_chain(pl.ds(ch * tm, tm), ch * tm * TOK_ROWS, counts,
                              attn_ref, conv_ref, x_ref, lng_ref, lnb_ref, wo_ref, g1_ref, b1_ref,
                              wrh_ref, wrl_ref, br_ref, h1_ref, idx_ref, rank_ref, gate_ref)
    carry_ref[...] = counts
    cnt_ref[...] = counts.astype(jnp.int32)


def _route_chain(rows, h1_row0, counts, attn_ref, conv_ref, x_ref, lng_ref, lnb_ref, wo_ref, g1_ref, b1_ref,
                 wrh_ref, wrl_ref, br_ref, h1_ref, idx_ref, rank_ref, gate_ref):
    tm = rows.size
    h0 = _layer_norm(x_ref[rows, :], lng_ref[...], lnb_ref[...])
    mix = _dot(attn_ref[rows, :], wo_ref[0:MLA_WIDTH, :]) + _dot(conv_ref[rows, :], wo_ref[MLA_WIDTH:, :])
    h1 = _layer_norm(DEEPNORM_ALPHA * h0 + mix, g1_ref[...], b1_ref[...])
    _store_token_major(h1_ref, h1_row0, h1)

    hi = h1.astype(BF16)
    lo = (h1 - hi.astype(F32)).astype(BF16)
    logits = (_dot(hi, wrh_ref[...]) + (_dot(hi, wrl_ref[...]) + _dot(lo, wrh_ref[...]))) + br_ref[...]

    lane = lax.broadcasted_iota(jnp.int32, (tm, N_EXPERTS), 1)
    work = logits
    vals, idxs = [], []
    for _ in range(TOP_K):
        mx = jnp.max(work, axis=1, keepdims=True)
        ix = jnp.min(jnp.where(work == mx, lane, N_EXPERTS), axis=1, keepdims=True)
        vals.append(mx)
        idxs.append(ix)
        work = jnp.where(lane == ix, -jnp.inf, work)
    exps = [jnp.exp(v - vals[0]) for v in vals]
    denom = exps[0] + exps[1] + exps[2] + exps[3]

    onehots = [(lane == ix) for ix in idxs]
    chosen = (onehots[0] | onehots[1] | onehots[2] | onehots[3])
    chosen_f = jnp.where(chosen, 1.0, 0.0)
    r = lax.broadcasted_iota(jnp.int32, (tm, tm), 0)
    c = lax.broadcasted_iota(jnp.int32, (tm, tm), 1)
    lower = jnp.where(c < r, 1.0, 0.0).astype(BF16)
    before = _dot(lower, chosen_f.astype(BF16)) + counts

    out_lane = lax.broadcasted_iota(jnp.int32, (tm, V7X_LANES), 1)
    idx_out = jnp.zeros((tm, V7X_LANES), jnp.int32)
    rank_out = jnp.zeros((tm, V7X_LANES), jnp.int32)
    gate_out = jnp.zeros((tm, V7X_LANES), F32)
    for k in range(TOP_K):
        rank_k = jnp.sum(jnp.where(onehots[k], before, 0.0), axis=1, keepdims=True).astype(jnp.int32)
        idx_out = jnp.where(out_lane == k, idxs[k], idx_out)
        rank_out = jnp.where(out_lane == k, rank_k, rank_out)
        gate_out = jnp.where(out_lane == k, exps[k] / denom, gate_out)
    idx_ref[rows, :] = idx_out
    rank_ref[rows, :] = rank_out
    gate_ref[rows, :] = gate_out
    return counts + jnp.sum(chosen_f, axis=0, keepdims=True)


def _out_proj(attn, conv, x2d, lng, lnb, wo, g1, b1, wrh, wrl, br):
    rows = x2d.shape[0]
    tm = OUT_TILE
    row = lambda i: (i, 0)
    return pl.pallas_call(
        _out_proj_kernel,
        grid=(rows // tm,),
        in_specs=[
            pl.BlockSpec((tm, MLA_WIDTH), row), pl.BlockSpec((tm, CONV_CH), row),
            pl.BlockSpec((tm, D_MODEL), row),
            _const_spec((1, D_MODEL)), _const_spec((1, D_MODEL)),
            _const_spec((D_MODEL, D_MODEL)),
            _const_spec((1, D_MODEL)), _const_spec((1, D_MODEL)),
            _const_spec((D_MODEL, N_EXPERTS)), _const_spec((D_MODEL, N_EXPERTS)),
            _const_spec((1, N_EXPERTS)),
        ],
        out_specs=[
            pl.BlockSpec((tm * TOK_ROWS, V7X_LANES), row),
            pl.BlockSpec((tm, V7X_LANES), row), pl.BlockSpec((tm, V7X_LANES), row),
            pl.BlockSpec((tm, V7X_LANES), row),
            _const_spec((1, N_EXPERTS)),
        ],
        out_shape=[
            jax.ShapeDtypeStruct((rows * TOK_ROWS, V7X_LANES), F32),
            jax.ShapeDtypeStruct((rows, V7X_LANES), jnp.int32),
            jax.ShapeDtypeStruct((rows, V7X_LANES), jnp.int32),
            jax.ShapeDtypeStruct((rows, V7X_LANES), F32),
            jax.ShapeDtypeStruct((1, N_EXPERTS), jnp.int32),
        ],
        scratch_shapes=[pltpu.VMEM((1, N_EXPERTS), F32)],
        compiler_params=pltpu.CompilerParams(
            dimension_semantics=("arbitrary",), vmem_limit_bytes=V7X_VMEM_LIMIT),
        name="out_proj_router",
    )(attn, conv, x2d, lng, lnb, wo, g1, b1, wrh, wrl, br)


def _token_copy(src_ref, src_tok, dst_ref, dst_tok, sem, n_tok=1, dst_pitch=TOK_ROWS):
    assert n_tok == 1 or dst_pitch == TOK_ROWS
    rows = n_tok * TOK_ROWS
    src = src_ref.at[pl.ds(pl.multiple_of(src_tok * TOK_ROWS, V7X_SUBLANES), rows), :]
    dst = dst_ref.at[pl.ds(pl.multiple_of(dst_tok * dst_pitch, V7X_SUBLANES), rows), :]
    return pltpu.make_async_copy(src, dst, sem)


def _dispatch_kernel(dest_ref, padrow_ref, npad_ref, h1_ref, xs_ref, zero_ref, sem, zsem):
    step = pl.program_id(0)
    base = step * (DISPATCH_TILE * TOP_K)

    def copies(t):
        return [_token_copy(h1_ref, t, xs_ref, dest_ref[base + t * TOP_K + k], sem) for k in range(TOP_K)]

    def start(t, _):
        for k, c in enumerate(copies(t)):
            c.start(priority=k % DMA_QUEUES)
        return 0

    def wait(t, _):
        for c in copies(t):
            c.wait()
        return 0

    lax.fori_loop(0, DISPATCH_TILE, start, 0)

    @pl.when(step == 0)
    def _():
        zero_ref[...] = jnp.zeros_like(zero_ref)
        n = npad_ref[0]
        tail_start = npad_ref[1]
        n_tail = (xs_ref.shape[0] // TOK_ROWS - tail_start) // SEG_ALIGN

        def zstart(p, _):
            _token_copy(zero_ref, 0, xs_ref, padrow_ref[p], zsem).start()
            return 0

        def zwait(p, _):
            _token_copy(zero_ref, 0, xs_ref, padrow_ref[p], zsem).wait()
            return 0

        def tstart(b, _):
            _token_copy(zero_ref, 0, xs_ref, tail_start + b * SEG_ALIGN, zsem, SEG_ALIGN).start()
            return 0

        def twait(b, _):
            _token_copy(zero_ref, 0, xs_ref, tail_start + b * SEG_ALIGN, zsem, SEG_ALIGN).wait()
            return 0

        lax.fori_loop(0, n, zstart, 0)
        lax.fori_loop(0, n, zwait, 0)
        lax.fori_loop(0, n_tail, tstart, 0)
        lax.fori_loop(0, n_tail, twait, 0)

    lax.fori_loop(0, DISPATCH_TILE, wait, 0)


def _dispatch(dest_flat, pad_rows, n_pad, h1t, n_rows):
    tokens = h1t.shape[0] // TOK_ROWS
    return pl.pallas_call(
        _dispatch_kernel,
        grid_spec=pltpu.PrefetchScalarGridSpec(
            num_scalar_prefetch=3,
            grid=(tokens // DISPATCH_TILE,),
            in_specs=[pl.BlockSpec((DISPATCH_TILE * TOK_ROWS, V7X_LANES), lambda i, *_: (i, 0))],
            out_specs=pl.BlockSpec(memory_space=pl.ANY),
            scratch_shapes=[pltpu.VMEM((SEG_ALIGN * TOK_ROWS, V7X_LANES), F32),
                            pltpu.SemaphoreType.DMA, pltpu.SemaphoreType.DMA],
        ),
        out_shape=jax.ShapeDtypeStruct((n_rows * TOK_ROWS, V7X_LANES), F32),
        compiler_params=pltpu.CompilerParams(
            dimension_semantics=("arbitrary",), has_side_effects=True),
        name="dispatch",
    )(dest_flat, pad_rows, n_pad, h1t)


N_FF = D_FF // FF_TILE
SUB = SEG_ALIGN
N_SUB = SUPER_ROWS // SUB
PREFETCH_SLOTS = -(-N_SUB // N_FF)
OUT_SLOTS = MM_ROWS // SUB


def _experts_kernel(st_e_ref, st_start_ref, st_rows_ref, n_used_ref,
                    xs_ref, w1g_ref, w1u_ref, b1g_ref, b1u_ref, w2_ref, b2_ref,
                    ys_ref,
                    xb_ref, acc_ref, wg_ref, wu_ref, wd_ref, xstage_ref, ostage_ref, xsem, osem):
    s = pl.program_id(0)
    j = pl.program_id(1)
    parity = s % 2
    rows = st_rows_ref[s]
    start = st_start_ref[s]
    n_blk = rows // SUB
    next_start = st_start_ref[s + 1]
    next_blk = st_rows_ref[s + 1] // SUB

    def x_copy(tok0, slot):
        return _token_copy(xs_ref, tok0, xstage_ref.at[slot], 0, xsem.at[slot], SUB)

    def y_copy(slot, tok0):
        return _token_copy(ostage_ref.at[slot], 0, ys_ref, tok0, osem.at[slot], SUB)

    def convert(slot, par, blk):
        off = pl.multiple_of(blk * SUB, SUB)
        for c in range(TOK_ROWS):
            xb_ref[par, pl.ds(off, SUB), c * V7X_LANES:(c + 1) * V7X_LANES] = (
                _load_token_major(xstage_ref.at[slot], 0, SUB, c).astype(BF16))

    @pl.when((s == 0) & (j == 0))
    def _():
        ostage_ref[0] = jnp.zeros(ostage_ref.shape[1:], F32)
        tail_start = n_used_ref[1]
        n_tail = (ys_ref.shape[0] // TOK_ROWS - tail_start) // SUB

        def tstart(b, _):
            y_copy(0, tail_start + b * SUB).start()
            return 0

        def twait(b, _):
            y_copy(0, tail_start + b * SUB).wait()
            return 0

        lax.fori_loop(0, n_tail, tstart, 0)
        lax.fori_loop(0, n_tail, twait, 0)

        def first(b, _):
            x_copy(start + b * SUB, 0).start()
            x_copy(start + b * SUB, 0).wait()
            convert(0, 0, b)
            return 0

        lax.fori_loop(0, n_blk, first, 0)

    for p in range(PREFETCH_SLOTS):
        @pl.when(j * PREFETCH_SLOTS + p < next_blk)
        def _(p=p):
            x_copy(next_start + (j * PREFETCH_SLOTS + p) * SUB, p).start()

    def compute(last):
        def cast_weights():
            wg_ref[...] = w1g_ref[0].astype(BF16)
            wu_ref[...] = w1u_ref[0].astype(BF16)
            wd_ref[...] = w2_ref[0].astype(BF16)

        bg = b1g_ref[0]
        bu = b1u_ref[0]
        b2 = b2_ref[0]

        def chunk(row0, m):
            off = row0 if isinstance(row0, int) else pl.multiple_of(row0, SUB)
            xb = xb_ref[parity, pl.ds(off, m), :]
            g = _dot(xb, wg_ref[...]) + bg
            u = _dot(xb, wu_ref[...]) + bu
            g = jnp.minimum(g, SWIGLU_LIMIT)
            u = jnp.clip(u, -SWIGLU_LIMIT, SWIGLU_LIMIT)
            act = g * jax.nn.sigmoid(SWIGLU_ALPHA * g) * (u + 1.0)
            y = _dot(act.astype(BF16), wd_ref[...])
            if not last:
                acc_ref[pl.ds(off, m), :] += y
                return
            y = acc_ref[pl.ds(off, m), :] + y + b2
            for i in range(m // SUB):
                blk = row0 // SUB + i
                slot = blk % OUT_SLOTS

                @pl.when(blk >= OUT_SLOTS)
                def _(blk=blk, slot=slot):
                    y_copy(slot, start + (blk - OUT_SLOTS) * SUB).wait()

                _store_token_major(ostage_ref.at[slot], 0, y[i * SUB:(i + 1) * SUB, :])
                y_copy(slot, start + blk * SUB).start()

        n_big = rows // MM_ROWS

        def big(r, _):
            chunk(r * MM_ROWS, MM_ROWS)
            return 0

        @pl.when(n_big > 0)
        def _():
            cast_weights()
            chunk(0, MM_ROWS)

        @pl.when(n_big == 0)
        def _():
            cast_weights()

        lax.fori_loop(1, n_big, big, 0)
        done = n_big * MM_ROWS
        m = MM_ROWS // 2
        while m >= SUB:
            take = ((rows - done) & m) != 0

            @pl.when(take)
            def _(done=done, m=m):
                chunk(done, m)

            done = done + jnp.where(take, m, 0)
            m //= 2

    @pl.when((j == 0) & (rows > 0))
    def _():
        def zero(b, _):
            acc_ref[pl.ds(pl.multiple_of(b * SUB, SUB), SUB), :] = jnp.zeros((SUB, D_MODEL), F32)
            return 0

        lax.fori_loop(0, n_blk, zero, 0)

    @pl.when((j < N_FF - 1) & (rows > 0))
    def _():
        compute(False)

    @pl.when((j == N_FF - 1) & (rows > 0))
    def _():
        compute(True)

    for p in range(PREFETCH_SLOTS):
        @pl.when(j * PREFETCH_SLOTS + p < next_blk)
        def _(p=p):
            blk = j * PREFETCH_SLOTS + p
            x_copy(next_start + blk * SUB, p).wait()
            convert(p, 1 - parity, blk)

    def drain(tok0, blocks):
        for slot in range(OUT_SLOTS):
            @pl.when(slot < blocks)
            def _(slot=slot):
                y_copy(slot, tok0 + _last_block_on_slot(blocks, slot) * SUB).wait()

    @pl.when((j == 0) & (s > 0))
    def _():
        prev = jnp.maximum(s - 1, 0)
        drain(st_start_ref[prev], st_rows_ref[prev] // SUB)

    @pl.when((j == N_FF - 1) & (s == pl.num_programs(0) - 1))
    def _():
        drain(start, n_blk)


def _last_block_on_slot(n_blk, slot):
    return slot + OUT_SLOTS * ((n_blk - 1 - slot) // OUT_SLOTS)


def _experts(st_e, st_start, st_rows, n_used, xs, w1, b1, w2, b2, n_super):
    n_rows = xs.shape[0] // TOK_ROWS

    def ff(s, j, n_used_ref):
        return jnp.where(s < n_used_ref[0], j, N_FF - 1)

    w1g_map = lambda s, j, e, st, rw, nu: (e[s], 0, ff(s, j, nu))
    w1u_map = lambda s, j, e, st, rw, nu: (e[s], 0, N_FF + ff(s, j, nu))
    w2_map = lambda s, j, e, st, rw, nu: (e[s], ff(s, j, nu), 0)
    b2_map = lambda s, j, e, st, rw, nu: (e[s], 0, 0)
    return pl.pallas_call(
        _experts_kernel,
        grid_spec=pltpu.PrefetchScalarGridSpec(
            num_scalar_prefetch=4,
            grid=(n_used[0], N_FF),
            in_specs=[
                pl.BlockSpec(memory_space=pl.ANY),
                pl.BlockSpec((1, D_MODEL, FF_TILE), w1g_map),
                pl.BlockSpec((1, D_MODEL, FF_TILE), w1u_map),
                pl.BlockSpec((1, 1, FF_TILE), w1g_map),
                pl.BlockSpec((1, 1, FF_TILE), w1u_map),
                pl.BlockSpec((1, FF_TILE, D_MODEL), w2_map),
                pl.BlockSpec((1, 1, D_MODEL), b2_map),
            ],
            out_specs=pl.BlockSpec(memory_space=pl.ANY),
            scratch_shapes=[
                pltpu.VMEM((2, SUPER_ROWS, D_MODEL), BF16),
                pltpu.VMEM((SUPER_ROWS, D_MODEL), F32),
                pltpu.VMEM((D_MODEL, FF_TILE), BF16),
                pltpu.VMEM((D_MODEL, FF_TILE), BF16),
                pltpu.VMEM((FF_TILE, D_MODEL), BF16),
                pltpu.VMEM((PREFETCH_SLOTS, SUB * TOK_ROWS, V7X_LANES), F32),
                pltpu.VMEM((OUT_SLOTS, SUB * TOK_ROWS, V7X_LANES), F32),
                pltpu.SemaphoreType.DMA((PREFETCH_SLOTS,)),
                pltpu.SemaphoreType.DMA((OUT_SLOTS,)),
            ],
        ),
        out_shape=jax.ShapeDtypeStruct((n_rows * TOK_ROWS, V7X_LANES), F32),
        compiler_params=pltpu.CompilerParams(
            dimension_semantics=("arbitrary", "arbitrary"), vmem_limit_bytes=V7X_VMEM_LIMIT,
            has_side_effects=True),
        name="experts",
    )(st_e, st_start, st_rows, n_used, xs, w1, w1, b1, b1, w2, b2)


def _combine_kernel(dest_ref, ys_ref, gate_ref, h1_ref, g2_ref, b2_ref, o_ref, buf_ref, sem):
    step = pl.program_id(0)
    n_steps = pl.num_programs(0)
    tm = COMBINE_TILE

    def copies(at_step, slot, t):
        base = at_step * (tm * TOP_K)
        return [_token_copy(ys_ref, dest_ref[base + t * TOP_K + k], buf_ref.at[slot, k], t, sem.at[slot],
                            dst_pitch=PADDED_PITCH)
                for k in range(TOP_K)]

    def gather(at_step, slot):
        def start(t, _):
            for k, c in enumerate(copies(at_step, slot, t)):
                c.start(priority=k % DMA_QUEUES)
            return 0

        lax.fori_loop(0, tm, start, 0)

    @pl.when(step == 0)
    def _():
        gather(0, 0)

    @pl.when(step + 1 < n_steps)
    def _():
        gather(step + 1, (step + 1) % 2)

    slot = step % 2

    def wait(t, _):
        for c in copies(step, slot, t):
            c.wait()
        return 0

    lax.fori_loop(0, tm, wait, 0)

    gates = gate_ref[...]
    for c in range(TOK_ROWS):
        z = DEEPNORM_ALPHA * _load_token_major(h1_ref, 0, tm, c)
        for k in range(TOP_K):
            z = z + _load_token_major(buf_ref.at[slot, k], 0, tm, c, PADDED_PITCH) * gates[:, k:k + 1]
        o_ref[:, c * V7X_LANES:(c + 1) * V7X_LANES] = z
    o_ref[...] = _layer_norm(o_ref[...], g2_ref[...], b2_ref[...])


def _combine(dest_flat, ys, gates, h1t, g2, b2):
    tokens = h1t.shape[0] // TOK_ROWS
    tm = COMBINE_TILE
    row = lambda i, *_: (i, 0)
    return pl.pallas_call(
        _combine_kernel,
        grid_spec=pltpu.PrefetchScalarGridSpec(
            num_scalar_prefetch=1,
            grid=(tokens // tm,),
            in_specs=[
                pl.BlockSpec(memory_space=pl.ANY),
                pl.BlockSpec((tm, V7X_LANES), row),
                pl.BlockSpec((tm * TOK_ROWS, V7X_LANES), row),
                pl.BlockSpec((1, D_MODEL), lambda i, *_: (0, 0)),
                pl.BlockSpec((1, D_MODEL), lambda i, *_: (0, 0)),
            ],
            out_specs=pl.BlockSpec((tm, D_MODEL), row),
            scratch_shapes=[pltpu.VMEM((2, TOP_K, tm * PADDED_PITCH, V7X_LANES), F32),
                            pltpu.SemaphoreType.DMA((2,))],
        ),
        out_shape=jax.ShapeDtypeStruct((tokens, D_MODEL), F32),
        compiler_params=pltpu.CompilerParams(
            dimension_semantics=("arbitrary",), vmem_limit_bytes=V7X_VMEM_LIMIT),
        name="combine",
    )(dest_flat, ys, gates, h1t, g2, b2)


def _rotate_half_cols(w):
    half = QK_ROPE // 2
    return jnp.concatenate([-w[..., half:], w[..., :half]], axis=-1)


def _rope_table(length):
    inv_freq = 1.0 / (ROPE_THETA ** (jnp.arange(0, QK_ROPE, 2, dtype=F32) / QK_ROPE))
    freqs = jnp.arange(length, dtype=F32)[:, None] * inv_freq[None, :]
    emb = jnp.concatenate([freqs, freqs], axis=-1)
    return jnp.concatenate([jnp.cos(emb), jnp.sin(emb)], axis=-1)


def _routing_plan(idx, rank, counts, n_super):
    experts = jnp.arange(N_EXPERTS, dtype=jnp.int32)

    def lookup(table, i):
        return jnp.sum(jnp.where(i[..., None] == experts, table, 0), axis=-1)

    def bucket(cum, i):
        return jnp.minimum(jnp.sum((cum <= i[..., None]).astype(jnp.int32), axis=-1), N_EXPERTS - 1)

    counts = counts.astype(jnp.int32)
    padded = (counts + SEG_ALIGN - 1) // SEG_ALIGN * SEG_ALIGN
    pad_end = jnp.cumsum(padded)
    pad_start = pad_end - padded
    dest = (lookup(pad_start, idx) + rank).reshape(-1).astype(jnp.int32)

    n_padmax = N_EXPERTS * SEG_ALIGN
    padcnt = padded - counts
    padcum = jnp.cumsum(padcnt)
    p = jnp.arange(n_padmax, dtype=jnp.int32)
    pe = bucket(padcum, p)
    pad_rows = lookup(pad_start + counts - (padcum - padcnt), pe) + p
    n_pad = jnp.stack([padcum[-1], pad_end[-1]]).astype(jnp.int32)
    pad_rows = jnp.where(p < n_pad[0], pad_rows, 0).astype(jnp.int32)

    n_st = (padded + SUPER_ROWS - 1) // SUPER_ROWS
    st_cum = jnp.cumsum(n_st)
    n_used = jnp.stack([st_cum[-1], pad_end[-1]]).astype(jnp.int32)
    s = jnp.arange(n_super + 1, dtype=jnp.int32)
    s_eff = jnp.minimum(s, n_used[0] - 1)
    se = bucket(st_cum, s_eff).astype(jnp.int32)
    local = s_eff - lookup(st_cum - n_st, se)
    st_start = (lookup(pad_start, se) + local * SUPER_ROWS).astype(jnp.int32)
    st_rows = jnp.clip(lookup(padded, se) - local * SUPER_ROWS, 0, SUPER_ROWS)
    st_rows = jnp.where(s < n_used[0], st_rows, 0).astype(jnp.int32)
    return dest, pad_rows, n_pad, se, st_start, st_rows, n_used


def kernel(x, meta_tokens, ln_in_g, ln_in_b, w_in, q_norm_g, w_uq, kv_norm_g, w_uk, w_uv, conv_dw_w,
           conv_dw_b, conv_ln_g, conv_ln_b, w_out, ln1_g, ln1_b, w_router, b_router, w_mlp1, b_mlp1,
           w_mlp2, b_mlp2, ln2_g, ln2_b):
    batch, seq, _ = x.shape
    tokens = batch * seq
    row2 = lambda a: a.reshape(1, -1)

    wi = w_in[0]
    s_kpe = Q_LORA + KV_LORA
    s_conv = s_kpe + QK_ROPE
    kpe_w = wi[:, s_kpe:s_conv]
    w_proj = (wi[:, :s_kpe].astype(BF16),
              jnp.concatenate([kpe_w, _rotate_half_cols(kpe_w)], axis=1).astype(BF16),
              wi[:, s_conv:s_conv + CONV_CH].astype(BF16),
              wi[:, s_conv + CONV_CH:].astype(BF16))
    wq3 = w_uq[0].reshape(Q_LORA, N_HEADS, QK_DIM)
    wq_nope = wq3[:, :, :QK_NOPE].reshape(Q_LORA, N_HEADS * QK_NOPE)
    wq_pe = wq3[:, :, QK_NOPE:]
    wq_pr = jnp.concatenate([wq_pe, _rotate_half_cols(wq_pe)], axis=-1).reshape(Q_LORA, N_HEADS * 2 * QK_ROPE)
    wq = jnp.concatenate([wq_nope, wq_pr], axis=1).astype(BF16)
    wuk = w_uk[0].astype(BF16)
    wuv = w_uv[0].astype(BF16)
    wo = w_out[0].astype(BF16)
    wr = w_router[0]
    wr_hi = wr.astype(BF16)
    wr_lo = (wr - wr_hi.astype(F32)).astype(BF16)
    cs = _rope_table(N_META + seq)
    conv_w = jnp.repeat(conv_dw_w[0], V7X_SUBLANES, axis=0)

    x2d = x.reshape(tokens, D_MODEL)
    proj_args = (row2(ln_in_g), row2(ln_in_b), *w_proj, row2(q_norm_g[0]), row2(kv_norm_g[0]), wq, wuk, wuv)

    _, k_meta, v_meta, glu_meta = _in_proj(meta_tokens, *proj_args, cs[:N_META], N_META)
    q, k, v, glu = _in_proj(x2d, *proj_args, cs[N_META:], ROW_TILE)
    k_meta = jnp.pad(k_meta, ((0, 0), (0, 0), (0, V7X_LANES - N_META)))
    v_meta = jnp.pad(v_meta, ((0, 0), (0, V7X_LANES - N_META), (0, 0)))
    attn = _attention(q, k, v, k_meta, v_meta, batch, seq)
    conv = _conv(glu, glu_meta, conv_w, row2(conv_dw_b[0]), row2(conv_ln_g[0]), row2(conv_ln_b[0]), batch, seq)

    h1t, idx, rank, gates, counts = _out_proj(
        attn, conv, x2d, row2(ln_in_g), row2(ln_in_b), wo, row2(ln1_g[0]), row2(ln1_b[0]),
        wr_hi, wr_lo, row2(b_router[0]))

    n_assign = tokens * TOP_K
    n_rows = n_assign + N_EXPERTS * SEG_ALIGN
    n_super = N_EXPERTS + -(-n_assign // SUPER_ROWS)
    dest, pad_rows, n_pad, st_e, st_start, st_rows, n_used = _routing_plan(
        idx[:, :TOP_K], rank[:, :TOP_K], counts[0], n_super)

    xs = _dispatch(dest, pad_rows, n_pad, h1t, n_rows)
    ys = _experts(st_e, st_start, st_rows, n_used, xs, w_mlp1[0], b_mlp1[0].reshape(N_EXPERTS, 1, 2 * D_FF),
                  w_mlp2[0], b_mlp2[0].reshape(N_EXPERTS, 1, D_MODEL), n_super)
    out = _combine(dest, ys, gates, h1t, row2(ln2_g[0]), row2(ln2_b[0]))
    return out.reshape(batch, seq, D_MODEL)
```

```python
import functools
import math

import jax
import jax.numpy as jnp
from jax import lax
from jax.experimental import pallas as pl
from jax.experimental.pallas import tpu as pltpu

D_MODEL = 2048
N_META = 16
N_HEADS = 8
QK_NOPE = 128
QK_ROPE = 64
QK_DIM = QK_NOPE + QK_ROPE
V_DIM = 128
Q_LORA = 768
KV_LORA = 512
ROPE_THETA = 10000.0
MLA_WIDTH = N_HEADS * V_DIM
CONV_CH = 1024
CONV_W = 31
N_EXPERTS = 32
TOP_K = 4
D_FF = 2048
SWIGLU_LIMIT = 7.0
SWIGLU_ALPHA = 1.702
DEEPNORM_ALPHA = 2.0 ** 0.25
LN_EPS = 1e-5
RMS_EPS = 1e-6

V7X_LANES = 128
V7X_SUBLANES = 8
V7X_VMEM_LIMIT = 56 * 1024 * 1024

ROW_TILE = 256
OUT_TILE = 256
OUT_CHAINS = 1
ATT_TILE = 512
ATT_CHAINS = 2
CONV_TILE = 256
CONV_HALO = 32
CONV_ROWS = 32
CONV_LANES = 256
DISPATCH_TILE = 256
COMBINE_TILE = 128
SEG_ALIGN = 128
SUPER_ROWS = 1536
FF_TILE = 256
MM_ROWS = 512

F32 = jnp.float32
BF16 = jnp.bfloat16


def _dot(a, b):
    return jnp.dot(a, b, preferred_element_type=F32)


def _dot_nt(a, b):
    return lax.dot_general(a, b, (((1,), (1,)), ((), ())), preferred_element_type=F32)


def _layer_norm(x, g, b):
    mu = jnp.mean(x, axis=-1, keepdims=True)
    xc = x - mu
    var = jnp.mean(xc * xc, axis=-1, keepdims=True)
    return xc * lax.rsqrt(var + LN_EPS) * g + b


def _rms_norm(x, g):
    ms = jnp.mean(x * x, axis=-1, keepdims=True)
    return x * lax.rsqrt(ms + RMS_EPS) * g


def _const_spec(shape):
    zeros = (0,) * len(shape)
    return pl.BlockSpec(shape, lambda *_: zeros)


TOK_ROWS = D_MODEL // V7X_LANES


def _load_token_major(ref, row0, n_tok, j, pitch=TOK_ROWS):
    return ref[pl.ds(row0 + j, n_tok, stride=pitch), :]


PADDED_PITCH = TOK_ROWS + V7X_SUBLANES


def _store_token_major(ref, row0, x):
    n_tok = x.shape[0]
    for j in range(TOK_ROWS):
        ref[pl.ds(row0 + j, n_tok, stride=TOK_ROWS), :] = x[:, j * V7X_LANES:(j + 1) * V7X_LANES]


def _in_proj_kernel(x_ref, lng_ref, lnb_ref, wc_ref, wkpe_ref, wa_ref, wg_ref, qg_ref, kvg_ref,
                    wq_ref, wuk_ref, wuv_ref, cs_ref,
                    q_ref, k_ref, v_ref, glu_ref):
    h0 = _layer_norm(x_ref[...], lng_ref[...], lnb_ref[...])
    hb = h0.astype(BF16)
    cs = cs_ref[...]

    def rope(t128):
        t = t128 * cs
        return t + pltpu.roll(t, QK_ROPE, axis=1)

    cq = _dot(hb, wc_ref[:, 0:Q_LORA])
    cqn = _rms_norm(cq, qg_ref[...]).astype(BF16)
    ckv = _dot(hb, wc_ref[:, Q_LORA:Q_LORA + KV_LORA])
    ckvn = _rms_norm(ckv, kvg_ref[...]).astype(BF16)
    kpe = rope(_dot(hb, wkpe_ref[...]))

    a = _dot(hb, wa_ref[...])
    g = _dot(hb, wg_ref[...])
    glu_ref[...] = a * jax.nn.sigmoid(g)

    knope = _dot(ckvn, wuk_ref[...])
    v = _dot(ckvn, wuv_ref[...])
    kpe_t = kpe.T[0:QK_ROPE, :].astype(BF16)
    for h in range(N_HEADS):
        k_ref[h, 0:QK_NOPE, :] = knope[:, h * QK_NOPE:(h + 1) * QK_NOPE].T.astype(BF16)
        k_ref[h, QK_NOPE:QK_DIM, :] = kpe_t
        v_ref[h] = v[:, h * V_DIM:(h + 1) * V_DIM].astype(BF16)

    qn = _dot(cqn, wq_ref[:, 0:N_HEADS * QK_NOPE])
    qp = _dot(cqn, wq_ref[:, N_HEADS * QK_NOPE:])
    for h in range(N_HEADS):
        q_ref[h, :, 0:QK_NOPE] = qn[:, h * QK_NOPE:(h + 1) * QK_NOPE].astype(BF16)
        q_ref[h, :, QK_NOPE:QK_DIM] = rope(qp[:, h * V7X_LANES:(h + 1) * V7X_LANES])[:, :QK_ROPE].astype(BF16)


def _in_proj(x2d, lng, lnb, wc, wkpe, wa, wg, qg, kvg, wq, wuk, wuv, cs, tm):
    rows = x2d.shape[0]
    n_cs = cs.shape[0] // tm
    row = lambda i: (i, 0)
    head_row = lambda i: (0, i, 0)
    return pl.pallas_call(
        _in_proj_kernel,
        grid=(rows // tm,),
        in_specs=[
            pl.BlockSpec((tm, D_MODEL), row),
            _const_spec((1, D_MODEL)), _const_spec((1, D_MODEL)),
            _const_spec((D_MODEL, Q_LORA + KV_LORA)), _const_spec((D_MODEL, 2 * QK_ROPE)),
            _const_spec((D_MODEL, CONV_CH)), _const_spec((D_MODEL, CONV_CH)),
            _const_spec((1, Q_LORA)), _const_spec((1, KV_LORA)),
            _const_spec((Q_LORA, 2 * N_HEADS * QK_NOPE)),
            _const_spec((KV_LORA, N_HEADS * QK_NOPE)), _const_spec((KV_LORA, MLA_WIDTH)),
            pl.BlockSpec((tm, V7X_LANES), lambda i: (i % n_cs, 0)),
        ],
        out_specs=[
            pl.BlockSpec((N_HEADS, tm, QK_DIM), head_row),
            pl.BlockSpec((N_HEADS, QK_DIM, tm), lambda i: (0, 0, i)),
            pl.BlockSpec((N_HEADS, tm, V_DIM), head_row),
            pl.BlockSpec((tm, CONV_CH), row),
        ],
        out_shape=[
            jax.ShapeDtypeStruct((N_HEADS, rows, QK_DIM), BF16),
            jax.ShapeDtypeStruct((N_HEADS, QK_DIM, rows), BF16),
            jax.ShapeDtypeStruct((N_HEADS, rows, V_DIM), BF16),
            jax.ShapeDtypeStruct((rows, CONV_CH), F32),
        ],
        compiler_params=pltpu.CompilerParams(
            dimension_semantics=("arbitrary",), vmem_limit_bytes=V7X_VMEM_LIMIT),
        name="in_proj",
    )(x2d, lng, lnb, wc, wkpe, wa, wg, qg, kvg, wq, wuk, wuv, cs)


def _attention_kernel(q_ref, k_ref, v_ref, km_ref, vm_ref, o_ref, *state):
    i = pl.program_id(2)
    c_exp = (1.0 / math.sqrt(QK_DIM)) * math.log2(math.e)
    chain_rows = ATT_TILE // ATT_CHAINS
    chains = range(ATT_CHAINS)
    m_refs, l_refs, acc_refs = (state[n * ATT_CHAINS:(n + 1) * ATT_CHAINS] for n in range(3))

    def lane_tiles(x):
        return [x[:, t * V7X_LANES:(t + 1) * V7X_LANES] for t in range(x.shape[1] // V7X_LANES)]

    def row_max(x):
        if x.shape[1] % V7X_LANES:
            return jnp.max(x, axis=1, keepdims=True)
        return jnp.max(functools.reduce(jnp.maximum, lane_tiles(x)), axis=1, keepdims=True)

    def lane_partial_sum(x):
        if x.shape[1] % V7X_LANES:
            lane = lax.broadcasted_iota(jnp.int32, (x.shape[0], V7X_LANES), 1)
            return jnp.where(lane == 0, jnp.sum(x, axis=1, keepdims=True), 0.0)
        return functools.reduce(jnp.add, lane_tiles(x))

    def update(h, s, vb, first):
        s_max = jnp.broadcast_to(row_max(s), (s.shape[0], V7X_LANES))
        if first:
            m_new = s_max
        else:
            m_old = m_refs[h][...]
            m_new = jnp.maximum(m_old, s_max)
            alpha = jnp.exp2(c_exp * (m_old - m_new))
        if s.shape[1] % V7X_LANES:
            p = jnp.exp2(c_exp * (s - m_new[:, :s.shape[1]]))
        else:
            p = jnp.concatenate([jnp.exp2(c_exp * (t - m_new)) for t in lane_tiles(s)], axis=1)
        p_sum = lane_partial_sum(p)
        pv = _dot(p.astype(BF16), vb)
        m_refs[h][...] = m_new
        if first:
            l_refs[h][...] = p_sum
            acc_refs[h][...] = pv
        else:
            l_refs[h][...] = alpha * l_refs[h][...] + p_sum
            acc_refs[h][...] = alpha * acc_refs[h][...] + pv

    def chain(x, h):
        return x[h * chain_rows:(h + 1) * chain_rows, :]

    q = q_ref[0]

    start = pl.multiple_of(i * ATT_TILE, ATT_TILE)
    s = _dot(q, jnp.concatenate([k_ref[0, :, pl.ds(start, ATT_TILE)], km_ref[0]], axis=1))
    r = lax.broadcasted_iota(jnp.int32, s.shape, 0)
    c = lax.broadcasted_iota(jnp.int32, s.shape, 1)
    last_visible = jnp.where(c >= ATT_TILE, ATT_TILE + N_META - 1, r)
    s = jnp.where(c <= last_visible, s, -1e30)
    for h in chains:
        cols = (h + 1) * chain_rows
        s_h = jnp.concatenate([chain(s, h)[:, :cols], chain(s, h)[:, ATT_TILE:]], axis=1)
        v_h = jnp.concatenate([v_ref[0, pl.ds(start, cols), :], vm_ref[0]], axis=0)
        update(h, s_h, v_h, True)

    def block(j, _):
        start = pl.multiple_of(j * ATT_TILE, ATT_TILE)
        s = _dot(q, k_ref[0, :, pl.ds(start, ATT_TILE)])
        vb = v_ref[0, pl.ds(start, ATT_TILE), :]
        for h in chains:
            update(h, chain(s, h), vb, False)
        return 0

    lax.fori_loop(0, i, block, 0)

    for h in chains:
        l = jnp.sum(l_refs[h][...], axis=1, keepdims=True)
        o_ref[pl.ds(h * chain_rows, chain_rows), :] = (acc_refs[h][...] / l).astype(BF16)


def _attention(q, k, v, km, vm, batch, seq):
    nq = seq // ATT_TILE
    return pl.pallas_call(
        _attention_kernel,
        grid=(batch, N_HEADS, nq),
        in_specs=[
            pl.BlockSpec((1, ATT_TILE, QK_DIM), lambda b, h, i: (h, b * nq + i, 0)),
            pl.BlockSpec((1, QK_DIM, seq), lambda b, h, i: (h, 0, b)),
            pl.BlockSpec((1, seq, V_DIM), lambda b, h, i: (h, b, 0)),
            pl.BlockSpec((1, QK_DIM, V7X_LANES), lambda b, h, i: (h, 0, 0)),
            pl.BlockSpec((1, V7X_LANES, V_DIM), lambda b, h, i: (h, 0, 0)),
        ],
        out_specs=pl.BlockSpec((ATT_TILE, V_DIM), lambda b, h, i: (b * nq + i, h)),
        out_shape=jax.ShapeDtypeStruct((batch * seq, MLA_WIDTH), BF16),
        scratch_shapes=[pltpu.VMEM((ATT_TILE // ATT_CHAINS, V7X_LANES), F32)] * (3 * ATT_CHAINS),
        compiler_params=pltpu.CompilerParams(
            dimension_semantics=("arbitrary", "arbitrary", "arbitrary"), vmem_limit_bytes=V7X_VMEM_LIMIT),
        name="attention",
    )(q, k, v, km, vm)


def _conv_kernel(cur_ref, prev_ref, meta_ref, w_ref, cb_ref, lng_ref, lnb_ref, o_ref,
                 win_ref, shift_ref, acc_ref):
    i = pl.program_id(1)

    @pl.when(i == 0)
    def _():
        win_ref[0:CONV_HALO - N_META, :] = jnp.zeros((CONV_HALO - N_META, CONV_CH), F32)
        win_ref[CONV_HALO - N_META:CONV_HALO, :] = meta_ref[...]

    @pl.when(i > 0)
    def _():
        win_ref[0:CONV_HALO, :] = prev_ref[...]

    win_ref[CONV_HALO:CONV_HALO + CONV_TILE, :] = cur_ref[...]
    win_ref[CONV_HALO + CONV_TILE:, :] = jnp.zeros((V7X_SUBLANES, CONV_CH), F32)

    base = CONV_HALO - (CONV_W - 1)
    win_rows = CONV_HALO + CONV_TILE
    for shift in range(1, V7X_SUBLANES):
        for r in range(0, win_rows, CONV_ROWS):
            shift_ref[shift - 1, pl.ds(r, CONV_ROWS), :] = win_ref[pl.ds(r + shift, CONV_ROWS), :]

    def window(shift, row, lanes):
        if shift == 0:
            return win_ref[pl.ds(row, CONV_ROWS), lanes]
        return shift_ref[shift - 1, pl.ds(row, CONV_ROWS), lanes]

    for rc in range(CONV_TILE // CONV_ROWS):
        r0 = rc * CONV_ROWS
        for c in range(CONV_CH // CONV_LANES):
            lanes = pl.ds(c * CONV_LANES, CONV_LANES)
            acc = jnp.zeros((CONV_ROWS, CONV_LANES), F32)
            for k in range(CONV_W):
                shift = (base + k) % V7X_SUBLANES
                w_k = jnp.concatenate([w_ref[k * V7X_SUBLANES:(k + 1) * V7X_SUBLANES, lanes]]
                                      * (CONV_ROWS // V7X_SUBLANES), axis=0)
                acc = acc + window(shift, r0 + base + k - shift, lanes) * w_k
            acc_ref[pl.ds(r0, CONV_ROWS), lanes] = acc

    y = _layer_norm(acc_ref[...] + cb_ref[...], lng_ref[...], lnb_ref[...])
    o_ref[...] = (y * jax.nn.sigmoid(y)).astype(BF16)


def _conv(glu, glu_meta, w, cb, lng, lnb, batch, seq):
    nt = seq // CONV_TILE
    per = CONV_TILE // CONV_HALO
    return pl.pallas_call(
        _conv_kernel,
        grid=(batch, nt),
        in_specs=[
            pl.BlockSpec((CONV_TILE, CONV_CH), lambda b, i: (b * nt + i, 0)),
            pl.BlockSpec((CONV_HALO, CONV_CH), lambda b, i: (jnp.maximum((b * nt + i) * per - 1, 0), 0)),
            _const_spec((N_META, CONV_CH)),
            _const_spec((CONV_W * V7X_SUBLANES, CONV_CH)),
            _const_spec((1, CONV_CH)), _const_spec((1, CONV_CH)), _const_spec((1, CONV_CH)),
        ],
        out_specs=pl.BlockSpec((CONV_TILE, CONV_CH), lambda b, i: (b * nt + i, 0)),
        out_shape=jax.ShapeDtypeStruct((batch * seq, CONV_CH), BF16),
        scratch_shapes=[pltpu.VMEM((CONV_HALO + CONV_TILE + V7X_SUBLANES, CONV_CH), F32),
                        pltpu.VMEM((V7X_SUBLANES - 1, CONV_HALO + CONV_TILE, CONV_CH), F32),
                        pltpu.VMEM((CONV_TILE, CONV_CH), F32)],
        compiler_params=pltpu.CompilerParams(
            dimension_semantics=("arbitrary", "arbitrary"), vmem_limit_bytes=V7X_VMEM_LIMIT),
        name="conv",
    )(glu, glu, glu_meta, w, cb, lng, lnb)


def _out_proj_kernel(attn_ref, conv_ref, x_ref, lng_ref, lnb_ref, wo_ref, g1_ref, b1_ref,
                     wrh_ref, wrl_ref, br_ref,
                     h1_ref, idx_ref, rank_ref, gate_ref, cnt_ref, carry_ref):
    step = pl.program_id(0)
    tm = x_ref.shape[0] // OUT_CHAINS

    @pl.when(step == 0)
    def _():
        carry_ref[...] = jnp.zeros_like(carry_ref)

    counts = carry_ref[...]
    for ch in range(OUT_CHAINS):
        counts = _route_chain(pl.ds(ch * tm, tm), ch * tm * TOK_ROWS, counts,
                              attn_ref, conv_ref, x_ref, lng_ref, lnb_ref, wo_ref, g1_ref, b1_ref,
                              wrh_ref, wrl_ref, br_ref, h1_ref, idx_ref, rank_ref, gate_ref)
    carry_ref[...] = counts
    cnt_ref[...] = counts.astype(jnp.int32)


def _route_chain(rows, h1_row0, counts, attn_ref, conv_ref, x_ref, lng_ref, lnb_ref, wo_ref, g1_ref, b1_ref,
                 wrh_ref, wrl_ref, br_ref, h1_ref, idx_ref, rank_ref, gate_ref):
    tm = rows.size
    h0 = _layer_norm(x_ref[rows, :], lng_ref[...], lnb_ref[...])
    mix = _dot(attn_ref[rows, :], wo_ref[0:MLA_WIDTH, :]) + _dot(conv_ref[rows, :], wo_ref[MLA_WIDTH:, :])
    h1 = _layer_norm(DEEPNORM_ALPHA * h0 + mix, g1_ref[...], b1_ref[...])
    _store_token_major(h1_ref, h1_row0, h1)

    hi = h1.astype(BF16)
    lo = (h1 - hi.astype(F32)).astype(BF16)
    logits = (_dot(hi, wrh_ref[...]) + (_dot(hi, wrl_ref[...]) + _dot(lo, wrh_ref[...]))) + br_ref[...]

    lane = lax.broadcasted_iota(jnp.int32, (tm, N_EXPERTS), 1)
    work = logits
    vals, idxs = [], []
    for _ in range(TOP_K):
        mx = jnp.max(work, axis=1, keepdims=True)
        ix = jnp.min(jnp.where(work == mx, lane, N_EXPERTS), axis=1, keepdims=True)
        vals.append(mx)
        idxs.append(ix)
        work = jnp.where(lane == ix, -jnp.inf, work)
    exps = [jnp.exp(v - vals[0]) for v in vals]
    denom = exps[0] + exps[1] + exps[2] + exps[3]

    onehots = [(lane == ix) for ix in idxs]
    chosen = (onehots[0] | onehots[1] | onehots[2] | onehots[3])
    chosen_f = jnp.where(chosen, 1.0, 0.0)
    r = lax.broadcasted_iota(jnp.int32, (tm, tm), 0)
    c = lax.broadcasted_iota(jnp.int32, (tm, tm), 1)
    lower = jnp.where(c < r, 1.0, 0.0).astype(BF16)
    before = _dot(lower, chosen_f.astype(BF16)) + counts

    out_lane = lax.broadcasted_iota(jnp.int32, (tm, V7X_LANES), 1)
    idx_out = jnp.zeros((tm, V7X_LANES), jnp.int32)
    rank_out = jnp.zeros((tm, V7X_LANES), jnp.int32)
    gate_out = jnp.zeros((tm, V7X_LANES), F32)
    for k in range(TOP_K):
        rank_k = jnp.sum(jnp.where(onehots[k], before, 0.0), axis=1, keepdims=True).astype(jnp.int32)
        idx_out = jnp.where(out_lane == k, idxs[k], idx_out)
        rank_out = jnp.where(out_lane == k, rank_k, rank_out)
        gate_out = jnp.where(out_lane == k, exps[k] / denom, gate_out)
    idx_ref[rows, :] = idx_out
    rank_ref[rows, :] = rank_out
    gate_ref[rows, :] = gate_out
    return counts + jnp.sum(chosen_f, axis=0, keepdims=True)


def _out_proj(attn, conv, x2d, lng, lnb, wo, g1, b1, wrh, wrl, br):
    rows = x2d.shape[0]
    tm = OUT_TILE
    row = lambda i: (i, 0)
    return pl.pallas_call(
        _out_proj_kernel,
        grid=(rows // tm,),
        in_specs=[
            pl.BlockSpec((tm, MLA_WIDTH), row), pl.BlockSpec((tm, CONV_CH), row),
            pl.BlockSpec((tm, D_MODEL), row),
            _const_spec((1, D_MODEL)), _const_spec((1, D_MODEL)),
            _const_spec((D_MODEL, D_MODEL)),
            _const_spec((1, D_MODEL)), _const_spec((1, D_MODEL)),
            _const_spec((D_MODEL, N_EXPERTS)), _const_spec((D_MODEL, N_EXPERTS)),
            _const_spec((1, N_EXPERTS)),
        ],
        out_specs=[
            pl.BlockSpec((tm * TOK_ROWS, V7X_LANES), row),
            pl.BlockSpec((tm, V7X_LANES), row), pl.BlockSpec((tm, V7X_LANES), row),
            pl.BlockSpec((tm, V7X_LANES), row),
            _const_spec((1, N_EXPERTS)),
        ],
        out_shape=[
            jax.ShapeDtypeStruct((rows * TOK_ROWS, V7X_LANES), F32),
            jax.ShapeDtypeStruct((rows, V7X_LANES), jnp.int32),
            jax.ShapeDtypeStruct((rows, V7X_LANES), jnp.int32),
            jax.ShapeDtypeStruct((rows, V7X_LANES), F32),
            jax.ShapeDtypeStruct((1, N_EXPERTS), jnp.int32),
        ],
        scratch_shapes=[pltpu.VMEM((1, N_EXPERTS), F32)],
        compiler_params=pltpu.CompilerParams(
            dimension_semantics=("arbitrary",), vmem_limit_bytes=V7X_VMEM_LIMIT),
        name="out_proj_router",
    )(attn, conv, x2d, lng, lnb, wo, g1, b1, wrh, wrl, br)


def _token_copy(src_ref, src_tok, dst_ref, dst_tok, sem, n_tok=1, dst_pitch=TOK_ROWS):
    assert n_tok == 1 or dst_pitch == TOK_ROWS
    rows = n_tok * TOK_ROWS
    src = src_ref.at[pl.ds(pl.multiple_of(src_tok * TOK_ROWS, V7X_SUBLANES), rows), :]
    dst = dst_ref.at[pl.ds(pl.multiple_of(dst_tok * dst_pitch, V7X_SUBLANES), rows), :]
    return pltpu.make_async_copy(src, dst, sem)


def _dispatch_kernel(dest_ref, padrow_ref, npad_ref, h1_ref, xs_ref, zero_ref, sem, zsem):
    step = pl.program_id(0)
    base = step * (DISPATCH_TILE * TOP_K)

    def copies(t):
        return [_token_copy(h1_ref, t, xs_ref, dest_ref[base + t * TOP_K + k], sem) for k in range(TOP_K)]

    def start(t, _):
        for c in copies(t):
            c.start()
        return 0

    def wait(t, _):
        for c in copies(t):
            c.wait()
        return 0

    lax.fori_loop(0, DISPATCH_TILE, start, 0)

    @pl.when(step == 0)
    def _():
        zero_ref[...] = jnp.zeros_like(zero_ref)
        n = npad_ref[0]
        tail_start = npad_ref[1]
        n_tail = (xs_ref.shape[0] // TOK_ROWS - tail_start) // SEG_ALIGN

        def zstart(p, _):
            _token_copy(zero_ref, 0, xs_ref, padrow_ref[p], zsem).start()
            return 0

        def zwait(p, _):
            _token_copy(zero_ref, 0, xs_ref, padrow_ref[p], zsem).wait()
            return 0

        def tstart(b, _):
            _token_copy(zero_ref, 0, xs_ref, tail_start + b * SEG_ALIGN, zsem, SEG_ALIGN).start()
            return 0

        def twait(b, _):
            _token_copy(zero_ref, 0, xs_ref, tail_start + b * SEG_ALIGN, zsem, SEG_ALIGN).wait()
            return 0

        lax.fori_loop(0, n, zstart, 0)
        lax.fori_loop(0, n, zwait, 0)
        lax.fori_loop(0, n_tail, tstart, 0)
        lax.fori_loop(0, n_tail, twait, 0)

    lax.fori_loop(0, DISPATCH_TILE, wait, 0)


def _dispatch(dest_flat, pad_rows, n_pad, h1t, n_rows):
    tokens = h1t.shape[0] // TOK_ROWS
    return pl.pallas_call(
        _dispatch_kernel,
        grid_spec=pltpu.PrefetchScalarGridSpec(
            num_scalar_prefetch=3,
            grid=(tokens // DISPATCH_TILE,),
            in_specs=[pl.BlockSpec((DISPATCH_TILE * TOK_ROWS, V7X_LANES), lambda i, *_: (i, 0))],
            out_specs=pl.BlockSpec(memory_space=pl.ANY),
            scratch_shapes=[pltpu.VMEM((SEG_ALIGN * TOK_ROWS, V7X_LANES), F32),
                            pltpu.SemaphoreType.DMA, pltpu.SemaphoreType.DMA],
        ),
        out_shape=jax.ShapeDtypeStruct((n_rows * TOK_ROWS, V7X_LANES), F32),
        compiler_params=pltpu.CompilerParams(
            dimension_semantics=("arbitrary",), has_side_effects=True),
        name="dispatch",
    )(dest_flat, pad_rows, n_pad, h1t)


N_FF = D_FF // FF_TILE
SUB = SEG_ALIGN
N_SUB = SUPER_ROWS // SUB
PREFETCH_SLOTS = -(-N_SUB // N_FF)
OUT_SLOTS = MM_ROWS // SUB


def _experts_kernel(st_e_ref, st_start_ref, st_rows_ref, n_used_ref,
                    xs_ref, w1g_ref, w1u_ref, b1g_ref, b1u_ref, w2_ref, b2_ref,
                    ys_ref,
                    xb_ref, acc_ref, wg_ref, wu_ref, wd_ref, xstage_ref, ostage_ref, xsem, osem):
    s = pl.program_id(0)
    j = pl.program_id(1)
    parity = s % 2
    rows = st_rows_ref[s]
    start = st_start_ref[s]
    n_blk = rows // SUB
    next_start = st_start_ref[s + 1]
    next_blk = st_rows_ref[s + 1] // SUB

    def x_copy(tok0, slot):
        return _token_copy(xs_ref, tok0, xstage_ref.at[slot], 0, xsem.at[slot], SUB)

    def y_copy(slot, tok0):
        return _token_copy(ostage_ref.at[slot], 0, ys_ref, tok0, osem.at[slot], SUB)

    def convert(slot, par, blk):
        off = pl.multiple_of(blk * SUB, SUB)
        for c in range(TOK_ROWS):
            xb_ref[par, pl.ds(off, SUB), c * V7X_LANES:(c + 1) * V7X_LANES] = (
                _load_token_major(xstage_ref.at[slot], 0, SUB, c).astype(BF16))

    @pl.when((s == 0) & (j == 0))
    def _():
        ostage_ref[0] = jnp.zeros(ostage_ref.shape[1:], F32)
        tail_start = n_used_ref[1]
        n_tail = (ys_ref.shape[0] // TOK_ROWS - tail_start) // SUB

        def tstart(b, _):
            y_copy(0, tail_start + b * SUB).start()
            return 0

        def twait(b, _):
            y_copy(0, tail_start + b * SUB).wait()
            return 0

        lax.fori_loop(0, n_tail, tstart, 0)
        lax.fori_loop(0, n_tail, twait, 0)

        def first(b, _):
            x_copy(start + b * SUB, 0).start()
            x_copy(start + b * SUB, 0).wait()
            convert(0, 0, b)
            return 0

        lax.fori_loop(0, n_blk, first, 0)

    for p in range(PREFETCH_SLOTS):
        @pl.when(j * PREFETCH_SLOTS + p < next_blk)
        def _(p=p):
            x_copy(next_start + (j * PREFETCH_SLOTS + p) * SUB, p).start()

    def compute(last):
        def cast_weights():
            wg_ref[...] = w1g_ref[0].astype(BF16)
            wu_ref[...] = w1u_ref[0].astype(BF16)
            wd_ref[...] = w2_ref[0].astype(BF16)

        bg = b1g_ref[0]
        bu = b1u_ref[0]
        b2 = b2_ref[0]

        def chunk(row0, m):
            off = row0 if isinstance(row0, int) else pl.multiple_of(row0, SUB)
            xb = xb_ref[parity, pl.ds(off, m), :]
            g = _dot(xb, wg_ref[...]) + bg
            u = _dot(xb, wu_ref[...]) + bu
            g = jnp.minimum(g, SWIGLU_LIMIT)
            u = jnp.clip(u, -SWIGLU_LIMIT, SWIGLU_LIMIT)
            act = g * jax.nn.sigmoid(SWIGLU_ALPHA * g) * (u + 1.0)
            y = _dot(act.astype(BF16), wd_ref[...])
            if not last:
                acc_ref[pl.ds(off, m), :] += y
                return
            y = acc_ref[pl.ds(off, m), :] + y + b2
            for i in range(m // SUB):
                blk = row0 // SUB + i
                slot = blk % OUT_SLOTS

                @pl.when(blk >= OUT_SLOTS)
                def _(blk=blk, slot=slot):
                    y_copy(slot, start + (blk - OUT_SLOTS) * SUB).wait()

                _store_token_major(ostage_ref.at[slot], 0, y[i * SUB:(i + 1) * SUB, :])
                y_copy(slot, start + blk * SUB).start()

        n_big = rows // MM_ROWS

        def big(r, _):
            chunk(r * MM_ROWS, MM_ROWS)
            return 0

        @pl.when(n_big >= 2)
        def _():
            cast_weights()
            chunk(0, MM_ROWS)
            chunk(MM_ROWS, MM_ROWS)

        @pl.when(n_big == 1)
        def _():
            cast_weights()
            chunk(0, MM_ROWS)

        @pl.when(n_big == 0)
        def _():
            cast_weights()

        lax.fori_loop(2, n_big, big, 0)
        done = n_big * MM_ROWS
        m = MM_ROWS // 2
        while m >= SUB:
            take = ((rows - done) & m) != 0

            @pl.when(take)
            def _(done=done, m=m):
                chunk(done, m)

            done = done + jnp.where(take, m, 0)
            m //= 2

    @pl.when((j == 0) & (rows > 0))
    def _():
        def zero(b, _):
            acc_ref[pl.ds(pl.multiple_of(b * SUB, SUB), SUB), :] = jnp.zeros((SUB, D_MODEL), F32)
            return 0

        lax.fori_loop(0, n_blk, zero, 0)

    @pl.when((j < N_FF - 1) & (rows > 0))
    def _():
        compute(False)

    @pl.when((j == N_FF - 1) & (rows > 0))
    def _():
        compute(True)

    for p in range(PREFETCH_SLOTS):
        @pl.when(j * PREFETCH_SLOTS + p < next_blk)
        def _(p=p):
            blk = j * PREFETCH_SLOTS + p
            x_copy(next_start + blk * SUB, p).wait()
            convert(p, 1 - parity, blk)

    def drain(tok0, blocks):
        for slot in range(OUT_SLOTS):
            @pl.when(slot < blocks)
            def _(slot=slot):
                y_copy(slot, tok0 + _last_block_on_slot(blocks, slot) * SUB).wait()

    @pl.when((j == 0) & (s > 0))
    def _():
        prev = jnp.maximum(s - 1, 0)
        drain(st_start_ref[prev], st_rows_ref[prev] // SUB)

    @pl.when((j == N_FF - 1) & (s == pl.num_programs(0) - 1))
    def _():
        drain(start, n_blk)


def _last_block_on_slot(n_blk, slot):
    return slot + OUT_SLOTS * ((n_blk - 1 - slot) // OUT_SLOTS)


def _experts(st_e, st_start, st_rows, n_used, xs, w1, b1, w2, b2, n_super):
    n_rows = xs.shape[0] // TOK_ROWS

    def ff(s, j, n_used_ref):
        return jnp.where(s < n_used_ref[0], j, N_FF - 1)

    w1g_map = lambda s, j, e, st, rw, nu: (e[s], 0, ff(s, j, nu))
    w1u_map = lambda s, j, e, st, rw, nu: (e[s], 0, N_FF + ff(s, j, nu))
    w2_map = lambda s, j, e, st, rw, nu: (e[s], ff(s, j, nu), 0)
    b2_map = lambda s, j, e, st, rw, nu: (e[s], 0, 0)
    return pl.pallas_call(
        _experts_kernel,
        grid_spec=pltpu.PrefetchScalarGridSpec(
            num_scalar_prefetch=4,
            grid=(n_used[0], N_FF),
            in_specs=[
                pl.BlockSpec(memory_space=pl.ANY),
                pl.BlockSpec((1, D_MODEL, FF_TILE), w1g_map),
                pl.BlockSpec((1, D_MODEL, FF_TILE), w1u_map),
                pl.BlockSpec((1, 1, FF_TILE), w1g_map),
                pl.BlockSpec((1, 1, FF_TILE), w1u_map),
                pl.BlockSpec((1, FF_TILE, D_MODEL), w2_map),
                pl.BlockSpec((1, 1, D_MODEL), b2_map),
            ],
            out_specs=pl.BlockSpec(memory_space=pl.ANY),
            scratch_shapes=[
                pltpu.VMEM((2, SUPER_ROWS, D_MODEL), BF16),
                pltpu.VMEM((SUPER_ROWS, D_MODEL), F32),
                pltpu.VMEM((D_MODEL, FF_TILE), BF16),
                pltpu.VMEM((D_MODEL, FF_TILE), BF16),
                pltpu.VMEM((FF_TILE, D_MODEL), BF16),
                pltpu.VMEM((PREFETCH_SLOTS, SUB * TOK_ROWS, V7X_LANES), F32),
                pltpu.VMEM((OUT_SLOTS, SUB * TOK_ROWS, V7X_LANES), F32),
                pltpu.SemaphoreType.DMA((PREFETCH_SLOTS,)),
                pltpu.SemaphoreType.DMA((OUT_SLOTS,)),
            ],
        ),
        out_shape=jax.ShapeDtypeStruct((n_rows * TOK_ROWS, V7X_LANES), F32),
        compiler_params=pltpu.CompilerParams(
            dimension_semantics=("arbitrary", "arbitrary"), vmem_limit_bytes=V7X_VMEM_LIMIT,
            has_side_effects=True),
        name="experts",
    )(st_e, st_start, st_rows, n_used, xs, w1, w1, b1, b1, w2, b2)


def _combine_kernel(dest_ref, ys_ref, gate_ref, h1_ref, g2_ref, b2_ref, o_ref, buf_ref, sem):
    step = pl.program_id(0)
    n_steps = pl.num_programs(0)
    tm = COMBINE_TILE

    def copies(at_step, slot, t):
        base = at_step * (tm * TOP_K)
        return [_token_copy(ys_ref, dest_ref[base + t * TOP_K + k], buf_ref.at[slot, k], t, sem.at[slot],
                            dst_pitch=PADDED_PITCH)
                for k in range(TOP_K)]

    def gather(at_step, slot):
        def start(t, _):
            for c in copies(at_step, slot, t):
                c.start()
            return 0

        lax.fori_loop(0, tm, start, 0)

    @pl.when(step == 0)
    def _():
        gather(0, 0)

    @pl.when(step + 1 < n_steps)
    def _():
        gather(step + 1, (step + 1) % 2)

    slot = step % 2

    def wait(t, _):
        for c in copies(step, slot, t):
            c.wait()
        return 0

    lax.fori_loop(0, tm, wait, 0)

    gates = gate_ref[...]
    for c in range(TOK_ROWS):
        z = DEEPNORM_ALPHA * _load_token_major(h1_ref, 0, tm, c)
        for k in range(TOP_K):
            z = z + _load_token_major(buf_ref.at[slot, k], 0, tm, c, PADDED_PITCH) * gates[:, k:k + 1]
        o_ref[:, c * V7X_LANES:(c + 1) * V7X_LANES] = z
    o_ref[...] = _layer_norm(o_ref[...], g2_ref[...], b2_ref[...])


def _combine(dest_flat, ys, gates, h1t, g2, b2):
    tokens = h1t.shape[0] // TOK_ROWS
    tm = COMBINE_TILE
    row = lambda i, *_: (i, 0)
    return pl.pallas_call(
        _combine_kernel,
        grid_spec=pltpu.PrefetchScalarGridSpec(
            num_scalar_prefetch=1,
            grid=(tokens // tm,),
            in_specs=[
                pl.BlockSpec(memory_space=pl.ANY),
                pl.BlockSpec((tm, V7X_LANES), row),
                pl.BlockSpec((tm * TOK_ROWS, V7X_LANES), row),
                pl.BlockSpec((1, D_MODEL), lambda i, *_: (0, 0)),
                pl.BlockSpec((1, D_MODEL), lambda i, *_: (0, 0)),
            ],
            out_specs=pl.BlockSpec((tm, D_MODEL), row),
            scratch_shapes=[pltpu.VMEM((2, TOP_K, tm * PADDED_PITCH, V7X_LANES), F32),
                            pltpu.SemaphoreType.DMA((2,))],
        ),
        out_shape=jax.ShapeDtypeStruct((tokens, D_MODEL), F32),
        compiler_params=pltpu.CompilerParams(
            dimension_semantics=("arbitrary",), vmem_limit_bytes=V7X_VMEM_LIMIT),
        name="combine",
    )(dest_flat, ys, gates, h1t, g2, b2)


def _rotate_half_cols(w):
    half = QK_ROPE // 2
    return jnp.concatenate([-w[..., half:], w[..., :half]], axis=-1)


def _rope_table(length):
    inv_freq = 1.0 / (ROPE_THETA ** (jnp.arange(0, QK_ROPE, 2, dtype=F32) / QK_ROPE))
    freqs = jnp.arange(length, dtype=F32)[:, None] * inv_freq[None, :]
    emb = jnp.concatenate([freqs, freqs], axis=-1)
    return jnp.concatenate([jnp.cos(emb), jnp.sin(emb)], axis=-1)


def _routing_plan(idx, rank, counts, n_super):
    experts = jnp.arange(N_EXPERTS, dtype=jnp.int32)

    def lookup(table, i):
        return jnp.sum(jnp.where(i[..., None] == experts, table, 0), axis=-1)

    def bucket(cum, i):
        return jnp.minimum(jnp.sum((cum <= i[..., None]).astype(jnp.int32), axis=-1), N_EXPERTS - 1)

    counts = counts.astype(jnp.int32)
    padded = (counts + SEG_ALIGN - 1) // SEG_ALIGN * SEG_ALIGN
    pad_end = jnp.cumsum(padded)
    pad_start = pad_end - padded
    dest = (lookup(pad_start, idx) + rank).reshape(-1).astype(jnp.int32)

    n_padmax = N_EXPERTS * SEG_ALIGN
    padcnt = padded - counts
    padcum = jnp.cumsum(padcnt)
    p = jnp.arange(n_padmax, dtype=jnp.int32)
    pe = bucket(padcum, p)
    pad_rows = lookup(pad_start + counts - (padcum - padcnt), pe) + p
    n_pad = jnp.stack([padcum[-1], pad_end[-1]]).astype(jnp.int32)
    pad_rows = jnp.where(p < n_pad[0], pad_rows, 0).astype(jnp.int32)

    n_st = (padded + SUPER_ROWS - 1) // SUPER_ROWS
    st_cum = jnp.cumsum(n_st)
    n_used = jnp.stack([st_cum[-1], pad_end[-1]]).astype(jnp.int32)
    s = jnp.arange(n_super + 1, dtype=jnp.int32)
    s_eff = jnp.minimum(s, n_used[0] - 1)
    se = bucket(st_cum, s_eff).astype(jnp.int32)
    local = s_eff - lookup(st_cum - n_st, se)
    st_start = (lookup(pad_start, se) + local * SUPER_ROWS).astype(jnp.int32)
    st_rows = jnp.clip(lookup(padded, se) - local * SUPER_ROWS, 0, SUPER_ROWS)
    st_rows = jnp.where(s < n_used[0], st_rows, 0).astype(jnp.int32)
    return dest, pad_rows, n_pad, se, st_start, st_rows, n_used


def kernel(x, meta_tokens, ln_in_g, ln_in_b, w_in, q_norm_g, w_uq, kv_norm_g, w_uk, w_uv, conv_dw_w,
           conv_dw_b, conv_ln_g, conv_ln_b, w_out, ln1_g, ln1_b, w_router, b_router, w_mlp1, b_mlp1,
           w_mlp2, b_mlp2, ln2_g, ln2_b):
    batch, seq, _ = x.shape
    tokens = batch * seq
    row2 = lambda a: a.reshape(1, -1)

    wi = w_in[0]
    s_kpe = Q_LORA + KV_LORA
    s_conv = s_kpe + QK_ROPE
    kpe_w = wi[:, s_kpe:s_conv]
    w_proj = (wi[:, :s_kpe].astype(BF16),
              jnp.concatenate([kpe_w, _rotate_half_cols(kpe_w)], axis=1).astype(BF16),
              wi[:, s_conv:s_conv + CONV_CH].astype(BF16),
              wi[:, s_conv + CONV_CH:].astype(BF16))
    wq3 = w_uq[0].reshape(Q_LORA, N_HEADS, QK_DIM)
    wq_nope = wq3[:, :, :QK_NOPE].reshape(Q_LORA, N_HEADS * QK_NOPE)
    wq_pe = wq3[:, :, QK_NOPE:]
    wq_pr = jnp.concatenate([wq_pe, _rotate_half_cols(wq_pe)], axis=-1).reshape(Q_LORA, N_HEADS * 2 * QK_ROPE)
    wq = jnp.concatenate([wq_nope, wq_pr], axis=1).astype(BF16)
    wuk = w_uk[0].astype(BF16)
    wuv = w_uv[0].astype(BF16)
    wo = w_out[0].astype(BF16)
    wr = w_router[0]
    wr_hi = wr.astype(BF16)
    wr_lo = (wr - wr_hi.astype(F32)).astype(BF16)
    cs = _rope_table(N_META + seq)
    conv_w = jnp.repeat(conv_dw_w[0], V7X_SUBLANES, axis=0)

    x2d = x.reshape(tokens, D_MODEL)
    proj_args = (row2(ln_in_g), row2(ln_in_b), *w_proj, row2(q_norm_g[0]), row2(kv_norm_g[0]), wq, wuk, wuv)

    _, k_meta, v_meta, glu_meta = _in_proj(meta_tokens, *proj_args, cs[:N_META], N_META)
    q, k, v, glu = _in_proj(x2d, *proj_args, cs[N_META:], ROW_TILE)
    k_meta = jnp.pad(k_meta, ((0, 0), (0, 0), (0, V7X_LANES - N_META)))
    v_meta = jnp.pad(v_meta, ((0, 0), (0, V7X_LANES - N_META), (0, 0)))
    attn = _attention(q, k, v, k_meta, v_meta, batch, seq)
    conv = _conv(glu, glu_meta, conv_w, row2(conv_dw_b[0]), row2(conv_ln_g[0]), row2(conv_ln_b[0]), batch, seq)

    h1t, idx, rank, gates, counts = _out_proj(
        attn, conv, x2d, row2(ln_in_g), row2(ln_in_b), wo, row2(ln1_g[0]), row2(ln1_b[0]),
        wr_hi, wr_lo, row2(b_router[0]))

    n_assign = tokens * TOP_K
    n_rows = n_assign + N_EXPERTS * SEG_ALIGN
    n_super = N_EXPERTS + -(-n_assign // SUPER_ROWS)
    dest, pad_rows, n_pad, st_e, st_start, st_rows, n_used = _routing_plan(
        idx[:, :TOP_K], rank[:, :TOP_K], counts[0], n_super)

    xs = _dispatch(dest, pad_rows, n_pad, h1t, n_rows)
    ys = _experts(st_e, st_start, st_rows, n_used, xs, w_mlp1[0], b_mlp1[0].reshape(N_EXPERTS, 1, 2 * D_FF),
                  w_mlp2[0], b_mlp2[0].reshape(N_EXPERTS, 1, D_MODEL), n_super)
    out = _combine(dest, ys, gates, h1t, row2(ln2_g[0]), row2(ln2_b[0]))
    return out.reshape(batch, seq, D_MODEL)
```

```python
import functools
import math

import jax
import jax.numpy as jnp
from jax import lax
from jax.experimental import pallas as pl
from jax.experimental.pallas import tpu as pltpu

D_MODEL = 2048
N_META = 16
N_HEADS = 8
QK_NOPE = 128
QK_ROPE = 64
QK_DIM = QK_NOPE + QK_ROPE
V_DIM = 128
Q_LORA = 768
KV_LORA = 512
ROPE_THETA = 10000.0
MLA_WIDTH = N_HEADS * V_DIM
CONV_CH = 1024
CONV_W = 31
N_EXPERTS = 32
TOP_K = 4
D_FF = 2048
SWIGLU_LIMIT = 7.0
SWIGLU_ALPHA = 1.702
DEEPNORM_ALPHA = 2.0 ** 0.25
LN_EPS = 1e-5
RMS_EPS = 1e-6

V7X_LANES = 128
V7X_SUBLANES = 8
V7X_VMEM_LIMIT = 56 * 1024 * 1024

ROW_TILE = 256
OUT_TILE = 256
OUT_CHAINS = 1
ATT_TILE = 512
ATT_CHAINS = 2
CONV_TILE = 256
CONV_HALO = 32
CONV_ROWS = 32
CONV_LANES = 256
DISPATCH_TILE = 512
COMBINE_TILE = 256
SEG_ALIGN = 128
SUPER_ROWS = 1536
FF_TILE = 256
MM_ROWS = 512

F32 = jnp.float32
BF16 = jnp.bfloat16


def _dot(a, b):
    return jnp.dot(a, b, preferred_element_type=F32)


def _dot_nt(a, b):
    return lax.dot_general(a, b, (((1,), (1,)), ((), ())), preferred_element_type=F32)


def _layer_norm(x, g, b):
    mu = jnp.mean(x, axis=-1, keepdims=True)
    xc = x - mu
    var = jnp.mean(xc * xc, axis=-1, keepdims=True)
    return xc * lax.rsqrt(var + LN_EPS) * g + b


def _rms_norm(x, g):
    ms = jnp.mean(x * x, axis=-1, keepdims=True)
    return x * lax.rsqrt(ms + RMS_EPS) * g


def _const_spec(shape):
    zeros = (0,) * len(shape)
    return pl.BlockSpec(shape, lambda *_: zeros)


TOK_ROWS = D_MODEL // V7X_LANES


def _load_token_major(ref, row0, n_tok, j, pitch=TOK_ROWS):
    return ref[pl.ds(row0 + j, n_tok, stride=pitch), :]


PADDED_PITCH = TOK_ROWS + V7X_SUBLANES


def _store_token_major(ref, row0, x):
    n_tok = x.shape[0]
    for j in range(TOK_ROWS):
        ref[pl.ds(row0 + j, n_tok, stride=TOK_ROWS), :] = x[:, j * V7X_LANES:(j + 1) * V7X_LANES]


def _in_proj_kernel(x_ref, lng_ref, lnb_ref, wc_ref, wkpe_ref, wa_ref, wg_ref, qg_ref, kvg_ref,
                    wq_ref, wuk_ref, wuv_ref, cs_ref,
                    q_ref, k_ref, v_ref, glu_ref):
    h0 = _layer_norm(x_ref[...], lng_ref[...], lnb_ref[...])
    hb = h0.astype(BF16)
    cs = cs_ref[...]

    def rope(t128):
        t = t128 * cs
        return t + pltpu.roll(t, QK_ROPE, axis=1)

    cq = _dot(hb, wc_ref[:, 0:Q_LORA])
    cqn = _rms_norm(cq, qg_ref[...]).astype(BF16)
    ckv = _dot(hb, wc_ref[:, Q_LORA:Q_LORA + KV_LORA])
    ckvn = _rms_norm(ckv, kvg_ref[...]).astype(BF16)
    kpe = rope(_dot(hb, wkpe_ref[...]))

    a = _dot(hb, wa_ref[...])
    g = _dot(hb, wg_ref[...])
    glu_ref[...] = a * jax.nn.sigmoid(g)

    knope = _dot(ckvn, wuk_ref[...])
    v = _dot(ckvn, wuv_ref[...])
    kpe_t = kpe.T[0:QK_ROPE, :].astype(BF16)
    for h in range(N_HEADS):
        k_ref[h, 0:QK_NOPE, :] = knope[:, h * QK_NOPE:(h + 1) * QK_NOPE].T.astype(BF16)
        k_ref[h, QK_NOPE:QK_DIM, :] = kpe_t
        v_ref[h] = v[:, h * V_DIM:(h + 1) * V_DIM].astype(BF16)

    qn = _dot(cqn, wq_ref[:, 0:N_HEADS * QK_NOPE])
    qp = _dot(cqn, wq_ref[:, N_HEADS * QK_NOPE:])
    for h in range(N_HEADS):
        q_ref[h, :, 0:QK_NOPE] = qn[:, h * QK_NOPE:(h + 1) * QK_NOPE].astype(BF16)
        q_ref[h, :, QK_NOPE:QK_DIM] = rope(qp[:, h * V7X_LANES:(h + 1) * V7X_LANES])[:, :QK_ROPE].astype(BF16)


def _in_proj(x2d, lng, lnb, wc, wkpe, wa, wg, qg, kvg, wq, wuk, wuv, cs, tm):
    rows = x2d.shape[0]
    n_cs = cs.shape[0] // tm
    row = lambda i: (i, 0)
    head_row = lambda i: (0, i, 0)
    return pl.pallas_call(
        _in_proj_kernel,
        grid=(rows // tm,),
        in_specs=[
            pl.BlockSpec((tm, D_MODEL), row),
            _const_spec((1, D_MODEL)), _const_spec((1, D_MODEL)),
            _const_spec((D_MODEL, Q_LORA + KV_LORA)), _const_spec((D_MODEL, 2 * QK_ROPE)),
            _const_spec((D_MODEL, CONV_CH)), _const_spec((D_MODEL, CONV_CH)),
            _const_spec((1, Q_LORA)), _const_spec((1, KV_LORA)),
            _const_spec((Q_LORA, 2 * N_HEADS * QK_NOPE)),
            _const_spec((KV_LORA, N_HEADS * QK_NOPE)), _const_spec((KV_LORA, MLA_WIDTH)),
            pl.BlockSpec((tm, V7X_LANES), lambda i: (i % n_cs, 0)),
        ],
        out_specs=[
            pl.BlockSpec((N_HEADS, tm, QK_DIM), head_row),
            pl.BlockSpec((N_HEADS, QK_DIM, tm), lambda i: (0, 0, i)),
            pl.BlockSpec((N_HEADS, tm, V_DIM), head_row),
            pl.BlockSpec((tm, CONV_CH), row),
        ],
        out_shape=[
            jax.ShapeDtypeStruct((N_HEADS, rows, QK_DIM), BF16),
            jax.ShapeDtypeStruct((N_HEADS, QK_DIM, rows), BF16),
            jax.ShapeDtypeStruct((N_HEADS, rows, V_DIM), BF16),
            jax.ShapeDtypeStruct((rows, CONV_CH), F32),
        ],
        compiler_params=pltpu.CompilerParams(
            dimension_semantics=("arbitrary",), vmem_limit_bytes=V7X_VMEM_LIMIT),
        name="in_proj",
    )(x2d, lng, lnb, wc, wkpe, wa, wg, qg, kvg, wq, wuk, wuv, cs)


def _attention_kernel(q_ref, k_ref, v_ref, km_ref, vm_ref, o_ref, *state):
    i = pl.program_id(2)
    c_exp = (1.0 / math.sqrt(QK_DIM)) * math.log2(math.e)
    chain_rows = ATT_TILE // ATT_CHAINS
    chains = range(ATT_CHAINS)
    m_refs, l_refs, acc_refs = (state[n * ATT_CHAINS:(n + 1) * ATT_CHAINS] for n in range(3))

    def lane_tiles(x):
        return [x[:, t * V7X_LANES:(t + 1) * V7X_LANES] for t in range(x.shape[1] // V7X_LANES)]

    def row_max(x):
        if x.shape[1] % V7X_LANES:
            return jnp.max(x, axis=1, keepdims=True)
        return jnp.max(functools.reduce(jnp.maximum, lane_tiles(x)), axis=1, keepdims=True)

    def lane_partial_sum(x):
        if x.shape[1] % V7X_LANES:
            lane = lax.broadcasted_iota(jnp.int32, (x.shape[0], V7X_LANES), 1)
            return jnp.where(lane == 0, jnp.sum(x, axis=1, keepdims=True), 0.0)
        return functools.reduce(jnp.add, lane_tiles(x))

    def update(h, s, vb, first):
        s_max = jnp.broadcast_to(row_max(s), (s.shape[0], V7X_LANES))
        if first:
            m_new = s_max
        else:
            m_old = m_refs[h][...]
            m_new = jnp.maximum(m_old, s_max)
            alpha = jnp.exp2(c_exp * (m_old - m_new))
        if s.shape[1] % V7X_LANES:
            p = jnp.exp2(c_exp * (s - m_new[:, :s.shape[1]]))
        else:
            p = jnp.concatenate([jnp.exp2(c_exp * (t - m_new)) for t in lane_tiles(s)], axis=1)
        p_sum = lane_partial_sum(p)
        pv = _dot(p.astype(BF16), vb)
        m_refs[h][...] = m_new
        if first:
            l_refs[h][...] = p_sum
            acc_refs[h][...] = pv
        else:
            l_refs[h][...] = alpha * l_refs[h][...] + p_sum
            acc_refs[h][...] = alpha * acc_refs[h][...] + pv

    def chain(x, h):
        return x[h * chain_rows:(h + 1) * chain_rows, :]

    q = q_ref[0]

    start = pl.multiple_of(i * ATT_TILE, ATT_TILE)
    s = _dot(q, jnp.concatenate([k_ref[0, :, pl.ds(start, ATT_TILE)], km_ref[0]], axis=1))
    r = lax.broadcasted_iota(jnp.int32, s.shape, 0)
    c = lax.broadcasted_iota(jnp.int32, s.shape, 1)
    last_visible = jnp.where(c >= ATT_TILE, ATT_TILE + N_META - 1, r)
    s = jnp.where(c <= last_visible, s, -1e30)
    for h in chains:
        cols = (h + 1) * chain_rows
        s_h = jnp.concatenate([chain(s, h)[:, :cols], chain(s, h)[:, ATT_TILE:]], axis=1)
        v_h = jnp.concatenate([v_ref[0, pl.ds(start, cols), :], vm_ref[0]], axis=0)
        update(h, s_h, v_h, True)

    def block(j, _):
        start = pl.multiple_of(j * ATT_TILE, ATT_TILE)
        s = _dot(q, k_ref[0, :, pl.ds(start, ATT_TILE)])
        vb = v_ref[0, pl.ds(start, ATT_TILE), :]
        for h in chains:
            update(h, chain(s, h), vb, False)
        return 0

    lax.fori_loop(0, i, block, 0)

    for h in chains:
        l = jnp.sum(l_refs[h][...], axis=1, keepdims=True)
        o_ref[pl.ds(h * chain_rows, chain_rows), :] = (acc_refs[h][...] / l).astype(BF16)


def _attention(q, k, v, km, vm, batch, seq):
    nq = seq // ATT_TILE
    return pl.pallas_call(
        _attention_kernel,
        grid=(batch, N_HEADS, nq),
        in_specs=[
            pl.BlockSpec((1, ATT_TILE, QK_DIM), lambda b, h, i: (h, b * nq + i, 0)),
            pl.BlockSpec((1, QK_DIM, seq), lambda b, h, i: (h, 0, b)),
            pl.BlockSpec((1, seq, V_DIM), lambda b, h, i: (h, b, 0)),
            pl.BlockSpec((1, QK_DIM, V7X_LANES), lambda b, h, i: (h, 0, 0)),
            pl.BlockSpec((1, V7X_LANES, V_DIM), lambda b, h, i: (h, 0, 0)),
        ],
        out_specs=pl.BlockSpec((ATT_TILE, V_DIM), lambda b, h, i: (b * nq + i, h)),
        out_shape=jax.ShapeDtypeStruct((batch * seq, MLA_WIDTH), BF16),
        scratch_shapes=[pltpu.VMEM((ATT_TILE // ATT_CHAINS, V7X_LANES), F32)] * (3 * ATT_CHAINS),
        compiler_params=pltpu.CompilerParams(
            dimension_semantics=("arbitrary", "arbitrary", "arbitrary"), vmem_limit_bytes=V7X_VMEM_LIMIT),
        name="attention",
    )(q, k, v, km, vm)


def _conv_kernel(cur_ref, prev_ref, meta_ref, w_ref, cb_ref, lng_ref, lnb_ref, o_ref,
                 win_ref, shift_ref, acc_ref):
    i = pl.program_id(1)

    @pl.when(i == 0)
    def _():
        win_ref[0:CONV_HALO - N_META, :] = jnp.zeros((CONV_HALO - N_META, CONV_CH), F32)
        win_ref[CONV_HALO - N_META:CONV_HALO, :] = meta_ref[...]

    @pl.when(i > 0)
    def _():
        win_ref[0:CONV_HALO, :] = prev_ref[...]

    win_ref[CONV_HALO:CONV_HALO + CONV_TILE, :] = cur_ref[...]
    win_ref[CONV_HALO + CONV_TILE:, :] = jnp.zeros((V7X_SUBLANES, CONV_CH), F32)

    base = CONV_HALO - (CONV_W - 1)
    win_rows = CONV_HALO + CONV_TILE
    for shift in range(1, V7X_SUBLANES):
        for r in range(0, win_rows, CONV_ROWS):
            shift_ref[shift - 1, pl.ds(r, CONV_ROWS), :] = win_ref[pl.ds(r + shift, CONV_ROWS), :]

    def window(shift, row, lanes):
        if shift == 0:
            return win_ref[pl.ds(row, CONV_ROWS), lanes]
        return shift_ref[shift - 1, pl.ds(row, CONV_ROWS), lanes]

    for rc in range(CONV_TILE // CONV_ROWS):
        r0 = rc * CONV_ROWS
        for c in range(CONV_CH // CONV_LANES):
            lanes = pl.ds(c * CONV_LANES, CONV_LANES)
            acc = jnp.zeros((CONV_ROWS, CONV_LANES), F32)
            for k in range(CONV_W):
                shift = (base + k) % V7X_SUBLANES
                w_k = jnp.concatenate([w_ref[k * V7X_SUBLANES:(k + 1) * V7X_SUBLANES, lanes]]
                                      * (CONV_ROWS // V7X_SUBLANES), axis=0)
                acc = acc + window(shift, r0 + base + k - shift, lanes) * w_k
            acc_ref[pl.ds(r0, CONV_ROWS), lanes] = acc

    y = _layer_norm(acc_ref[...] + cb_ref[...], lng_ref[...], lnb_ref[...])
    o_ref[...] = (y * jax.nn.sigmoid(y)).astype(BF16)


def _conv(glu, glu_meta, w, cb, lng, lnb, batch, seq):
    nt = seq // CONV_TILE
    per = CONV_TILE // CONV_HALO
    return pl.pallas_call(
        _conv_kernel,
        grid=(batch, nt),
        in_specs=[
            pl.BlockSpec((CONV_TILE, CONV_CH), lambda b, i: (b * nt + i, 0)),
            pl.BlockSpec((CONV_HALO, CONV_CH), lambda b, i: (jnp.maximum((b * nt + i) * per - 1, 0), 0)),
            _const_spec((N_META, CONV_CH)),
            _const_spec((CONV_W * V7X_SUBLANES, CONV_CH)),
            _const_spec((1, CONV_CH)), _const_spec((1, CONV_CH)), _const_spec((1, CONV_CH)),
        ],
        out_specs=pl.BlockSpec((CONV_TILE, CONV_CH), lambda b, i: (b * nt + i, 0)),
        out_shape=jax.ShapeDtypeStruct((batch * seq, CONV_CH), BF16),
        scratch_shapes=[pltpu.VMEM((CONV_HALO + CONV_TILE + V7X_SUBLANES, CONV_CH), F32),
                        pltpu.VMEM((V7X_SUBLANES - 1, CONV_HALO + CONV_TILE, CONV_CH), F32),
                        pltpu.VMEM((CONV_TILE, CONV_CH), F32)],
        compiler_params=pltpu.CompilerParams(
            dimension_semantics=("arbitrary", "arbitrary"), vmem_limit_bytes=V7X_VMEM_LIMIT),
        name="conv",
    )(glu, glu, glu_meta, w, cb, lng, lnb)


def _out_proj_kernel(attn_ref, conv_ref, x_ref, lng_ref, lnb_ref, wo_ref, g1_ref, b1_ref,
                     wrh_ref, wrl_ref, br_ref,
                     h1_ref, idx_ref, rank_ref, gate_ref, cnt_ref, carry_ref):
    step = pl.program_id(0)
    tm = x_ref.shape[0] // OUT_CHAINS

    @pl.when(step == 0)
    def _():
        carry_ref[...] = jnp.zeros_like(carry_ref)

    counts = carry_ref[...]
    for ch in range(OUT_CHAINS):
        counts = _route_chain(pl.ds(ch * tm, tm), ch * tm * TOK_ROWS, counts,
                              attn_ref, conv_ref, x_ref, lng_ref, lnb_ref, wo_ref, g1_ref, b1_ref,
                              wrh_ref, wrl_ref, br_ref, h1_ref, idx_ref, rank_ref, gate_ref)
    carry_ref[...] = counts
    cnt_ref[...] = counts.astype(jnp.int32)


def _route_chain(rows, h1_row0, counts, attn_ref, conv_ref, x_ref, lng_ref, lnb_ref, wo_ref, g1_ref, b1_ref,
                 wrh_ref, wrl_ref, br_ref, h1_ref, idx_ref, rank_ref, gate_ref):
    tm = rows.size
    h0 = _layer_norm(x_ref[rows, :], lng_ref[...], lnb_ref[...])
    mix = _dot(attn_ref[rows, :], wo_ref[0:MLA_WIDTH, :]) + _dot(conv_ref[rows, :], wo_ref[MLA_WIDTH:, :])
    h1 = _layer_norm(DEEPNORM_ALPHA * h0 + mix, g1_ref[...], b1_ref[...])
    _store_token_major(h1_ref, h1_row0, h1)

    hi = h1.astype(BF16)
    lo = (h1 - hi.astype(F32)).astype(BF16)
    logits = (_dot(hi, wrh_ref[...]) + (_dot(hi, wrl_ref[...]) + _dot(lo, wrh_ref[...]))) + br_ref[...]

    lane = lax.broadcasted_iota(jnp.int32, (tm, N_EXPERTS), 1)
    work = logits
    vals, idxs = [], []
    for _ in range(TOP_K):
        mx = jnp.max(work, axis=1, keepdims=True)
        ix = jnp.min(jnp.where(work == mx, lane, N_EXPERTS), axis=1, keepdims=True)
        vals.append(mx)
        idxs.append(ix)
        work = jnp.where(lane == ix, -jnp.inf, work)
    exps = [jnp.exp(v - vals[0]) for v in vals]
    denom = exps[0] + exps[1] + exps[2] + exps[3]

    onehots = [(lane == ix) for ix in idxs]
    chosen = (onehots[0] | onehots[1] | onehots[2] | onehots[3])
    chosen_f = jnp.where(chosen, 1.0, 0.0)
    r = lax.broadcasted_iota(jnp.int32, (tm, tm), 0)
    c = lax.broadcasted_iota(jnp.int32, (tm, tm), 1)
    lower = jnp.where(c < r, 1.0, 0.0).astype(BF16)
    before = _dot(lower, chosen_f.astype(BF16)) + counts

    out_lane = lax.broadcasted_iota(jnp.int32, (tm, V7X_LANES), 1)
    idx_out = jnp.zeros((tm, V7X_LANES), jnp.int32)
    rank_out = jnp.zeros((tm, V7X_LANES), jnp.int32)
    gate_out = jnp.zeros((tm, V7X_LANES), F32)
    for k in range(TOP_K):
        rank_k = jnp.sum(jnp.where(onehots[k], before, 0.0), axis=1, keepdims=True).astype(jnp.int32)
        idx_out = jnp.where(out_lane == k, idxs[k], idx_out)
        rank_out = jnp.where(out_lane == k, rank_k, rank_out)
        gate_out = jnp.where(out_lane == k, exps[k] / denom, gate_out)
    idx_ref[rows, :] = idx_out
    rank_ref[rows, :] = rank_out
    gate_ref[rows, :] = gate_out
    return counts + jnp.sum(chosen_f, axis=0, keepdims=True)


def _out_proj(attn, conv, x2d, lng, lnb, wo, g1, b1, wrh, wrl, br):
    rows = x2d.shape[0]
    tm = OUT_TILE
    row = lambda i: (i, 0)
    return pl.pallas_call(
        _out_proj_kernel,
        grid=(rows // tm,),
        in_specs=[
            pl.BlockSpec((tm, MLA_WIDTH), row), pl.BlockSpec((tm, CONV_CH), row),
            pl.BlockSpec((tm, D_MODEL), row),
            _const_spec((1, D_MODEL)), _const_spec((1, D_MODEL)),
            _const_spec((D_MODEL, D_MODEL)),
            _const_spec((1, D_MODEL)), _const_spec((1, D_MODEL)),
            _const_spec((D_MODEL, N_EXPERTS)), _const_spec((D_MODEL, N_EXPERTS)),
            _const_spec((1, N_EXPERTS)),
        ],
        out_specs=[
            pl.BlockSpec((tm * TOK_ROWS, V7X_LANES), row),
            pl.BlockSpec((tm, V7X_LANES), row), pl.BlockSpec((tm, V7X_LANES), row),
            pl.BlockSpec((tm, V7X_LANES), row),
            _const_spec((1, N_EXPERTS)),
        ],
        out_shape=[
            jax.ShapeDtypeStruct((rows * TOK_ROWS, V7X_LANES), F32),
            jax.ShapeDtypeStruct((rows, V7X_LANES), jnp.int32),
            jax.ShapeDtypeStruct((rows, V7X_LANES), jnp.int32),
            jax.ShapeDtypeStruct((rows, V7X_LANES), F32),
            jax.ShapeDtypeStruct((1, N_EXPERTS), jnp.int32),
        ],
        scratch_shapes=[pltpu.VMEM((1, N_EXPERTS), F32)],
        compiler_params=pltpu.CompilerParams(
            dimension_semantics=("arbitrary",), vmem_limit_bytes=V7X_VMEM_LIMIT),
        name="out_proj_router",
    )(attn, conv, x2d, lng, lnb, wo, g1, b1, wrh, wrl, br)


def _token_copy(src_ref, src_tok, dst_ref, dst_tok, sem, n_tok=1, dst_pitch=TOK_ROWS):
    assert n_tok == 1 or dst_pitch == TOK_ROWS
    rows = n_tok * TOK_ROWS
    src = src_ref.at[pl.ds(pl.multiple_of(src_tok * TOK_ROWS, V7X_SUBLANES), rows), :]
    dst = dst_ref.at[pl.ds(pl.multiple_of(dst_tok * dst_pitch, V7X_SUBLANES), rows), :]
    return pltpu.make_async_copy(src, dst, sem)


def _dispatch_kernel(dest_ref, padrow_ref, npad_ref, h1_ref, xs_ref, zero_ref, sem, zsem):
    step = pl.program_id(0)
    base = step * (DISPATCH_TILE * TOP_K)

    def copies(t):
        return [_token_copy(h1_ref, t, xs_ref, dest_ref[base + t * TOP_K + k], sem) for k in range(TOP_K)]

    def start(t, _):
        for c in copies(t):
            c.start()
        return 0

    def wait(t, _):
        for c in copies(t):
            c.wait()
        return 0

    lax.fori_loop(0, DISPATCH_TILE, start, 0)

    @pl.when(step == 0)
    def _():
        zero_ref[...] = jnp.zeros_like(zero_ref)
        n = npad_ref[0]
        tail_start = npad_ref[1]
        n_tail = (xs_ref.shape[0] // TOK_ROWS - tail_start) // SEG_ALIGN

        def zstart(p, _):
            _token_copy(zero_ref, 0, xs_ref, padrow_ref[p], zsem).start()
            return 0

        def zwait(p, _):
            _token_copy(zero_ref, 0, xs_ref, padrow_ref[p], zsem).wait()
            return 0

        def tstart(b, _):
            _token_copy(zero_ref, 0, xs_ref, tail_start + b * SEG_ALIGN, zsem, SEG_ALIGN).start()
            return 0

        def twait(b, _):
            _token_copy(zero_ref, 0, xs_ref, tail_start + b * SEG_ALIGN, zsem, SEG_ALIGN).wait()
            return 0

        lax.fori_loop(0, n, zstart, 0)
        lax.fori_loop(0, n, zwait, 0)
        lax.fori_loop(0, n_tail, tstart, 0)
        lax.fori_loop(0, n_tail, twait, 0)

    lax.fori_loop(0, DISPATCH_TILE, wait, 0)


def _dispatch(dest_flat, pad_rows, n_pad, h1t, n_rows):
    tokens = h1t.shape[0] // TOK_ROWS
    return pl.pallas_call(
        _dispatch_kernel,
        grid_spec=pltpu.PrefetchScalarGridSpec(
            num_scalar_prefetch=3,
            grid=(tokens // DISPATCH_TILE,),
            in_specs=[pl.BlockSpec((DISPATCH_TILE * TOK_ROWS, V7X_LANES), lambda i, *_: (i, 0))],
            out_specs=pl.BlockSpec(memory_space=pl.ANY),
            scratch_shapes=[pltpu.VMEM((SEG_ALIGN * TOK_ROWS, V7X_LANES), F32),
                            pltpu.SemaphoreType.DMA, pltpu.SemaphoreType.DMA],
        ),
        out_shape=jax.ShapeDtypeStruct((n_rows * TOK_ROWS, V7X_LANES), F32),
        compiler_params=pltpu.CompilerParams(
            dimension_semantics=("arbitrary",), has_side_effects=True),
        name="dispatch",
    )(dest_flat, pad_rows, n_pad, h1t)


N_FF = D_FF // FF_TILE
SUB = SEG_ALIGN
N_SUB = SUPER_ROWS // SUB
PREFETCH_SLOTS = -(-N_SUB // N_FF)
OUT_SLOTS = MM_ROWS // SUB


def _experts_kernel(st_e_ref, st_start_ref, st_rows_ref, n_used_ref,
                    xs_ref, w1g_ref, w1u_ref, b1g_ref, b1u_ref, w2_ref, b2_ref,
                    ys_ref,
                    xb_ref, acc_ref, wg_ref, wu_ref, wd_ref, xstage_ref, ostage_ref, xsem, osem):
    s = pl.program_id(0)
    j = pl.program_id(1)
    parity = s % 2
    rows = st_rows_ref[s]
    start = st_start_ref[s]
    n_blk = rows // SUB
    next_start = st_start_ref[s + 1]
    next_blk = st_rows_ref[s + 1] // SUB

    def x_copy(tok0, slot):
        return _token_copy(xs_ref, tok0, xstage_ref.at[slot], 0, xsem.at[slot], SUB)

    def y_copy(slot, tok0):
        return _token_copy(ostage_ref.at[slot], 0, ys_ref, tok0, osem.at[slot], SUB)

    def convert(slot, par, blk):
        off = pl.multiple_of(blk * SUB, SUB)
        for c in range(TOK_ROWS):
            xb_ref[par, pl.ds(off, SUB), c * V7X_LANES:(c + 1) * V7X_LANES] = (
                _load_token_major(xstage_ref.at[slot], 0, SUB, c).astype(BF16))

    @pl.when((s == 0) & (j == 0))
    def _():
        ostage_ref[0] = jnp.zeros(ostage_ref.shape[1:], F32)
        tail_start = n_used_ref[1]
        n_tail = (ys_ref.shape[0] // TOK_ROWS - tail_start) // SUB

        def tstart(b, _):
            y_copy(0, tail_start + b * SUB).start()
            return 0

        def twait(b, _):
            y_copy(0, tail_start + b * SUB).wait()
            return 0

        lax.fori_loop(0, n_tail, tstart, 0)
        lax.fori_loop(0, n_tail, twait, 0)

        def first(b, _):
            x_copy(start + b * SUB, 0).start()
            x_copy(start + b * SUB, 0).wait()
            convert(0, 0, b)
            return 0

        lax.fori_loop(0, n_blk, first, 0)

    for p in range(PREFETCH_SLOTS):
        @pl.when(j * PREFETCH_SLOTS + p < next_blk)
        def _(p=p):
            x_copy(next_start + (j * PREFETCH_SLOTS + p) * SUB, p).start()

    def compute(last):
        def cast_weights():
            wg_ref[...] = w1g_ref[0].astype(BF16)
            wu_ref[...] = w1u_ref[0].astype(BF16)
            wd_ref[...] = w2_ref[0].astype(BF16)

        bg = b1g_ref[0]
        bu = b1u_ref[0]
        b2 = b2_ref[0]

        def chunk(row0, m):
            off = row0 if isinstance(row0, int) else pl.multiple_of(row0, SUB)
            xb = xb_ref[parity, pl.ds(off, m), :]
            g = _dot(xb, wg_ref[...]) + bg
            u = _dot(xb, wu_ref[...]) + bu
            g = jnp.minimum(g, SWIGLU_LIMIT)
            u = jnp.clip(u, -SWIGLU_LIMIT, SWIGLU_LIMIT)
            act = g * jax.nn.sigmoid(SWIGLU_ALPHA * g) * (u + 1.0)
            y = _dot(act.astype(BF16), wd_ref[...])
            if not last:
                acc_ref[pl.ds(off, m), :] += y
                return
            y = acc_ref[pl.ds(off, m), :] + y + b2
            for i in range(m // SUB):
                blk = row0 // SUB + i
                slot = blk % OUT_SLOTS

                @pl.when(blk >= OUT_SLOTS)
                def _(blk=blk, slot=slot):
                    y_copy(slot, start + (blk - OUT_SLOTS) * SUB).wait()

                _store_token_major(ostage_ref.at[slot], 0, y[i * SUB:(i + 1) * SUB, :])
                y_copy(slot, start + blk * SUB).start()

        n_big = rows // MM_ROWS

        def big(r, _):
            chunk(r * MM_ROWS, MM_ROWS)
            return 0

        @pl.when(n_big >= 2)
        def _():
            cast_weights()
            chunk(0, MM_ROWS)
            chunk(MM_ROWS, MM_ROWS)

        @pl.when(n_big == 1)
        def _():
            cast_weights()
            chunk(0, MM_ROWS)

        @pl.when(n_big == 0)
        def _():
            cast_weights()

        lax.fori_loop(2, n_big, big, 0)
        done = n_big * MM_ROWS
        for m in range(SUB, MM_ROWS, SUB):
            @pl.when(rows - done == m)
            def _(m=m):
                chunk(done, m)

    @pl.when((j == 0) & (rows > 0))
    def _():
        def zero(b, _):
            acc_ref[pl.ds(pl.multiple_of(b * SUB, SUB), SUB), :] = jnp.zeros((SUB, D_MODEL), F32)
            return 0

        lax.fori_loop(0, n_blk, zero, 0)

    @pl.when((j < N_FF - 1) & (rows > 0))
    def _():
        compute(False)

    @pl.when((j == N_FF - 1) & (rows > 0))
    def _():
        compute(True)

    for p in range(PREFETCH_SLOTS):
        @pl.when(j * PREFETCH_SLOTS + p < next_blk)
        def _(p=p):
            blk = j * PREFETCH_SLOTS + p
            x_copy(next_start + blk * SUB, p).wait()
            convert(p, 1 - parity, blk)

    def drain(tok0, blocks):
        for slot in range(OUT_SLOTS):
            @pl.when(slot < blocks)
            def _(slot=slot):
                y_copy(slot, tok0 + _last_block_on_slot(blocks, slot) * SUB).wait()

    @pl.when((j == 0) & (s > 0))
    def _():
        prev = jnp.maximum(s - 1, 0)
        drain(st_start_ref[prev], st_rows_ref[prev] // SUB)

    @pl.when((j == N_FF - 1) & (s == pl.num_programs(0) - 1))
    def _():
        drain(start, n_blk)


def _last_block_on_slot(n_blk, slot):
    return slot + OUT_SLOTS * ((n_blk - 1 - slot) // OUT_SLOTS)


def _experts(st_e, st_start, st_rows, n_used, xs, w1, b1, w2, b2, n_super):
    n_rows = xs.shape[0] // TOK_ROWS

    def ff(s, j, n_used_ref):
        return jnp.where(s < n_used_ref[0], j, N_FF - 1)

    w1g_map = lambda s, j, e, st, rw, nu: (e[s], 0, ff(s, j, nu))
    w1u_map = lambda s, j, e, st, rw, nu: (e[s], 0, N_FF + ff(s, j, nu))
    w2_map = lambda s, j, e, st, rw, nu: (e[s], ff(s, j, nu), 0)
    b2_map = lambda s, j, e, st, rw, nu: (e[s], 0, 0)
    return pl.pallas_call(
        _experts_kernel,
        grid_spec=pltpu.PrefetchScalarGridSpec(
            num_scalar_prefetch=4,
            grid=(n_used[0], N_FF),
            in_specs=[
                pl.BlockSpec(memory_space=pl.ANY),
                pl.BlockSpec((1, D_MODEL, FF_TILE), w1g_map),
                pl.BlockSpec((1, D_MODEL, FF_TILE), w1u_map),
                pl.BlockSpec((1, 1, FF_TILE), w1g_map),
                pl.BlockSpec((1, 1, FF_TILE), w1u_map),
                pl.BlockSpec((1, FF_TILE, D_MODEL), w2_map),
                pl.BlockSpec((1, 1, D_MODEL), b2_map),
            ],
            out_specs=pl.BlockSpec(memory_space=pl.ANY),
            scratch_shapes=[
                pltpu.VMEM((2, SUPER_ROWS, D_MODEL), BF16),
                pltpu.VMEM((SUPER_ROWS, D_MODEL), F32),
                pltpu.VMEM((D_MODEL, FF_TILE), BF16),
                pltpu.VMEM((D_MODEL, FF_TILE), BF16),
                pltpu.VMEM((FF_TILE, D_MODEL), BF16),
                pltpu.VMEM((PREFETCH_SLOTS, SUB * TOK_ROWS, V7X_LANES), F32),
                pltpu.VMEM((OUT_SLOTS, SUB * TOK_ROWS, V7X_LANES), F32),
                pltpu.SemaphoreType.DMA((PREFETCH_SLOTS,)),
                pltpu.SemaphoreType.DMA((OUT_SLOTS,)),
            ],
        ),
        out_shape=jax.ShapeDtypeStruct((n_rows * TOK_ROWS, V7X_LANES), F32),
        compiler_params=pltpu.CompilerParams(
            dimension_semantics=("arbitrary", "arbitrary"), vmem_limit_bytes=V7X_VMEM_LIMIT,
            has_side_effects=True),
        name="experts",
    )(st_e, st_start, st_rows, n_used, xs, w1, w1, b1, b1, w2, b2)


def _combine_kernel(dest_ref, ys_ref, gate_ref, h1_ref, g2_ref, b2_ref, o_ref, buf_ref, sem):
    step = pl.program_id(0)
    n_steps = pl.num_programs(0)
    tm = COMBINE_TILE

    def copies(at_step, slot, t):
        base = at_step * (tm * TOP_K)
        return [_token_copy(ys_ref, dest_ref[base + t * TOP_K + k], buf_ref.at[slot, k], t, sem.at[slot],
                            dst_pitch=PADDED_PITCH)
                for k in range(TOP_K)]

    def gather(at_step, slot):
        def start(t, _):
            for c in copies(at_step, slot, t):
                c.start()
            return 0

        lax.fori_loop(0, tm, start, 0)

    @pl.when(step == 0)
    def _():
        gather(0, 0)

    @pl.when(step + 1 < n_steps)
    def _():
        gather(step + 1, (step + 1) % 2)

    slot = step % 2

    def wait(t, _):
        for c in copies(step, slot, t):
            c.wait()
        return 0

    lax.fori_loop(0, tm, wait, 0)

    gates = gate_ref[...]
    for c in range(TOK_ROWS):
        z = DEEPNORM_ALPHA * _load_token_major(h1_ref, 0, tm, c)
        for k in range(TOP_K):
            z = z + _load_token_major(buf_ref.at[slot, k], 0, tm, c, PADDED_PITCH) * gates[:, k:k + 1]
        o_ref[:, c * V7X_LANES:(c + 1) * V7X_LANES] = z
    o_ref[...] = _layer_norm(o_ref[...], g2_ref[...], b2_ref[...])


def _combine(dest_flat, ys, gates, h1t, g2, b2):
    tokens = h1t.shape[0] // TOK_ROWS
    tm = COMBINE_TILE
    row = lambda i, *_: (i, 0)
    return pl.pallas_call(
        _combine_kernel,
        grid_spec=pltpu.PrefetchScalarGridSpec(
            num_scalar_prefetch=1,
            grid=(tokens // tm,),
            in_specs=[
                pl.BlockSpec(memory_space=pl.ANY),
                pl.BlockSpec((tm, V7X_LANES), row),
                pl.BlockSpec((tm * TOK_ROWS, V7X_LANES), row),
                pl.BlockSpec((1, D_MODEL), lambda i, *_: (0, 0)),
                pl.BlockSpec((1, D_MODEL), lambda i, *_: (0, 0)),
            ],
            out_specs=pl.BlockSpec((tm, D_MODEL), row),
            scratch_shapes=[pltpu.VMEM((2, TOP_K, tm * PADDED_PITCH, V7X_LANES), F32),
                            pltpu.SemaphoreType.DMA((2,))],
        ),
        out_shape=jax.ShapeDtypeStruct((tokens, D_MODEL), F32),
        compiler_params=pltpu.CompilerParams(
            dimension_semantics=("arbitrary",), vmem_limit_bytes=V7X_VMEM_LIMIT),
        name="combine",
    )(dest_flat, ys, gates, h1t, g2, b2)


def _rotate_half_cols(w):
    half = QK_ROPE // 2
    return jnp.concatenate([-w[..., half:], w[..., :half]], axis=-1)


def _rope_table(length):
    inv_freq = 1.0 / (ROPE_THETA ** (jnp.arange(0, QK_ROPE, 2, dtype=F32) / QK_ROPE))
    freqs = jnp.arange(length, dtype=F32)[:, None] * inv_freq[None, :]
    emb = jnp.concatenate([freqs, freqs], axis=-1)
    return jnp.concatenate([jnp.cos(emb), jnp.sin(emb)], axis=-1)


def _routing_plan(idx, rank, counts, n_super):
    experts = jnp.arange(N_EXPERTS, dtype=jnp.int32)

    def lookup(table, i):
        return jnp.sum(jnp.where(i[..., None] == experts, table, 0), axis=-1)

    def bucket(cum, i):
        return jnp.minimum(jnp.sum((cum <= i[..., None]).astype(jnp.int32), axis=-1), N_EXPERTS - 1)

    counts = counts.astype(jnp.int32)
    padded = (counts + SEG_ALIGN - 1) // SEG_ALIGN * SEG_ALIGN
    pad_end = jnp.cumsum(padded)
    pad_start = pad_end - padded
    dest = (lookup(pad_start, idx) + rank).reshape(-1).astype(jnp.int32)

    n_padmax = N_EXPERTS * SEG_ALIGN
    padcnt = padded - counts
    padcum = jnp.cumsum(padcnt)
    p = jnp.arange(n_padmax, dtype=jnp.int32)
    pe = bucket(padcum, p)
    pad_rows = lookup(pad_start + counts - (padcum - padcnt), pe) + p
    n_pad = jnp.stack([padcum[-1], pad_end[-1]]).astype(jnp.int32)
    pad_rows = jnp.where(p < n_pad[0], pad_rows, 0).astype(jnp.int32)

    n_st = (padded + SUPER_ROWS - 1) // SUPER_ROWS
    st_cum = jnp.cumsum(n_st)
    n_used = jnp.stack([st_cum[-1], pad_end[-1]]).astype(jnp.int32)
    s = jnp.arange(n_super + 1, dtype=jnp.int32)
    s_eff = jnp.minimum(s, n_used[0] - 1)
    se = bucket(st_cum, s_eff).astype(jnp.int32)
    local = s_eff - lookup(st_cum - n_st, se)
    st_start = (lookup(pad_start, se) + local * SUPER_ROWS).astype(jnp.int32)
    st_rows = jnp.clip(lookup(padded, se) - local * SUPER_ROWS, 0, SUPER_ROWS)
    st_rows = jnp.where(s < n_used[0], st_rows, 0).astype(jnp.int32)
    return dest, pad_rows, n_pad, se, st_start, st_rows, n_used


def kernel(x, meta_tokens, ln_in_g, ln_in_b, w_in, q_norm_g, w_uq, kv_norm_g, w_uk, w_uv, conv_dw_w,
           conv_dw_b, conv_ln_g, conv_ln_b, w_out, ln1_g, ln1_b, w_router, b_router, w_mlp1, b_mlp1,
           w_mlp2, b_mlp2, ln2_g, ln2_b):
    batch, seq, _ = x.shape
    tokens = batch * seq
    row2 = lambda a: a.reshape(1, -1)

    wi = w_in[0]
    s_kpe = Q_LORA + KV_LORA
    s_conv = s_kpe + QK_ROPE
    kpe_w = wi[:, s_kpe:s_conv]
    w_proj = (wi[:, :s_kpe].astype(BF16),
              jnp.concatenate([kpe_w, _rotate_half_cols(kpe_w)], axis=1).astype(BF16),
              wi[:, s_conv:s_conv + CONV_CH].astype(BF16),
              wi[:, s_conv + CONV_CH:].astype(BF16))
    wq3 = w_uq[0].reshape(Q_LORA, N_HEADS, QK_DIM)
    wq_nope = wq3[:, :, :QK_NOPE].reshape(Q_LORA, N_HEADS * QK_NOPE)
    wq_pe = wq3[:, :, QK_NOPE:]
    wq_pr = jnp.concatenate([wq_pe, _rotate_half_cols(wq_pe)], axis=-1).reshape(Q_LORA, N_HEADS * 2 * QK_ROPE)
    wq = jnp.concatenate([wq_nope, wq_pr], axis=1).astype(BF16)
    wuk = w_uk[0].astype(BF16)
    wuv = w_uv[0].astype(BF16)
    wo = w_out[0].astype(BF16)
    wr = w_router[0]
    wr_hi = wr.astype(BF16)
    wr_lo = (wr - wr_hi.astype(F32)).astype(BF16)
    cs = _rope_table(N_META + seq)
    conv_w = jnp.repeat(conv_dw_w[0], V7X_SUBLANES, axis=0)

    x2d = x.reshape(tokens, D_MODEL)
    proj_args = (row2(ln_in_g), row2(ln_in_b), *w_proj, row2(q_norm_g[0]), row2(kv_norm_g[0]), wq, wuk, wuv)

    _, k_meta, v_meta, glu_meta = _in_proj(meta_tokens, *proj_args, cs[:N_META], N_META)
    q, k, v, glu = _in_proj(x2d, *proj_args, cs[N_META:], ROW_TILE)
    k_meta = jnp.pad(k_meta, ((0, 0), (0, 0), (0, V7X_LANES - N_META)))
    v_meta = jnp.pad(v_meta, ((0, 0), (0, V7X_LANES - N_META), (0, 0)))
    attn = _attention(q, k, v, k_meta, v_meta, batch, seq)
    conv = _conv(glu, glu_meta, conv_w, row2(conv_dw_b[0]), row2(conv_ln_g[0]), row2(conv_ln_b[0]), batch, seq)

    h1t, idx, rank, gates, counts = _out_proj(
        attn, conv, x2d, row2(ln_in_g), row2(ln_in_b), wo, row2(ln1_g[0]), row2(ln1_b[0]),
        wr_hi, wr_lo, row2(b_router[0]))

    n_assign = tokens * TOP_K
    n_rows = n_assign + N_EXPERTS * SEG_ALIGN
    n_super = N_EXPERTS + -(-n_assign // SUPER_ROWS)
    dest, pad_rows, n_pad, st_e, st_start, st_rows, n_used = _routing_plan(
        idx[:, :TOP_K], rank[:, :TOP_K], counts[0], n_super)

    xs = _dispatch(dest, pad_rows, n_pad, h1t, n_rows)
    ys = _experts(st_e, st_start, st_rows, n_used, xs, w_mlp1[0], b_mlp1[0].reshape(N_EXPERTS, 1, 2 * D_FF),
                  w_mlp2[0], b_mlp2[0].reshape(N_EXPERTS, 1, D_MODEL), n_super)
    out = _combine(dest, ys, gates, h1t, row2(ln2_g[0]), row2(ln2_b[0]))
    return out.reshape(batch, seq, D_MODEL)
```
